```python
import math
import jax
import jax.numpy as jnp
from jax import lax
import numpy as np

D_MODEL = 1024
BATCH = 8
SEQ = 2048
DEPTH = 1
DEC_BATCH = 32
DEC_SEQ = 4
PAST_LEN = 8192
PAGE_SIZE = 128

A_HEADS = 8
A_KV_HEADS = 2
A_HEAD_DIM = 64
IDX_HEADS = 4
IDX_DIM = 64
TOPK_MAX = 256
B_HEADS = 4
B_HEAD_DIM = 64
MEM_TOKENS = 256
MEM_HEADS = 4
MEM_HEAD_DIM = 128
D_FF = 2816
CONV_W = 3
NUM_BUCKETS = 32
MAX_DISTANCE = 128
Q_BLOCK = 128
EPS = 1e-6

A_Q = A_HEADS * A_HEAD_DIM
A_KV = A_KV_HEADS * A_HEAD_DIM
IDX_Q = IDX_HEADS * IDX_DIM
B_QK = B_HEADS * 2 * B_HEAD_DIM
B_V = B_HEADS * 2 * B_HEAD_DIM
GROUP_WIDTHS = (A_Q, A_KV, A_KV, IDX_Q, IDX_DIM, IDX_HEADS, B_QK, B_QK, B_V)
D_IN = sum(GROUP_WIDTHS)
MIX_WIDTH = A_Q + B_V
N_BIAS_HEADS = A_HEADS + B_HEADS
MEM_W = MEM_HEADS * MEM_HEAD_DIM

kernel_name = 'hybrid_dsa_diffattn_decoder_step'


def rmsnorm(x, g):
    xf = x.astype(jnp.float32)
    y = xf * lax.rsqrt(jnp.mean(xf * xf, axis=-1, keepdims=True) + EPS)
    return (y * g.astype(jnp.float32)).astype(x.dtype)


def rel_bucket(dist):
    n = jnp.maximum(dist, 0)
    max_exact = NUM_BUCKETS // 2
    nf = jnp.maximum(n, 1).astype(jnp.float32)
    log_b = jnp.log(nf / max_exact) / math.log(MAX_DISTANCE / max_exact) * (NUM_BUCKETS - max_exact)
    large = jnp.minimum(max_exact + log_b.astype(jnp.int32), NUM_BUCKETS - 1)
    return jnp.where(n < max_exact, n, large)


def project_groups(xn, w_in, a_qn, a_kn, b_qn, b_kn):
    b, t, _ = xn.shape
    proj = xn @ w_in
    cuts, acc = [], 0
    for w in GROUP_WIDTHS[:-1]:
        acc += w
        cuts.append(acc)
    qa, ka, va, qi, ki, wi, qb, kb, vb = jnp.split(proj, cuts, axis=-1)
    qa = rmsnorm(qa.reshape(b, t, A_HEADS, A_HEAD_DIM), a_qn)
    ka = rmsnorm(ka.reshape(b, t, A_KV_HEADS, A_HEAD_DIM), a_kn)
    va = va.reshape(b, t, A_KV_HEADS, A_HEAD_DIM)
    qi = qi.reshape(b, t, IDX_HEADS, IDX_DIM)
    qb = rmsnorm(qb.reshape(b, t, B_HEADS, 2, B_HEAD_DIM), b_qn)
    kb = rmsnorm(kb.reshape(b, t, B_HEADS, 2, B_HEAD_DIM), b_kn)
    vb = vb.reshape(b, t, B_HEADS, 2 * B_HEAD_DIM)
    return qa, ka, va, qi, ki, wi, qb, kb, vb


def indexer_scores(qi, wi, ki):
    dots = jnp.einsum('bqhd,bkd->bqhk', qi, ki).astype(jnp.float32) * (IDX_DIM ** -0.5)
    return jnp.einsum('bqhk,bqh->bqk', jax.nn.relu(dots), wi.astype(jnp.float32) * (IDX_HEADS ** -0.5))


def select_keys(scores, qpos, n_keys, topk):
    kpos = jnp.arange(n_keys)
    causal = kpos[None, None, :] <= qpos[None, :, None]
    _, idx = lax.top_k(jnp.where(causal, scores, -jnp.inf), topk)
    valid = idx <= qpos[None, :, None]
    return idx, valid


def dsa_core(q, k_sel, v_sel, valid, dist, bias_a):
    b, nq, kk = valid.shape
    rep = A_HEADS // A_KV_HEADS
    qg = q.reshape(b, nq, A_KV_HEADS, rep, A_HEAD_DIM)
    logits = jnp.einsum('bqgrd,bqkgd->bqgrk', qg, k_sel).astype(jnp.float32) * (A_HEAD_DIM ** -0.5)
    bias = bias_a[rel_bucket(dist)].astype(jnp.float32)
    bias = bias.reshape(b, nq, kk, A_KV_HEADS, rep).transpose(0, 1, 3, 4, 2)
    logits = jnp.where(valid[:, :, None, None, :], logits + bias, -jnp.inf)
    p = jax.nn.softmax(logits, axis=-1).astype(v_sel.dtype)
    o = jnp.einsum('bqgrk,bqkgd->bqgrd', p, v_sel)
    return o.reshape(b, nq, A_Q)


def dsa_prompt(qa, ka, va, qi, ki, wi, bias_a):
    b, s = qa.shape[:2]
    topk = min(TOPK_MAX, s // 4)
    b_ix = jnp.arange(b)[:, None, None]

    def block(i):
        q0 = i * Q_BLOCK
        qpos = q0 + jnp.arange(Q_BLOCK)
        sl = lambda a: lax.dynamic_slice_in_dim(a, q0, Q_BLOCK, axis=1)
        idx, valid = select_keys(indexer_scores(sl(qi), sl(wi), ki), qpos, s, topk)
        return dsa_core(sl(qa), ka[b_ix, idx], va[b_ix, idx], valid, qpos[None, :, None] - idx, bias_a)

    out = lax.map(block, jnp.arange(s // Q_BLOCK))
    return out.transpose(1, 0, 2, 3).reshape(b, s, A_Q)


def gather_pages(pool, page_table):
    rows = pool[page_table]
    return rows.reshape((page_table.shape[0], -1) + pool.shape[2:])


def gather_selected(pool, page_table, new_rows, idx, past):
    page = pool.shape[1]
    b_ix = jnp.arange(idx.shape[0])[:, None, None]
    ip = jnp.minimum(idx, past - 1)
    from_pool = pool[page_table[b_ix, ip // page], ip % page].astype(new_rows.dtype)
    from_new = new_rows[b_ix, jnp.clip(idx - past, 0, new_rows.shape[1] - 1)]
    in_past = (idx < past).reshape(idx.shape + (1,) * (from_pool.ndim - idx.ndim))
    return jnp.where(in_past, from_pool, from_new)


def dsa_sample(qa, ka, va, qi, ki, wi, cache_k, cache_v, cache_ik, page_table, bias_a):
    t = qa.shape[1]
    past = page_table.shape[1] * cache_k.shape[1]
    n_keys = past + t
    qpos = past + jnp.arange(t)
    ki_all = jnp.concatenate([gather_pages(cache_ik, page_table).astype(ki.dtype), ki], axis=1)
    idx, valid = select_keys(indexer_scores(qi, wi, ki_all), qpos, n_keys, min(TOPK_MAX, n_keys // 4))
    k_sel = gather_selected(cache_k, page_table, ka, idx, past)
    v_sel = gather_selected(cache_v, page_table, va, idx, past)
    return dsa_core(qa, k_sel, v_sel, valid, qpos[None, :, None] - idx, bias_a)


def diff_lambda_value(lam_p, lam_init):
    lp = lam_p.astype(jnp.float32)
    return jnp.exp(jnp.sum(lp[0] * lp[1])) - jnp.exp(jnp.sum(lp[2] * lp[3])) + lam_init


def diff_core(q, k, v, qpos, kpos, bias_b, lam, lam_init, subln_g):
    b, nq = q.shape[:2]
    logits = jnp.einsum('bqhcd,bkhcd->bhcqk', q, k).astype(jnp.float32) * (B_HEAD_DIM ** -0.5)
    bias = bias_b[rel_bucket(qpos[:, None] - kpos[None, :])].astype(jnp.float32)
    logits = logits + bias.transpose(2, 0, 1)[None, :, None]
    causal = kpos[None, :] <= qpos[:, None]
    p = jax.nn.softmax(jnp.where(causal, logits, -jnp.inf), axis=-1)
    w = (p[:, :, 0] - lam * p[:, :, 1]).astype(v.dtype)
    o = jnp.einsum('bhqk,bkhe->bqhe', w, v)
    o = rmsnorm(o, subln_g) * (1.0 - lam_init)
    return o.reshape(b, nq, B_V)


def diff_prompt(qb, kb, vb, bias_b, lam, lam_init, subln_g):
    b, s = qb.shape[:2]
    kpos = jnp.arange(s)

    def block(i):
        q0 = i * Q_BLOCK
        qblk = lax.dynamic_slice_in_dim(qb, q0, Q_BLOCK, axis=1)
        return diff_core(qblk, kb, vb, q0 + jnp.arange(Q_BLOCK), kpos, bias_b, lam, lam_init, subln_g)

    out = lax.map(block, jnp.arange(s // Q_BLOCK))
    return out.transpose(1, 0, 2, 3).reshape(b, s, B_V)


def diff_sample(qb, kb, vb, cache_k, cache_v, page_table, bias_b, lam, lam_init, subln_g):
    t = qb.shape[1]
    past = page_table.shape[1] * cache_k.shape[1]
    k_all = jnp.concatenate([gather_pages(cache_k, page_table).astype(kb.dtype), kb], axis=1)
    v_all = jnp.concatenate([gather_pages(cache_v, page_table).astype(vb.dtype), vb], axis=1)
    return diff_core(qb, k_all, v_all, past + jnp.arange(t), jnp.arange(past + t), bias_b, lam, lam_init, subln_g)


def memory_kv(mem, g_src, w_kv, k_norm_g):
    b, m, _ = mem.shape
    kv = rmsnorm(mem, g_src) @ w_kv
    k, v = jnp.split(kv, 2, axis=-1)
    k = rmsnorm(k.reshape(b, m, MEM_HEADS, MEM_HEAD_DIM), k_norm_g)
    return k, v.reshape(b, m, MEM_HEADS, MEM_HEAD_DIM)


def memory_attend(xn, w_q, q_norm_g, mk, mv, w_o):
    b, t, _ = xn.shape
    q = rmsnorm((xn @ w_q).reshape(b, t, MEM_HEADS, MEM_HEAD_DIM), q_norm_g)
    logits = jnp.einsum('bqhd,bmhd->bhqm', q, mk.astype(q.dtype)).astype(jnp.float32) * (MEM_HEAD_DIM ** -0.5)
    p = jax.nn.softmax(logits, axis=-1).astype(q.dtype)
    o = jnp.einsum('bhqm,bmhd->bqhd', p, mv.astype(q.dtype))
    return o.reshape(b, t, MEM_W) @ w_o


def conv_ffn(xn, conv_state, w_up, w_gate, conv_w, conv_b, w_down):
    t = xn.shape[1]
    g = xn @ w_gate
    gb = jnp.concatenate([conv_state.astype(g.dtype), g], axis=1)
    gc = conv_b + sum(conv_w[j] * gb[:, j:j + t] for j in range(CONV_W))
    y = (jax.nn.silu(gc) * (xn @ w_up)) @ w_down
    return y, gb[:, t:]


def setup_inputs(seed: int = 0) -> dict:
    key = jax.random.key(seed)
    keys = list(jax.random.split(key, 48))

    def nrm(shape, scale):
        return scale * jax.random.normal(keys.pop(), shape, jnp.float32)

    def gain(shape):
        return 1.0 + 0.01 * jax.random.normal(keys.pop(), shape, jnp.float32)

    n_pages = PAST_LEN // PAGE_SIZE
    n_pool = (DEC_BATCH * n_pages * 5) // 4
    page_table = jax.random.permutation(keys.pop(), n_pool)[: DEC_BATCH * n_pages].reshape(DEC_BATCH, n_pages).astype(jnp.int32)
    pool = (DEPTH, n_pool, PAGE_SIZE)
    return {
        'x_prompt': nrm((BATCH, SEQ, D_MODEL), 1.0),
        'x_sample': nrm((DEC_BATCH, DEC_SEQ, D_MODEL), 1.0),
        'mem_prompt': nrm((BATCH, MEM_TOKENS, D_MODEL), 1.0),
        'cache_a_k': nrm(pool + (A_KV_HEADS, A_HEAD_DIM), 1.0),
        'cache_a_v': nrm(pool + (A_KV_HEADS, A_HEAD_DIM), 1.0),
        'cache_idx_k': nrm(pool + (IDX_DIM,), 1.0),
        'cache_b_k': nrm(pool + (B_HEADS, 2, B_HEAD_DIM), 1.0),
        'cache_b_v': nrm(pool + (B_HEADS, 2 * B_HEAD_DIM), 1.0),
        'cache_mem_k': nrm((DEPTH, DEC_BATCH, MEM_TOKENS, MEM_HEADS, MEM_HEAD_DIM), 1.0),
        'cache_mem_v': nrm((DEPTH, DEC_BATCH, MEM_TOKENS, MEM_HEADS, MEM_HEAD_DIM), 1.0),
        'state_ffn_conv': nrm((DEPTH, DEC_BATCH, CONV_W - 1, D_FF), 1.0),
        'page_table': page_table,
        'rel_bias': nrm((NUM_BUCKETS, N_BIAS_HEADS), 0.5),
        'norm_mix': gain((DEPTH, D_MODEL)),
        'w_in': nrm((DEPTH, D_MODEL, D_IN), D_MODEL ** -0.5),
        'a_q_norm': gain((DEPTH, A_HEAD_DIM)),
        'a_k_norm': gain((DEPTH, A_HEAD_DIM)),
        'b_q_norm': gain((DEPTH, B_HEAD_DIM)),
        'b_k_norm': gain((DEPTH, B_HEAD_DIM)),
        'diff_lambda': nrm((DEPTH, 4, B_HEAD_DIM), 0.1),
        'diff_subln': gain((DEPTH, 2 * B_HEAD_DIM)),
        'w_out': nrm((DEPTH, MIX_WIDTH, D_MODEL), MIX_WIDTH ** -0.5),
        'norm_mem_x': gain((DEPTH, D_MODEL)),
        'norm_mem_src': gain((DEPTH, D_MODEL)),
        'w_mem_q': nrm((DEPTH, D_MODEL, MEM_W), D_MODEL ** -0.5),
        'w_mem_kv': nrm((DEPTH, D_MODEL, 2 * MEM_W), D_MODEL ** -0.5),
        'mem_q_norm': gain((DEPTH, MEM_HEAD_DIM)),
        'mem_k_norm': gain((DEPTH, MEM_HEAD_DIM)),
        'w_mem_o': nrm((DEPTH, MEM_W, D_MODEL), MEM_W ** -0.5),
        'norm_ffn': gain((DEPTH, D_MODEL)),
        'w_up': nrm((DEPTH, D_MODEL, D_FF), D_MODEL ** -0.5),
        'w_gate': nrm((DEPTH, D_MODEL, D_FF), D_MODEL ** -0.5),
        'ffn_conv_w': nrm((DEPTH, CONV_W, D_FF), CONV_W ** -0.5),
        'ffn_conv_b': nrm((DEPTH, D_FF), 0.01),
        'w_down': nrm((DEPTH, D_FF, D_MODEL), D_FF ** -0.5),
    }


def reference(x_prompt, x_sample, mem_prompt, cache_a_k, cache_a_v, cache_idx_k, cache_b_k, cache_b_v,
              cache_mem_k, cache_mem_v, state_ffn_conv, page_table, rel_bias, norm_mix, w_in,
              a_q_norm, a_k_norm, b_q_norm, b_k_norm, diff_lambda, diff_subln, w_out,
              norm_mem_x, norm_mem_src, w_mem_q, w_mem_kv, mem_q_norm, mem_k_norm, w_mem_o,
              norm_ffn, w_up, w_gate, ffn_conv_w, ffn_conv_b, w_down):
    bias_a = rel_bias[:, :A_HEADS]
    bias_b = rel_bias[:, A_HEADS:]
    yp, ys = x_prompt, x_sample
    pak, pav, pik, pbk, pbv, pmk, pmv, pcv = [], [], [], [], [], [], [], []
    sak, sav, sik, sbk, sbv, scv = [], [], [], [], [], []
    for l in range(DEPTH):
        lam_init = 0.8 - 0.6 * math.exp(-0.3 * l)
        lam = diff_lambda_value(diff_lambda[l], lam_init)
        qk_gains = (a_q_norm[l], a_k_norm[l], b_q_norm[l], b_k_norm[l])

        qa, ka, va, qi, ki, wi, qb, kb, vb = project_groups(rmsnorm(yp, norm_mix[l]), w_in[l], *qk_gains)
        mix = jnp.concatenate([dsa_prompt(qa, ka, va, qi, ki, wi, bias_a),
                               diff_prompt(qb, kb, vb, bias_b, lam, lam_init, diff_subln[l])], axis=-1)
        yp = yp + mix @ w_out[l]
        mk, mv = memory_kv(mem_prompt, norm_mem_src[l], w_mem_kv[l], mem_k_norm[l])
        yp = yp + memory_attend(rmsnorm(yp, norm_mem_x[l]), w_mem_q[l], mem_q_norm[l], mk, mv, w_mem_o[l])
        zero_conv = jnp.zeros((yp.shape[0], CONV_W - 1, D_FF), yp.dtype)
        f, conv_p = conv_ffn(rmsnorm(yp, norm_ffn[l]), zero_conv, w_up[l], w_gate[l], ffn_conv_w[l], ffn_conv_b[l], w_down[l])
        yp = yp + f
        pak.append(ka); pav.append(va); pik.append(ki); pbk.append(kb); pbv.append(vb)
        pmk.append(mk); pmv.append(mv); pcv.append(conv_p)

        qa, ka, va, qi, ki, wi, qb, kb, vb = project_groups(rmsnorm(ys, norm_mix[l]), w_in[l], *qk_gains)
        mix = jnp.concatenate([
            dsa_sample(qa, ka, va, qi, ki, wi, cache_a_k[l], cache_a_v[l], cache_idx_k[l], page_table, bias_a),
            diff_sample(qb, kb, vb, cache_b_k[l], cache_b_v[l], page_table, bias_b, lam, lam_init, diff_subln[l])], axis=-1)
        ys = ys + mix @ w_out[l]
        ys = ys + memory_attend(rmsnorm(ys, norm_mem_x[l]), w_mem_q[l], mem_q_norm[l], cache_mem_k[l], cache_mem_v[l], w_mem_o[l])
        f, conv_s = conv_ffn(rmsnorm(ys, norm_ffn[l]), state_ffn_conv[l], w_up[l], w_gate[l], ffn_conv_w[l], ffn_conv_b[l], w_down[l])
        ys = ys + f
        sak.append(ka); sav.append(va); sik.append(ki); sbk.append(kb); sbv.append(vb); scv.append(conv_s)

    return (yp, ys,
            jnp.stack(pak), jnp.stack(pav), jnp.stack(pik), jnp.stack(pbk), jnp.stack(pbv),
            jnp.stack(pmk), jnp.stack(pmv), jnp.stack(pcv),
            jnp.stack(sak), jnp.stack(sav), jnp.stack(sik), jnp.stack(sbk), jnp.stack(sbv), jnp.stack(scv))
```

```python
import functools
import math

import jax
import jax.numpy as jnp
from jax import lax
from jax.experimental import pallas as pl
from jax.experimental.pallas import tpu as pltpu

F32 = jnp.float32
BF16 = jnp.bfloat16
I32 = jnp.int32

EPS = 1e-6
NEG = -1e30
INT_MIN = -(2 ** 31)

A_HEADS = 8
A_KV_HEADS = 2
A_HEAD_DIM = 64
IDX_HEADS = 4
IDX_DIM = 64
TOPK_MAX = 256
B_HEADS = 4
B_HEAD_DIM = 64
MEM_HEADS = 4
MEM_HEAD_DIM = 128
CONV_W = 3
NUM_BUCKETS = 32
MAX_DISTANCE = 128

A_Q = A_HEADS * A_HEAD_DIM
A_KV = A_KV_HEADS * A_HEAD_DIM
IDX_Q = IDX_HEADS * IDX_DIM
B_QK = B_HEADS * 2 * B_HEAD_DIM
B_V = B_HEADS * 2 * B_HEAD_DIM
MEM_W = MEM_HEADS * MEM_HEAD_DIM
A_REP = A_HEADS // A_KV_HEADS

LANES = 128
SUBLANES = 8
TQ = 128
TK = 128
T_PAD = SUBLANES
VMEM_LIMIT = 56 * 1024 * 1024

C_QA = 0
C_KA = C_QA + A_Q
C_VA = C_KA + A_KV
C_QI = C_VA + A_KV
C_KIW = C_QI + IDX_Q
C_QB = C_KIW + LANES
C_KB = C_QB + B_QK
C_VB = C_KB + B_QK
D_IN_PAD = C_VB + B_V


def _dot(a, b):
    return jnp.dot(a, b, preferred_element_type=F32)


def _dot_nt(a, b):
    return lax.dot_general(a, b, (((1,), (1,)), ((), ())), preferred_element_type=F32)


def _rms(x):
    return x * lax.rsqrt(jnp.mean(x * x, axis=-1, keepdims=True) + EPS)


def _float_key(x):
    bits = lax.bitcast_convert_type(x, I32)
    return bits ^ ((bits >> 31) & 0x7FFFFFFF)


def _cparams(sem, vmem=VMEM_LIMIT):
    return pltpu.CompilerParams(dimension_semantics=sem, vmem_limit_bytes=vmem)


def _const_spec(shape, single_buffer=False):
    nd = len(shape)
    if single_buffer:
        return pl.BlockSpec(shape, lambda *_: (0,) * nd, pipeline_mode=pl.Buffered(1))
    return pl.BlockSpec(shape, lambda *_: (0,) * nd)


def _norm_proj_kernel(x_ref, g_ref, w_ref, gm_ref, hg_ref, *out_refs, segs):
    xn = _rms(x_ref[...]) * g_ref[...]
    p = _dot(xn.astype(BF16), w_ref[...])
    for (start, width, norm, out_width), o_ref in zip(segs, out_refs):
        s = p[:, start:start + width]
        if norm:
            ms = _dot((s * s).astype(BF16), gm_ref[:width, :width])
            s = s * lax.rsqrt(ms + EPS) * hg_ref[:, start:start + width]
        o_ref[...] = s[:, :out_width]


def _norm_proj(x2d, gain, w, gmat, hgain, segs, tm, name):
    m, d = x2d.shape
    n = w.shape[1]
    return pl.pallas_call(
        functools.partial(_norm_proj_kernel, segs=segs),
        out_shape=[jax.ShapeDtypeStruct((m, ow), F32) for (_, _, _, ow) in segs],
        grid=(m // tm,),
        in_specs=[pl.BlockSpec((tm, d), lambda i: (i, 0)), _const_spec((1, d)), _const_spec((d, n)),
                  _const_spec(gmat.shape), _const_spec((1, n))],
        out_specs=[pl.BlockSpec((tm, ow), lambda i: (i, 0)) for (_, _, _, ow) in segs],
        compiler_params=_cparams(("parallel",)),
        name=name,
    )(x2d, gain, w, gmat, hgain)


def _online_step(s, vmat, m, l, acc):
    m_new = jnp.maximum(m, jnp.max(s, axis=-1, keepdims=True))
    alpha = jnp.exp(m - m_new)
    p = jnp.exp(s - m_new)
    l = alpha * l + jnp.sum(p, axis=-1, keepdims=True)
    acc = alpha * acc + _dot(p.astype(BF16), vmat)
    return m_new, l, acc


def _select_chunk(sk, thr, need, off, tri):
    eq = sk == thr
    eqf = jnp.where(eq, 1.0, 0.0)
    before = _dot(eqf.astype(BF16), tri) + off
    tie_ok = jnp.where(eq, jnp.where(before < need, 1.0, 0.0), 0.0)
    real = jnp.where(sk != INT_MIN, 1.0, 0.0)
    sel = jnp.where(sk > thr, real, tie_ok * real)
    maskb = jnp.where(sel > 0.5, 0.0, NEG)
    return maskb, off + jnp.sum(eqf, axis=-1, keepdims=True)


def _dsa_prompt_kernel(qi_ref, kiwq_ref, kiw_ref, qa_ref, ka_ref, va_ref, dtab_ref, tri_ref, o_ref,
                       skey_ref, selb_ref, *, topk):
    i = pl.program_id(1)
    nch = i + 1
    qi = qi_ref[0].astype(BF16)
    wi = kiwq_ref[0][:, IDX_DIM:IDX_DIM + IDX_HEADS] * (IDX_HEADS ** -0.5)
    row = lax.broadcasted_iota(I32, (TQ, TK), 0)
    col = lax.broadcasted_iota(I32, (TQ, TK), 1)

    def score_body(c, carry):
        off = pl.multiple_of(c * TK, TK)
        kc = kiw_ref[0, pl.ds(off, TK), :][:, :IDX_DIM].astype(BF16)
        sc = jnp.zeros((TQ, TK), F32)
        for h in range(IDX_HEADS):
            d = _dot_nt(qi[:, h * IDX_DIM:(h + 1) * IDX_DIM], kc) * (IDX_DIM ** -0.5)
            sc = sc + jnp.maximum(d, 0.0) * wi[:, h:h + 1]
        sc = jnp.where(sc == 0.0, 0.0, sc)
        causal = (c * TK + col) <= (i * TQ + row)
        skey_ref[c] = jnp.where(causal, _float_key(sc), INT_MIN)
        return carry

    lax.fori_loop(0, nch, score_body, 0)

    def bit_body(it, thr):
        cand = thr ^ lax.shift_left(jnp.int32(1), 31 - it)

        def cnt_body(c, cnt):
            return cnt + jnp.where(skey_ref[c] >= cand, 1.0, 0.0)

        cnt = lax.fori_loop(0, nch, cnt_body, jnp.zeros((TQ, TK), F32))
        cnt = jnp.sum(cnt, axis=-1, keepdims=True)
        return jnp.where(cnt >= topk, cand, thr)

    thr = lax.fori_loop(0, 32, bit_body, jnp.full((TQ, 1), INT_MIN, I32))

    def gt_body(c, cnt):
        return cnt + jnp.where(skey_ref[c] > thr, 1.0, 0.0)

    cnt_gt = jnp.sum(lax.fori_loop(0, nch, gt_body, jnp.zeros((TQ, TK), F32)), axis=-1, keepdims=True)
    need = float(topk) - cnt_gt
    tri = tri_ref[...]

    def sel_body(c, off):
        maskb, off = _select_chunk(skey_ref[c], thr, need, off, tri)
        selb_ref[c] = maskb
        return off

    lax.fori_loop(0, nch, sel_body, jnp.zeros((TQ, 1), F32))

    scale = A_HEAD_DIM ** -0.5
    for h in range(A_HEADS):
        g0 = (h // A_REP) * A_HEAD_DIM
        qh = qa_ref[0, :, h * A_HEAD_DIM:(h + 1) * A_HEAD_DIM].astype(BF16)

        def att_body(c, carry, g0=g0, qh=qh, h=h):
            off = pl.multiple_of(c * TK, TK)
            kc = ka_ref[0, pl.ds(off, TK), g0:g0 + A_HEAD_DIM].astype(BF16)
            vc = va_ref[0, pl.ds(off, TK), g0:g0 + A_HEAD_DIM].astype(BF16)
            s = _dot_nt(qh, kc) * scale + dtab_ref[h, jnp.minimum(i - c, 2)] + selb_ref[c]
            return _online_step(s, vc, *carry)

        init = (jnp.full((TQ, 1), -jnp.inf, F32), jnp.zeros((TQ, 1), F32), jnp.zeros((TQ, A_HEAD_DIM), F32))
        _, l, acc = lax.fori_loop(0, nch, att_body, init)
        o_ref[0, :, h * A_HEAD_DIM:(h + 1) * A_HEAD_DIM] = acc / l


def _dsa_prompt(qi, kiw, qa, ka, va, dtab, tri):
    b, s, _ = qa.shape
    nq = s // TQ
    topk = min(TOPK_MAX, s // 4)
    blk = lambda w: pl.BlockSpec((1, TQ, w), lambda bi, i: (bi, i, 0))
    full = lambda w: pl.BlockSpec((1, s, w), lambda bi, i: (bi, 0, 0))
    return pl.pallas_call(
        functools.partial(_dsa_prompt_kernel, topk=topk),
        out_shape=jax.ShapeDtypeStruct((b, s, A_Q), F32),
        grid=(b, nq),
        in_specs=[blk(IDX_Q), blk(LANES), full(LANES), blk(A_Q), full(A_KV), full(A_KV),
                  _const_spec(dtab.shape), _const_spec(tri.shape)],
        out_specs=blk(A_Q),
        scratch_shapes=[pltpu.VMEM((nq, TQ, TK), I32), pltpu.VMEM((nq, TQ, TK), F32)],
        compiler_params=_cparams(("parallel", "arbitrary")),
        name="dsa_prompt",
    )(qi, kiw, kiw, qa, ka, va, dtab, tri)


def _diff_lambda(lp_ref, lam_init):
    lp = lp_ref[...]
    s1 = jnp.sum(lp[0:1] * lp[1:2], axis=-1, keepdims=True)
    s2 = jnp.sum(lp[2:3] * lp[3:4], axis=-1, keepdims=True)
    return jnp.exp(s1) - jnp.exp(s2) + lam_init


def _diff_finish(o0, o1, lam, sg, lam_init):
    o = o0 - lam * o1
    return _rms(o) * sg * (1.0 - lam_init)


def _diff_prompt_kernel(q_ref, k_ref, v_ref, dtab_ref, lp_ref, sg_ref, o_ref, *, lam_init):
    i = pl.program_id(1)
    nch = i + 1
    lam = _diff_lambda(lp_ref, lam_init)
    scale = B_HEAD_DIM ** -0.5
    hd = 2 * B_HEAD_DIM
    for h in range(B_HEADS):
        outs = []
        for comp in range(2):
            c0 = (h * 2 + comp) * B_HEAD_DIM
            qh = q_ref[0, :, c0:c0 + B_HEAD_DIM].astype(BF16)

            def body(c, carry, c0=c0, qh=qh, h=h):
                off = pl.multiple_of(c * TK, TK)
                kc = k_ref[0, pl.ds(off, TK), c0:c0 + B_HEAD_DIM].astype(BF16)
                vc = v_ref[0, pl.ds(off, TK), h * hd:(h + 1) * hd].astype(BF16)
                s = _dot_nt(qh, kc) * scale + dtab_ref[h, jnp.minimum(i - c, 2)]
                return _online_step(s, vc, *carry)

            init = (jnp.full((TQ, 1), -jnp.inf, F32), jnp.zeros((TQ, 1), F32), jnp.zeros((TQ, hd), F32))
            _, l, acc = lax.fori_loop(0, nch, body, init)
            outs.append(acc / l)
        o_ref[0, :, h * hd:(h + 1) * hd] = _diff_finish(outs[0], outs[1], lam, sg_ref[...], lam_init)


def _diff_prompt(qb, kb, vb, dtab, lam_p, subln, lam_init):
    b, s, _ = qb.shape
    nq = s // TQ
    blk = lambda w: pl.BlockSpec((1, TQ, w), lambda bi, i: (bi, i, 0))
    full = lambda w: pl.BlockSpec((1, s, w), lambda bi, i: (bi, 0, 0))
    return pl.pallas_call(
        functools.partial(_diff_prompt_kernel, lam_init=lam_init),
        out_shape=jax.ShapeDtypeStruct((b, s, B_V), F32),
        grid=(b, nq),
        in_specs=[blk(B_QK), full(B_QK), full(B_V), _const_spec(dtab.shape), _const_spec(lam_p.shape),
                  _const_spec(subln.shape)],
        out_specs=blk(B_V),
        compiler_params=_cparams(("parallel", "arbitrary")),
        name="diff_prompt",
    )(qb, kb, vb, dtab, lam_p, subln)


def _dsa_sample_select_kernel(pt_ref, qi_ref, kiwq_ref, kinew_ref, *rest, pps, npages, topk):
    del pt_ref
    page_refs = rest[:pps]
    tri_ref, selb_ref, skey_ref = rest[pps:]
    s = pl.program_id(1)
    qi = qi_ref[0].astype(BF16)
    wi = kiwq_ref[0][:, IDX_DIM:IDX_DIM + IDX_HEADS] * (IDX_HEADS ** -0.5)

    def scores(kmat):
        sc = jnp.zeros((T_PAD, TK), F32)
        for h in range(IDX_HEADS):
            d = _dot_nt(qi[:, h * IDX_DIM:(h + 1) * IDX_DIM], kmat) * (IDX_DIM ** -0.5)
            sc = sc + jnp.maximum(d, 0.0) * wi[:, h:h + 1]
        return jnp.where(sc == 0.0, 0.0, sc)

    for j in range(pps):
        skey_ref[s * pps + j] = _float_key(scores(page_refs[j][0].astype(BF16)))

    @pl.when(s == pl.num_programs(1) - 1)
    def _():
        row = lax.broadcasted_iota(I32, (T_PAD, TK), 0)
        col = lax.broadcasted_iota(I32, (T_PAD, TK), 1)
        sc_new = scores(kinew_ref[0].astype(BF16))
        skey_ref[npages] = jnp.where(col <= row, _float_key(sc_new), INT_MIN)

        def count(pred_fn):
            def body(c, cnt):
                return cnt + jnp.where(pred_fn(skey_ref[c]), 1.0, 0.0)
            cnt = lax.fori_loop(0, npages + 1, body, jnp.zeros((T_PAD, TK), F32))
            return jnp.sum(cnt, axis=-1, keepdims=True)

        def bit_body(it, thr):
            cand = thr ^ lax.shift_left(jnp.int32(1), 31 - it)
            cnt = count(lambda sk: sk >= cand)
            return jnp.where(cnt >= topk, cand, thr)

        thr = lax.fori_loop(0, 32, bit_body, jnp.full((T_PAD, 1), INT_MIN, I32))
        need = float(topk) - count(lambda sk: sk > thr)
        tri = tri_ref[...]

        def sel_body(c, off):
            maskb, off = _select_chunk(skey_ref[c], thr, need, off, tri)
            selb_ref[0, c] = maskb
            return off

        lax.fori_loop(0, npages + 1, sel_body, jnp.zeros((T_PAD, 1), F32))


def _page_specs(pps, width):
    return [pl.BlockSpec((1, TK, width), functools.partial(lambda b, s, pt, j: (pt[b, s * pps + j], 0, 0), j=j))
            for j in range(pps)]


def _dsa_sample_select(page_table, qi, kiw, ki_new, cache_ik, tri, n_new, pps):
    db, npages = page_table.shape
    topk = min(TOPK_MAX, (npages * TK + n_new) // 4)
    per_b = lambda shape: pl.BlockSpec((1,) + shape, lambda b, s, pt: (b,) + (0,) * len(shape))
    grid_spec = pltpu.PrefetchScalarGridSpec(
        num_scalar_prefetch=1,
        grid=(db, npages // pps),
        in_specs=[per_b((T_PAD, IDX_Q)), per_b((T_PAD, LANES)), per_b((TK, IDX_DIM))]
        + _page_specs(pps, IDX_DIM) + [pl.BlockSpec(tri.shape, lambda b, s, pt: (0, 0))],
        out_specs=per_b((npages + 1, T_PAD, TK)),
        scratch_shapes=[pltpu.VMEM((npages + 1, T_PAD, TK), I32)],
    )
    return pl.pallas_call(
        functools.partial(_dsa_sample_select_kernel, pps=pps, npages=npages, topk=topk),
        out_shape=jax.ShapeDtypeStruct((db, npages + 1, T_PAD, TK), F32),
        grid_spec=grid_spec,
        compiler_params=_cparams(("parallel", "arbitrary")),
        name="dsa_sample_select",
    )(page_table, qi, kiw, ki_new, *([cache_ik] * pps), tri)


def _paged_flash_step(q, kmat, vmat, bias, scale, m_sc, l_sc, acc_sc):
    s = _dot_nt(q, kmat) * scale + bias
    m, l, acc = _online_step(s, vmat, m_sc[...], l_sc[...], acc_sc[...])
    m_sc[...] = m
    l_sc[...] = l
    acc_sc[...] = acc


def _init_flash(m_sc, l_sc, acc_sc):
    m_sc[...] = jnp.full(m_sc.shape, -jnp.inf, F32)
    l_sc[...] = jnp.zeros(l_sc.shape, F32)
    acc_sc[...] = jnp.zeros(acc_sc.shape, F32)


def _dsa_sample_attn_kernel(pt_ref, q_ref, knew_ref, vnew_ref, selb_ref, btab_ref, *rest, pps, npages):
    del pt_ref
    k_refs = rest[:pps]
    v_refs = rest[pps:2 * pps]
    o_ref, m_sc, l_sc, acc_sc = rest[2 * pps:]
    s = pl.program_id(1)

    @pl.when(s == 0)
    def _():
        _init_flash(m_sc, l_sc, acc_sc)

    q = q_ref[0].astype(BF16)
    scale = A_HEAD_DIM ** -0.5

    def mask_rows(c):
        return jnp.concatenate([selb_ref[0, c]] * A_HEADS, axis=0)

    for j in range(pps):
        c = s * pps + j
        bias = btab_ref[jnp.where(c == npages - 1, 1, 0)] + mask_rows(c)
        _paged_flash_step(q, k_refs[j][0].astype(BF16), v_refs[j][0].astype(BF16), bias, scale, m_sc, l_sc, acc_sc)

    @pl.when(s == pl.num_programs(1) - 1)
    def _():
        bias = btab_ref[2] + mask_rows(npages)
        _paged_flash_step(q, knew_ref[0].astype(BF16), vnew_ref[0].astype(BF16), bias, scale, m_sc, l_sc, acc_sc)
        o = acc_sc[...] / l_sc[...]
        for h in range(A_HEADS):
            g0 = (h // A_REP) * A_HEAD_DIM
            o_ref[0, :, h * A_HEAD_DIM:(h + 1) * A_HEAD_DIM] = o[h * T_PAD:(h + 1) * T_PAD, g0:g0 + A_HEAD_DIM]


def _dsa_sample_attn(page_table, q_bd, k_new, v_new, selb, btab, cache_k, cache_v, pps):
    db, npages = page_table.shape
    rows = A_HEADS * T_PAD
    per_b = lambda shape: pl.BlockSpec((1,) + shape, lambda b, s, pt: (b,) + (0,) * len(shape))
    grid_spec = pltpu.PrefetchScalarGridSpec(
        num_scalar_prefetch=1,
        grid=(db, npages // pps),
        in_specs=[per_b((rows, A_KV)), per_b((TK, A_KV)), per_b((TK, A_KV)), per_b((npages + 1, T_PAD, TK)),
                  pl.BlockSpec(btab.shape, lambda b, s, pt: (0, 0, 0))]
        + _page_specs(pps, A_KV) + _page_specs(pps, A_KV),
        out_specs=per_b((T_PAD, A_Q)),
        scratch_shapes=[pltpu.VMEM((rows, 1), F32), pltpu.VMEM((rows, 1), F32), pltpu.VMEM((rows, A_KV), F32)],
    )
    return pl.pallas_call(
        functools.partial(_dsa_sample_attn_kernel, pps=pps, npages=npages),
        out_shape=jax.ShapeDtypeStruct((db, T_PAD, A_Q), F32),
        grid_spec=grid_spec,
        compiler_params=_cparams(("parallel", "arbitrary")),
        name="dsa_sample_attn",
    )(page_table, q_bd, k_new, v_new, selb, btab, *([cache_k] * pps), *([cache_v] * pps))


def _diff_sample_kernel(pt_ref, q_ref, knew_ref, vnew_ref, btab_ref, lp_ref, sg_ref, *rest, pps, npages, lam_init):
    del pt_ref
    k_refs = rest[:pps]
    v_refs = rest[pps:2 * pps]
    o_ref, m_sc, l_sc, acc_sc = rest[2 * pps:]
    s = pl.program_id(1)

    @pl.when(s == 0)
    def _():
        _init_flash(m_sc, l_sc, acc_sc)

    q = q_ref[0].astype(BF16)
    scale = B_HEAD_DIM ** -0.5
    for j in range(pps):
        c = s * pps + j
        bias = btab_ref[jnp.where(c == npages - 1, 1, 0)]
        _paged_flash_step(q, k_refs[j][0].astype(BF16), v_refs[j][0].astype(BF16), bias, scale, m_sc, l_sc, acc_sc)

    @pl.when(s == pl.num_programs(1) - 1)
    def _():
        _paged_flash_step(q, knew_ref[0].astype(BF16), vnew_ref[0].astype(BF16), btab_ref[2], scale,
                          m_sc, l_sc, acc_sc)
        o = acc_sc[...] / l_sc[...]
        lam = _diff_lambda(lp_ref, lam_init)
        hd = 2 * B_HEAD_DIM
        for h in range(B_HEADS):
            r0 = 2 * h * T_PAD
            o0 = o[r0:r0 + T_PAD, h * hd:(h + 1) * hd]
            o1 = o[r0 + T_PAD:r0 + 2 * T_PAD, h * hd:(h + 1) * hd]
            o_ref[0, :, h * hd:(h + 1) * hd] = _diff_finish(o0, o1, lam, sg_ref[...], lam_init)


def _diff_sample(page_table, q_bd, k_new, v_new, btab, lam_p, subln, cache_k, cache_v, lam_init, pps):
    db, npages = page_table.shape
    rows = 2 * B_HEADS * T_PAD
    per_b = lambda shape: pl.BlockSpec((1,) + shape, lambda b, s, pt: (b,) + (0,) * len(shape))
    const = lambda shape: pl.BlockSpec(shape, lambda b, s, pt: (0,) * len(shape))
    grid_spec = pltpu.PrefetchScalarGridSpec(
        num_scalar_prefetch=1,
        grid=(db, npages // pps),
        in_specs=[per_b((rows, B_QK)), per_b((TK, B_QK)), per_b((TK, B_V)), const(btab.shape),
                  const(lam_p.shape), const(subln.shape)]
        + _page_specs(pps, B_QK) + _page_specs(pps, B_V),
        out_specs=per_b((T_PAD, B_V)),
        scratch_shapes=[pltpu.VMEM((rows, 1), F32), pltpu.VMEM((rows, 1), F32), pltpu.VMEM((rows, B_V), F32)],
    )
    return pl.pallas_call(
        functools.partial(_diff_sample_kernel, pps=pps, npages=npages, lam_init=lam_init),
        out_shape=jax.ShapeDtypeStruct((db, T_PAD, B_V), F32),
        grid_spec=grid_spec,
        compiler_params=_cparams(("parallel", "arbitrary")),
        name="diff_sample",
    )(page_table, q_bd, k_new, v_new, btab, lam_p, subln, *([cache_k] * pps), *([cache_v] * pps))


def _mid_kernel(x_ref, ma_ref, mb_ref, mk_ref, mv_ref, wout_ref, gx_ref, wq_ref, gm_ref, qg_ref, wo_ref, h2_ref):
    h = (x_ref[0] + _dot(ma_ref[0].astype(BF16), wout_ref[:A_Q, :])
         + _dot(mb_ref[0].astype(BF16), wout_ref[A_Q:, :]))
    hn = _rms(h) * gx_ref[...]
    q = _dot(hn.astype(BF16), wq_ref[...])
    ms = _dot((q * q).astype(BF16), gm_ref[...])
    q = (q * lax.rsqrt(ms + EPS) * qg_ref[...]).astype(BF16)
    mk = mk_ref[0].astype(BF16)
    mv = mv_ref[0].astype(BF16)
    outs = []
    for hh in range(MEM_HEADS):
        sl = slice(hh * MEM_HEAD_DIM, (hh + 1) * MEM_HEAD_DIM)
        s = _dot_nt(q[:, sl], mk[:, sl]) * (MEM_HEAD_DIM ** -0.5)
        p = jnp.exp(s - jnp.max(s, axis=-1, keepdims=True))
        l = jnp.sum(p, axis=-1, keepdims=True)
        outs.append(_dot(p.astype(BF16), mv[:, sl]) / l)
    o = jnp.concatenate(outs, axis=-1)
    h2_ref[0] = h + _dot(o.astype(BF16), wo_ref[...])


def _mid(x, mix_a, mix_b, mk, mv, w_out, g_x, w_q, gmat, q_gain, w_o, tm):
    b, s, d = x.shape
    m_tok = mk.shape[1]
    blk = lambda w: pl.BlockSpec((1, tm, w), lambda bi, i: (bi, i, 0))
    per_b = lambda w: pl.BlockSpec((1, m_tok, w), lambda bi, i: (bi, 0, 0))
    return pl.pallas_call(
        _mid_kernel,
        out_shape=jax.ShapeDtypeStruct((b, s, d), F32),
        grid=(b, s // tm),
        in_specs=[blk(d), blk(A_Q), blk(B_V), per_b(MEM_W), per_b(MEM_W), _const_spec(w_out.shape),
                  _const_spec(g_x.shape), _const_spec(w_q.shape), _const_spec(gmat.shape),
                  _const_spec(q_gain.shape), _const_spec(w_o.shape)],
        out_specs=blk(d),
        compiler_params=_cparams(("parallel", "arbitrary")),
        name="mid",
    )(x, mix_a, mix_b, mk, mv, w_out, g_x, w_q, gmat, q_gain, w_o)


def _ffn_core(h, gn, wg_ref, wu_ref, cw_ref, cb_ref, wd_ref, shifted):
    xb = (_rms(h) * gn).astype(BF16)
    g = _dot(xb, wg_ref[...])
    u = _dot(xb, wu_ref[...])
    gm1, gm2 = shifted(g)
    gc = cb_ref[...] + cw_ref[0:1, :] * gm2 + cw_ref[1:2, :] * gm1 + cw_ref[2:3, :] * g
    a = gc / (1.0 + jnp.exp(-gc)) * u
    return h + _dot(a.astype(BF16), wd_ref[...]), g


def _ffn_prompt_kernel(h_ref, gn_ref, wg_ref, wu_ref, cw_ref, cb_ref, wd_ref, y_ref, tail_ref, carry_ref):
    @pl.when(pl.program_id(1) == 0)
    def _():
        carry_ref[...] = jnp.zeros(carry_ref.shape, F32)

    tm = h_ref.shape[1]
    row = lax.broadcasted_iota(I32, (tm, 1), 0)
    c0 = carry_ref[SUBLANES - 2:SUBLANES - 1, :]
    c1 = carry_ref[SUBLANES - 1:SUBLANES, :]

    def shifted(g):
        gm1 = jnp.where(row == 0, c1, pltpu.roll(g, 1, 0))
        gm2 = jnp.where(row == 0, c0, jnp.where(row == 1, c1, pltpu.roll(g, 2, 0)))
        return gm1, gm2

    y, g = _ffn_core(h_ref[0], gn_ref[...], wg_ref, wu_ref, cw_ref, cb_ref, wd_ref, shifted)
    y_ref[0] = y
    tail = g[tm - SUBLANES:, :]
    carry_ref[...] = tail
    tail_ref[0] = tail


def _ffn_prompt(h, gn, wg, wu, cw, cb, wd, tm):
    b, s, d = h.shape
    f = wg.shape[1]
    blk = pl.BlockSpec((1, tm, d), lambda bi, i: (bi, i, 0))
    wspec = lambda shape: _const_spec(shape, single_buffer=True)
    return pl.pallas_call(
        _ffn_prompt_kernel,
        out_shape=[jax.ShapeDtypeStruct((b, s, d), F32), jax.ShapeDtypeStruct((b, SUBLANES, f), F32)],
        grid=(b, s // tm),
        in_specs=[blk, _const_spec(gn.shape), wspec(wg.shape), wspec(wu.shape), _const_spec(cw.shape),
                  _const_spec(cb.shape), wspec(wd.shape)],
        out_specs=[blk, pl.BlockSpec((1, SUBLANES, f), lambda bi, i: (bi, 0, 0))],
        scratch_shapes=[pltpu.VMEM((SUBLANES, f), F32)],
        compiler_params=_cparams(("arbitrary", "arbitrary")),
        name="ffn_prompt",
    )(h, gn, wg, wu, cw, cb, wd)


def _ffn_sample_kernel(h_ref, gn_ref, wg_ref, wu_ref, cw_ref, cb_ref, wd_ref, st1_ref, st2_ref, y_ref, g_ref):
    m = h_ref.shape[0]
    t = lax.broadcasted_iota(I32, (m, 1), 0) & (T_PAD - 1)

    def shifted(g):
        gm1 = jnp.where(t == 0, st1_ref[...], pltpu.roll(g, 1, 0))
        gm2 = jnp.where(t < 2, st2_ref[...], pltpu.roll(g, 2, 0))
        return gm1, gm2

    y, g = _ffn_core(h_ref[...], gn_ref[...], wg_ref, wu_ref, cw_ref, cb_ref, wd_ref, shifted)
    y_ref[...] = y
    g_ref[...] = g


def _ffn_sample(h2d, gn, wg, wu, cw, cb, wd, st1, st2):
    m, d = h2d.shape
    f = wg.shape[1]
    wspec = lambda shape: _const_spec(shape, single_buffer=True)
    return pl.pallas_call(
        _ffn_sample_kernel,
        out_shape=[jax.ShapeDtypeStruct((m, d), F32), jax.ShapeDtypeStruct((m, f), F32)],
        grid=(1,),
        in_specs=[_const_spec((m, d)), _const_spec(gn.shape), wspec(wg.shape), wspec(wu.shape),
                  _const_spec(cw.shape), _const_spec(cb.shape), wspec(wd.shape), _const_spec((m, f)),
                  _const_spec((m, f))],
        out_specs=[_const_spec((m, d)), _const_spec((m, f))],
        compiler_params=_cparams(("arbitrary",)),
        name="ffn_sample",
    )(h2d, gn, wg, wu, cw, cb, wd, st1, st2)


def _rel_bucket(dist):
    n = jnp.maximum(dist, 0)
    max_exact = NUM_BUCKETS // 2
    nf = jnp.maximum(n, 1).astype(F32)
    log_b = jnp.log(nf / max_exact) / math.log(MAX_DISTANCE / max_exact) * (NUM_BUCKETS - max_exact)
    large = jnp.minimum(max_exact + log_b.astype(I32), NUM_BUCKETS - 1)
    return jnp.where(n < max_exact, n, large)


def _bias_by_dist(dist, causal, bias):
    vals = jnp.moveaxis(bias[_rel_bucket(dist)], -1, 0)
    return jnp.where(causal[None], vals, NEG).astype(F32)


def _prompt_bias_tables(bias):
    r = jnp.arange(TQ)[:, None]
    c = jnp.arange(TK)[None, :]
    always = jnp.ones((TQ, TK), bool)
    t0 = _bias_by_dist(r - c, r >= c, bias)
    t1 = _bias_by_dist(r - c + TK, always, bias)
    t2 = _bias_by_dist(r - c + 2 * TK, always, bias)
    return jnp.stack([t0, t1, t2], axis=1)


def _sample_bias_tables(bias, streams_per_head):
    t = jnp.arange(T_PAD)[:, None]
    c = jnp.arange(TK)[None, :]
    always = jnp.ones((T_PAD, TK), bool)
    far = _bias_by_dist(t - c + 2 * TK, always, bias)
    last = _bias_by_dist(t - c + TK, always, bias)
    new = _bias_by_dist(t - c, c <= t, bias)
    tabs = jnp.stack([far, last, new], axis=0)
    tabs = jnp.repeat(tabs[:, :, None], streams_per_head, axis=2)
    return tabs.reshape(3, -1, TK)


def _group_mean_matrix(width, group):
    idx = jnp.arange(width) // group
    return jnp.where(idx[:, None] == idx[None, :], 1.0 / group, 0.0).astype(BF16)


def _pad_rows(x, rows):
    return jnp.pad(x, ((0, 0), (0, rows - x.shape[1]), (0, 0)))


def kernel(x_prompt, x_sample, mem_prompt, cache_a_k, cache_a_v, cache_idx_k, cache_b_k, cache_b_v, cache_mem_k, cache_mem_v, state_ffn_conv, page_table, rel_bias, norm_mix, w_in, a_q_norm, a_k_norm, b_q_norm, b_k_norm, diff_lambda, diff_subln, w_out, norm_mem_x, norm_mem_src, w_mem_q, w_mem_kv, mem_q_norm, mem_k_norm, w_mem_o, norm_ffn, w_up, w_gate, ffn_conv_w, ffn_conv_b, w_down):
    depth = w_in.shape[0]
    assert depth == 1, "single-layer trunk"
    layer = 0
    lam_init = 0.8 - 0.6 * math.exp(-0.3 * layer)
    b, s, d = x_prompt.shape
    db, t_new, _ = x_sample.shape
    assert t_new <= T_PAD and cache_a_k.shape[2] == TK
    m_tok = mem_prompt.shape[1]
    f = w_up.shape[-1]
    n_pool = cache_a_k.shape[1]

    w_in_l = w_in[layer]
    n_front = A_Q + 2 * A_KV + IDX_Q + IDX_DIM + IDX_HEADS
    w_in_p = jnp.concatenate(
        [w_in_l[:, :n_front], jnp.zeros((d, C_QB - n_front), F32), w_in_l[:, n_front:]], axis=1).astype(BF16)
    ones = lambda n: jnp.ones((n,), F32)
    hgain = jnp.concatenate([
        jnp.tile(a_q_norm[layer], A_HEADS), jnp.tile(a_k_norm[layer], A_KV_HEADS), ones(C_QB - C_VA),
        jnp.tile(b_q_norm[layer], 2 * B_HEADS), jnp.tile(b_k_norm[layer], 2 * B_HEADS), ones(B_V)])[None, :]
    gmat64 = _group_mean_matrix(A_Q, A_HEAD_DIM)
    gmat128 = _group_mean_matrix(MEM_W, MEM_HEAD_DIM)
    proj_segs = ((C_QA, A_Q, True, A_Q), (C_KA, A_KV, True, A_KV), (C_VA, A_KV, False, A_KV),
                 (C_QI, IDX_Q, False, IDX_Q), (C_KIW, LANES, False, LANES), (C_KIW, LANES, False, IDX_DIM),
                 (C_QB, B_QK, True, B_QK), (C_KB, B_QK, True, B_QK), (C_VB, B_V, False, B_V))
    g_mix = norm_mix[layer][None, :]
    w_out_b = w_out[layer].astype(BF16)
    w_q_b = w_mem_q[layer].astype(BF16)
    w_o_b = w_mem_o[layer].astype(BF16)
    w_kv_b = w_mem_kv[layer].astype(BF16)
    w_gate_b = w_gate[layer].astype(BF16)
    w_up_b = w_up[layer].astype(BF16)
    w_down_b = w_down[layer].astype(BF16)
    g_memx = norm_mem_x[layer][None, :]
    q_gain = jnp.tile(mem_q_norm[layer], MEM_HEADS)[None, :]
    kv_gain = jnp.concatenate([jnp.tile(mem_k_norm[layer], MEM_HEADS), ones(MEM_W)])[None, :]
    g_ffn = norm_ffn[layer][None, :]
    conv_w = ffn_conv_w[layer]
    conv_b = ffn_conv_b[layer][None, :]
    lam_p = diff_lambda[layer]
    subln = diff_subln[layer][None, :]
    bias_a = rel_bias[:, :A_HEADS]
    bias_b = rel_bias[:, A_HEADS:]
    tri = (jnp.arange(TK)[:, None] < jnp.arange(TK)[None, :]).astype(BF16)

    qa, ka, va, qi, kiw, ki, qb, kb, vb = _norm_proj(
        x_prompt.reshape(b * s, d), g_mix, w_in_p, gmat64, hgain, proj_segs, 256, "proj_prompt")
    r3 = lambda a: a.reshape(b, s, a.shape[-1])
    mix_a = _dsa_prompt(r3(qi), r3(kiw), r3(qa), r3(ka), r3(va), _prompt_bias_tables(bias_a), tri)
    mix_b = _diff_prompt(r3(qb), r3(kb), r3(vb), _prompt_bias_tables(bias_b), lam_p, subln, lam_init)
    mk, mv = _norm_proj(mem_prompt.reshape(b * m_tok, d), norm_mem_src[layer][None, :], w_kv_b, gmat128, kv_gain,
                        ((0, MEM_W, True, MEM_W), (MEM_W, MEM_W, False, MEM_W)), 256, "mem_kv")
    h2 = _mid(x_prompt, mix_a, mix_b, mk.reshape(b, m_tok, MEM_W), mv.reshape(b, m_tok, MEM_W),
              w_out_b, g_memx, w_q_b, gmat128, q_gain, w_o_b, 256)
    yp, tail = _ffn_prompt(h2, g_ffn, w_gate_b, w_up_b, conv_w, conv_b, w_down_b, 256)
    conv_p = tail[:, SUBLANES - (CONV_W - 1):, :]

    xs = _pad_rows(x_sample, T_PAD)
    qa, ka_s, va_s, qi, kiw, ki_s, qb, kb_s, vb_s = _norm_proj(
        xs.reshape(db * T_PAD, d), g_mix, w_in_p, gmat64, hgain, proj_segs, db * T_PAD, "proj_sample")
    r3 = lambda a: a.reshape(db, T_PAD, a.shape[-1])
    selb = _dsa_sample_select(page_table, r3(qi), r3(kiw), _pad_rows(r3(ki_s), TK),
                              cache_idx_k[layer], tri, t_new, 8)
    q_rows = r3(qa).reshape(db, T_PAD, A_HEADS, A_HEAD_DIM).transpose(0, 2, 1, 3)
    eye_g = jnp.repeat(jnp.eye(A_KV_HEADS, dtype=F32), A_REP, axis=0)
    qa_bd = (q_rows[:, :, :, None, :] * eye_g[None, :, None, :, None]).reshape(db, A_HEADS * T_PAD, A_KV)
    mix_a = _dsa_sample_attn(page_table, qa_bd, _pad_rows(r3(ka_s), TK), _pad_rows(r3(va_s), TK), selb,
                             _sample_bias_tables(bias_a, 1), cache_a_k[layer].reshape(n_pool, TK, A_KV),
                             cache_a_v[layer].reshape(n_pool, TK, A_KV), 8)
    n_str = 2 * B_HEADS
    q_rows = r3(qb).reshape(db, T_PAD, n_str, B_HEAD_DIM).transpose(0, 2, 1, 3)
    qb_bd = (q_rows[:, :, :, None, :] * jnp.eye(n_str, dtype=F32)[None, :, None, :, None]).reshape(
        db, n_str * T_PAD, B_QK)
    mix_b = _diff_sample(page_table, qb_bd, _pad_rows(r3(kb_s), TK), _pad_rows(r3(vb_s), TK),
                         _sample_bias_tables(bias_b, 2), lam_p, subln,
                         cache_b_k[layer].reshape(n_pool, TK, B_QK), cache_b_v[layer].reshape(n_pool, TK, B_V),
                         lam_init, 4)
    h2 = _mid(xs, mix_a, mix_b, cache_mem_k[layer].reshape(db, m_tok, MEM_W),
              cache_mem_v[layer].reshape(db, m_tok, MEM_W), w_out_b, g_memx, w_q_b, gmat128, q_gain, w_o_b, T_PAD)
    state = state_ffn_conv[layer]
    st1 = _pad_rows(state[:, 1:2, :], T_PAD).reshape(db * T_PAD, f)
    st2 = _pad_rows(state, T_PAD).reshape(db * T_PAD, f)
    ys, g_s = _ffn_sample(h2.reshape(db * T_PAD, d), g_ffn, w_gate_b, w_up_b, conv_w, conv_b, w_down_b, st1, st2)
    ys = ys.reshape(db, T_PAD, d)[:, :t_new]
    conv_s = g_s.reshape(db, T_PAD, f)[:, t_new - (CONV_W - 1):t_new]

    def new_rows(a, shape):
        return a.reshape(db, T_PAD, -1)[:, :t_new].reshape((1, db, t_new) + shape)

    return (yp, ys,
            ka.reshape(1, b, s, A_KV_HEADS, A_HEAD_DIM), va.reshape(1, b, s, A_KV_HEADS, A_HEAD_DIM),
            ki.reshape(1, b, s, IDX_DIM), kb.reshape(1, b, s, B_HEADS, 2, B_HEAD_DIM),
            vb.reshape(1, b, s, B_HEADS, 2 * B_HEAD_DIM),
            mk.reshape(1, b, m_tok, MEM_HEADS, MEM_HEAD_DIM), mv.reshape(1, b, m_tok, MEM_HEADS, MEM_HEAD_DIM),
            conv_p[None],
            new_rows(ka_s, (A_KV_HEADS, A_HEAD_DIM)), new_rows(va_s, (A_KV_HEADS, A_HEAD_DIM)),
            new_rows(ki_s, (IDX_DIM,)), new_rows(kb_s, (B_HEADS, 2, B_HEAD_DIM)),
            new_rows(vb_s, (B_HEADS, 2 * B_HEAD_DIM)), conv_s[None])
```

```python
import functools
import math

import jax
import jax.numpy as jnp
from jax import lax
from jax.experimental import pallas as pl
from jax.experimental.pallas import tpu as pltpu

F32 = jnp.float32
BF16 = jnp.bfloat16
I32 = jnp.int32

EPS = 1e-6
NEG = -1e30
INT_MIN = -(2 ** 31)

A_HEADS = 8
A_KV_HEADS = 2
A_HEAD_DIM = 64
IDX_HEADS = 4
IDX_DIM = 64
TOPK_MAX = 256
B_HEADS = 4
B_HEAD_DIM = 64
MEM_HEADS = 4
MEM_HEAD_DIM = 128
CONV_W = 3
NUM_BUCKETS = 32
MAX_DISTANCE = 128

A_Q = A_HEADS * A_HEAD_DIM
A_KV = A_KV_HEADS * A_HEAD_DIM
IDX_Q = IDX_HEADS * IDX_DIM
B_QK = B_HEADS * 2 * B_HEAD_DIM
B_V = B_HEADS * 2 * B_HEAD_DIM
B_VH = 2 * B_HEAD_DIM
MEM_W = MEM_HEADS * MEM_HEAD_DIM
A_REP = A_HEADS // A_KV_HEADS

LANES = 128
SUBLANES = 8
TQ = 128
TK = 128
TKB_MAX = 512
T_PAD = SUBLANES
VMEM_LIMIT = 56 * 1024 * 1024

C_QA = 0
C_KA = C_QA + A_Q
C_VA = C_KA + A_KV
C_QI = C_VA + A_KV
C_KIW = C_QI + IDX_Q
C_QB = C_KIW + LANES
C_KB = C_QB + B_QK
C_VB = C_KB + B_QK
D_IN_PAD = C_VB + B_V


def _dot(a, b):
    return jnp.dot(a, b, preferred_element_type=F32)


def _dot_nt(a, b):
    return lax.dot_general(a, b, (((1,), (1,)), ((), ())), preferred_element_type=F32)


def _rms(x):
    return x * lax.rsqrt(jnp.mean(x * x, axis=-1, keepdims=True) + EPS)


def _float_key(x):
    bits = lax.bitcast_convert_type(x, I32)
    return bits ^ ((bits >> 31) & 0x7FFFFFFF)


def _cparams(sem, vmem=VMEM_LIMIT):
    return pltpu.CompilerParams(dimension_semantics=sem, vmem_limit_bytes=vmem)


def _const_spec(shape, single_buffer=False):
    nd = len(shape)
    if single_buffer:
        return pl.BlockSpec(shape, lambda *_: (0,) * nd, pipeline_mode=pl.Buffered(1))
    return pl.BlockSpec(shape, lambda *_: (0,) * nd)


def _norm_proj_kernel(x_ref, g_ref, w_ref, gm_ref, hg_ref, *out_refs, segs):
    xn = _rms(x_ref[...]) * g_ref[...]
    p = _dot(xn.astype(BF16), w_ref[...])
    for (start, width, norm, out_width), o_ref in zip(segs, out_refs):
        s = p[:, start:start + width]
        if norm:
            ms = _dot((s * s).astype(BF16), gm_ref[:width, :width])
            s = s * lax.rsqrt(ms + EPS) * hg_ref[:, start:start + width]
        o_ref[...] = s[:, :out_width]


def _norm_proj(x2d, gain, w, gmat, hgain, segs, tm, name):
    m, d = x2d.shape
    n = w.shape[1]
    return pl.pallas_call(
        functools.partial(_norm_proj_kernel, segs=segs),
        out_shape=[jax.ShapeDtypeStruct((m, ow), F32) for (_, _, _, ow) in segs],
        grid=(m // tm,),
        in_specs=[pl.BlockSpec((tm, d), lambda i: (i, 0)), _const_spec((1, d)), _const_spec((d, n)),
                  _const_spec(gmat.shape), _const_spec((1, n))],
        out_specs=[pl.BlockSpec((tm, ow), lambda i: (i, 0)) for (_, _, _, ow) in segs],
        compiler_params=_cparams(("parallel",)),
        name=name,
    )(x2d, gain, w, gmat, hgain)


def _init_flash(m_sc, l_sc, acc_sc):
    m_sc[...] = jnp.full(m_sc.shape, -jnp.inf, F32)
    l_sc[...] = jnp.zeros(l_sc.shape, F32)
    acc_sc[...] = jnp.zeros(acc_sc.shape, F32)


def _flash_update(st, s, pv_fn, m_sc, l_sc, acc_sc):
    m_old = m_sc[st]
    m_new = jnp.maximum(m_old, jnp.max(s, axis=-1, keepdims=True))
    alpha = jnp.exp(m_old - m_new)
    p = jnp.exp(s - m_new)
    l_sc[st] = alpha * l_sc[st] + jnp.sum(p, axis=-1, keepdims=True)
    acc_sc[st] = alpha * acc_sc[st] + pv_fn(p.astype(BF16))
    m_sc[st] = m_new


def _flash_scratch(streams, rows, dv):
    return [pltpu.VMEM((streams, rows, 1), F32), pltpu.VMEM((streams, rows, 1), F32),
            pltpu.VMEM((streams, rows, dv), F32)]


def _select_mask(sk, thr, need, before):
    tie_ok = jnp.where(sk == thr, jnp.where(before < need, 1.0, 0.0), 0.0)
    real = jnp.where(sk != INT_MIN, 1.0, 0.0)
    sel = jnp.where(sk > thr, real, tie_ok * real)
    return jnp.where(sel > 0.5, 0.0, NEG)


def _chunk_bias(dtab_ref, h, i, c, ksub):
    return jnp.concatenate([dtab_ref[h, jnp.clip(i - (c * ksub + j) + 1, 0, 3)] for j in range(ksub)], axis=1)


def _dsa_prompt_kernel(qi_ref, kiwq_ref, kiw_ref, qa_ref, ka_ref, va_ref, dtab_ref, tri_ref, o_ref,
                       skey_ref, selb_ref, m_sc, l_sc, acc_sc, *, topk, tkb):
    i = pl.program_id(1)
    ksub = tkb // TK
    nbig = i // ksub + 1
    qi = qi_ref[0].astype(BF16)
    wi = kiwq_ref[0][:, IDX_DIM:IDX_DIM + IDX_HEADS] * (IDX_HEADS ** -0.5)
    row = lax.broadcasted_iota(I32, (TQ, tkb), 0)
    col = lax.broadcasted_iota(I32, (TQ, tkb), 1)

    def score_body(c, carry):
        off = pl.multiple_of(c * tkb, tkb)
        kc = kiw_ref[0, pl.ds(off, tkb), 0:IDX_DIM].astype(BF16)
        sc = jnp.zeros((TQ, tkb), F32)
        for h in range(IDX_HEADS):
            d = _dot_nt(qi[:, h * IDX_DIM:(h + 1) * IDX_DIM], kc) * (IDX_DIM ** -0.5)
            sc = sc + jnp.maximum(d, 0.0) * wi[:, h:h + 1]
        sc = jnp.where(sc == 0.0, 0.0, sc)
        causal = (c * tkb + col) <= (i * TQ + row)
        skey_ref[c] = jnp.where(causal, _float_key(sc), INT_MIN)
        return carry

    lax.fori_loop(0, nbig, score_body, 0)

    def count(pred_fn):
        def body(c, cnt):
            return cnt + jnp.where(pred_fn(skey_ref[c]), 1.0, 0.0)
        cnt = lax.fori_loop(0, nbig, body, jnp.zeros((TQ, tkb), F32))
        return jnp.sum(cnt, axis=-1, keepdims=True)

    def bit_body(it, thr):
        cand = thr ^ lax.shift_left(jnp.int32(1), 31 - it)
        return jnp.where(count(lambda sk: sk >= cand) >= topk, cand, thr)

    thr = lax.fori_loop(0, 32, bit_body, jnp.full((TQ, 1), INT_MIN, I32))
    need = float(topk) - count(lambda sk: sk > thr)
    tri = tri_ref[...]

    def sel_body(c, off):
        sk = skey_ref[c]
        eqf = jnp.where(sk == thr, 1.0, 0.0)
        before = _dot(eqf.astype(BF16), tri) + off
        selb_ref[c] = _select_mask(sk, thr, need, before)
        return off + jnp.sum(eqf, axis=-1, keepdims=True)

    lax.fori_loop(0, nbig, sel_body, jnp.zeros((TQ, 1), F32))

    _init_flash(m_sc, l_sc, acc_sc)
    scale = A_HEAD_DIM ** -0.5

    def att_body(c, carry):
        off = pl.multiple_of(c * tkb, tkb)
        maskb = selb_ref[c]
        for g in range(A_KV_HEADS):
            g0 = g * A_HEAD_DIM
            kc = ka_ref[0, pl.ds(off, tkb), g0:g0 + A_HEAD_DIM].astype(BF16)
            vc = va_ref[0, pl.ds(off, tkb), g0:g0 + A_HEAD_DIM].astype(BF16)
            for r in range(A_REP):
                h = g * A_REP + r
                qh = (qa_ref[0, :, h * A_HEAD_DIM:(h + 1) * A_HEAD_DIM] * scale).astype(BF16)
                s = _dot_nt(qh, kc) + (_chunk_bias(dtab_ref, h, i, c, ksub) + maskb)
                _flash_update(h, s, lambda p, vc=vc: _dot(p, vc), m_sc, l_sc, acc_sc)
        return carry

    lax.fori_loop(0, nbig, att_body, 0)
    for h in range(A_HEADS):
        o_ref[0, :, h * A_HEAD_DIM:(h + 1) * A_HEAD_DIM] = acc_sc[h] / l_sc[h]


def _dsa_prompt(qi, kiw, qa, ka, va, dtab, tkb):
    b, s, _ = qa.shape
    nq = s // TQ
    topk = min(TOPK_MAX, s // 4)
    tri = (jnp.arange(tkb)[:, None] < jnp.arange(tkb)[None, :]).astype(BF16)
    blk = lambda w: pl.BlockSpec((1, TQ, w), lambda bi, i: (bi, i, 0))
    full = lambda w: pl.BlockSpec((1, s, w), lambda bi, i: (bi, 0, 0))
    return pl.pallas_call(
        functools.partial(_dsa_prompt_kernel, topk=topk, tkb=tkb),
        out_shape=jax.ShapeDtypeStruct((b, s, A_Q), F32),
        grid=(b, nq),
        in_specs=[blk(IDX_Q), blk(LANES), full(LANES), blk(A_Q), full(A_KV), full(A_KV),
                  _const_spec(dtab.shape), _const_spec(tri.shape)],
        out_specs=blk(A_Q),
        scratch_shapes=[pltpu.VMEM((s // tkb, TQ, tkb), I32), pltpu.VMEM((s // tkb, TQ, tkb), F32)]
        + _flash_scratch(A_HEADS, TQ, A_HEAD_DIM),
        compiler_params=_cparams(("parallel", "arbitrary")),
        name="dsa_prompt",
    )(qi, kiw, kiw, qa, ka, va, dtab, tri)


def _diff_lambda(lp_ref, lam_init):
    lp = lp_ref[...]
    s1 = jnp.sum(lp[0:1] * lp[1:2], axis=-1, keepdims=True)
    s2 = jnp.sum(lp[2:3] * lp[3:4], axis=-1, keepdims=True)
    return jnp.exp(s1) - jnp.exp(s2) + lam_init


def _diff_finish(o0, o1, lam, sg, lam_init):
    o = o0 - lam * o1
    return _rms(o) * sg * (1.0 - lam_init)


def _diff_prompt_kernel(q_ref, k_ref, v_ref, dtab_ref, lp_ref, sg_ref, o_ref, m_sc, l_sc, acc_sc, *, lam_init, tkb):
    i = pl.program_id(1)
    ksub = tkb // TK
    nbig = i // ksub + 1
    scale = B_HEAD_DIM ** -0.5
    _init_flash(m_sc, l_sc, acc_sc)

    def body(c, carry):
        off = pl.multiple_of(c * tkb, tkb)
        for h in range(B_HEADS):
            bias = _chunk_bias(dtab_ref, h, i, c, ksub)
            vc = v_ref[0, pl.ds(off, tkb), h * B_VH:(h + 1) * B_VH].astype(BF16)
            for comp in range(2):
                st = h * 2 + comp
                c0 = st * B_HEAD_DIM
                qh = (q_ref[0, :, c0:c0 + B_HEAD_DIM] * scale).astype(BF16)
                kc = k_ref[0, pl.ds(off, tkb), c0:c0 + B_HEAD_DIM].astype(BF16)
                s = _dot_nt(qh, kc) + bias
                _flash_update(st, s, lambda p, vc=vc: _dot(p, vc), m_sc, l_sc, acc_sc)
        return carry

    lax.fori_loop(0, nbig, body, 0)
    lam = _diff_lambda(lp_ref, lam_init)
    for h in range(B_HEADS):
        o0 = acc_sc[2 * h] / l_sc[2 * h]
        o1 = acc_sc[2 * h + 1] / l_sc[2 * h + 1]
        o_ref[0, :, h * B_VH:(h + 1) * B_VH] = _diff_finish(o0, o1, lam, sg_ref[...], lam_init)


def _diff_prompt(qb, kb, vb, dtab, lam_p, subln, lam_init, tkb):
    b, s, _ = qb.shape
    nq = s // TQ
    blk = lambda w: pl.BlockSpec((1, TQ, w), lambda bi, i: (bi, i, 0))
    full = lambda w: pl.BlockSpec((1, s, w), lambda bi, i: (bi, 0, 0))
    return pl.pallas_call(
        functools.partial(_diff_prompt_kernel, lam_init=lam_init, tkb=tkb),
        out_shape=jax.ShapeDtypeStruct((b, s, B_V), F32),
        grid=(b, nq),
        in_specs=[blk(B_QK), full(B_QK), full(B_V), _const_spec(dtab.shape), _const_spec(lam_p.shape),
                  _const_spec(subln.shape)],
        out_specs=blk(B_V),
        scratch_shapes=_flash_scratch(2 * B_HEADS, TQ, B_VH),
        compiler_params=_cparams(("parallel", "arbitrary")),
        name="diff_prompt",
    )(qb, kb, vb, dtab, lam_p, subln)


def _page_specs(pps, rows, cols):
    return [pl.BlockSpec((1, rows, cols), functools.partial(lambda b, s, pt, j: (pt[b, s * pps + j], 0, 0), j=j))
            for j in range(pps)]


def _dsa_sample_select_kernel(pt_ref, qi_ref, kiwq_ref, kinew_ref, *rest, pps, npages, topk):
    del pt_ref
    page_refs = rest[:pps]
    tri_ref, selb_ref, skey_ref = rest[pps:]
    s = pl.program_id(1)
    qi = qi_ref[0].astype(BF16)
    wi = kiwq_ref[0][:, IDX_DIM:IDX_DIM + IDX_HEADS] * (IDX_HEADS ** -0.5)

    def scores(kt):
        sc = jnp.zeros((T_PAD, TK), F32)
        for h in range(IDX_HEADS):
            d = _dot(qi[:, h * IDX_DIM:(h + 1) * IDX_DIM], kt) * (IDX_DIM ** -0.5)
            sc = sc + jnp.maximum(d, 0.0) * wi[:, h:h + 1]
        return jnp.where(sc == 0.0, 0.0, sc)

    for j in range(pps):
        skey_ref[s * pps + j] = _float_key(scores(page_refs[j][0].astype(BF16)))

    @pl.when(s == pl.num_programs(1) - 1)
    def _():
        np1 = npages + 1
        row = lax.broadcasted_iota(I32, (T_PAD, TK), 0)
        col = lax.broadcasted_iota(I32, (T_PAD, TK), 1)
        sc_new = scores(kinew_ref[0].astype(BF16))
        skey_ref[npages] = jnp.where(col <= row, _float_key(sc_new), INT_MIN)
        sk = skey_ref[...]

        def count(pred):
            cnt = jnp.sum(jnp.where(pred, 1.0, 0.0), axis=0)
            return jnp.sum(cnt, axis=-1, keepdims=True)

        def bit_body(it, thr):
            cand = thr ^ lax.shift_left(jnp.int32(1), 31 - it)
            return jnp.where(count(sk >= cand[None]) >= topk, cand, thr)

        thr = lax.fori_loop(0, 32, bit_body, jnp.full((T_PAD, 1), INT_MIN, I32))
        need = float(topk) - count(sk > thr[None])
        eqf = jnp.where(sk == thr[None], 1.0, 0.0)
        before = _dot(eqf.reshape(np1 * T_PAD, TK).astype(BF16), tri_ref[...]).reshape(np1, T_PAD, TK)
        ties = jnp.sum(eqf, axis=-1, keepdims=True)
        off = jnp.zeros((T_PAD, 1), F32)
        for c in range(np1):
            selb_ref[0, c] = _select_mask(sk[c], thr, need, before[c] + off)
            off = off + ties[c]


def _dsa_sample_select(page_table, qi, kiw, ki_new_t, cache_ikt, n_new, pps):
    db, npages = page_table.shape
    topk = min(TOPK_MAX, (npages * TK + n_new) // 4)
    tri = (jnp.arange(TK)[:, None] < jnp.arange(TK)[None, :]).astype(BF16)
    per_b = lambda shape: pl.BlockSpec((1,) + shape, lambda b, s, pt: (b,) + (0,) * len(shape))
    grid_spec = pltpu.PrefetchScalarGridSpec(
        num_scalar_prefetch=1,
        grid=(db, npages // pps),
        in_specs=[per_b((T_PAD, IDX_Q)), per_b((T_PAD, LANES)), per_b((IDX_DIM, TK))]
        + _page_specs(pps, IDX_DIM, TK) + [pl.BlockSpec(tri.shape, lambda b, s, pt: (0, 0))],
        out_specs=per_b((npages + 1, T_PAD, TK)),
        scratch_shapes=[pltpu.VMEM((npages + 1, T_PAD, TK), I32)],
    )
    return pl.pallas_call(
        functools.partial(_dsa_sample_select_kernel, pps=pps, npages=npages, topk=topk),
        out_shape=jax.ShapeDtypeStruct((db, npages + 1, T_PAD, TK), F32),
        grid_spec=grid_spec,
        compiler_params=_cparams(("parallel", "arbitrary")),
        name="dsa_sample_select",
    )(page_table, qi, kiw, ki_new_t, *([cache_ikt] * pps), tri)


def _page_bias(btab_ref, c0, n, npages):
    return jnp.concatenate([btab_ref[jnp.where(c0 + j == npages - 1, 1, 0)] for j in range(n)], axis=1)


def _dsa_sample_attn_kernel(pt_ref, q_ref, knew_ref, vnew_ref, selb_ref, btab_ref, *rest, pps, npages):
    del pt_ref
    k_refs = rest[:pps]
    v_refs = rest[pps:2 * pps]
    o_ref, m_sc, l_sc, acc_sc = rest[2 * pps:]
    s = pl.program_id(1)

    @pl.when(s == 0)
    def _():
        _init_flash(m_sc, l_sc, acc_sc)

    scale = A_HEAD_DIM ** -0.5
    q = (q_ref[0] * scale).astype(BF16)
    rows = A_HEADS * T_PAD

    def step(kt_list, vt_list, bias, mask):
        n = len(kt_list)
        sc = jnp.concatenate([_dot(q, kt.astype(BF16)) for kt in kt_list], axis=1) + bias
        sc = (sc.reshape(A_HEADS, T_PAD, n * TK) + mask[None]).reshape(rows, n * TK)

        def pv(p):
            out = _dot_nt(p[:, 0:TK], vt_list[0].astype(BF16))
            for j in range(1, n):
                out = out + _dot_nt(p[:, j * TK:(j + 1) * TK], vt_list[j].astype(BF16))
            return out

        _flash_update(0, sc, pv, m_sc, l_sc, acc_sc)

    c0 = s * pps
    mask = jnp.concatenate([selb_ref[0, c0 + j] for j in range(pps)], axis=1)
    step([r[0] for r in k_refs], [r[0] for r in v_refs], _page_bias(btab_ref, c0, pps, npages), mask)

    @pl.when(s == pl.num_programs(1) - 1)
    def _():
        step([knew_ref[0]], [vnew_ref[0]], btab_ref[2], selb_ref[0, npages])
        o = acc_sc[0] / l_sc[0]
        for h in range(A_HEADS):
            g0 = (h // A_REP) * A_HEAD_DIM
            o_ref[0, :, h * A_HEAD_DIM:(h + 1) * A_HEAD_DIM] = o[h * T_PAD:(h + 1) * T_PAD, g0:g0 + A_HEAD_DIM]


def _dsa_sample_attn(page_table, q_bd, k_new_t, v_new_t, selb, btab, cache_kt, cache_vt, pps):
    db, npages = page_table.shape
    rows = A_HEADS * T_PAD
    per_b = lambda shape: pl.BlockSpec((1,) + shape, lambda b, s, pt: (b,) + (0,) * len(shape))
    grid_spec = pltpu.PrefetchScalarGridSpec(
        num_scalar_prefetch=1,
        grid=(db, npages // pps),
        in_specs=[per_b((rows, A_KV)), per_b((A_KV, TK)), per_b((A_KV, TK)), per_b((npages + 1, T_PAD, TK)),
                  pl.BlockSpec(btab.shape, lambda b, s, pt: (0, 0, 0))]
        + _page_specs(pps, A_KV, TK) + _page_specs(pps, A_KV, TK),
        out_specs=per_b((T_PAD, A_Q)),
        scratch_shapes=_flash_scratch(1, rows, A_KV),
    )
    return pl.pallas_call(
        functools.partial(_dsa_sample_attn_kernel, pps=pps, npages=npages),
        out_shape=jax.ShapeDtypeStruct((db, T_PAD, A_Q), F32),
        grid_spec=grid_spec,
        compiler_params=_cparams(("parallel", "arbitrary")),
        name="dsa_sample_attn",
    )(page_table, q_bd, k_new_t, v_new_t, selb, btab, *([cache_kt] * pps), *([cache_vt] * pps))


def _diff_sample_kernel(pt_ref, q_ref, knew_ref, vnew_ref, btab_ref, lp_ref, sg_ref, *rest, pps, npages, lam_init):
    del pt_ref
    k_refs = rest[:pps]
    v_refs = rest[pps:2 * pps]
    o_ref, m_sc, l_sc, acc_sc = rest[2 * pps:]
    s = pl.program_id(1)

    @pl.when(s == 0)
    def _():
        _init_flash(m_sc, l_sc, acc_sc)

    scale = B_HEAD_DIM ** -0.5
    q = (q_ref[0] * scale).astype(BF16)
    hrows = 2 * T_PAD

    def step(kt_list, v_fn, bias):
        n = len(kt_list)
        sc = jnp.concatenate([_dot(q, kt.astype(BF16)) for kt in kt_list], axis=1) + bias

        def pv(p):
            outs = []
            for h in range(B_HEADS):
                ph = p[h * hrows:(h + 1) * hrows]
                out = _dot(ph[:, 0:TK], v_fn(0, h))
                for j in range(1, n):
                    out = out + _dot(ph[:, j * TK:(j + 1) * TK], v_fn(j, h))
                outs.append(out)
            return jnp.concatenate(outs, axis=0)

        _flash_update(0, sc, pv, m_sc, l_sc, acc_sc)

    def page_v(j, h):
        return v_refs[j][0, pl.ds(h, TK, stride=B_HEADS), :].astype(BF16)

    step([r[0] for r in k_refs], page_v, _page_bias(btab_ref, s * pps, pps, npages))

    @pl.when(s == pl.num_programs(1) - 1)
    def _():
        step([knew_ref[0]], lambda j, h: vnew_ref[0, :, h * B_VH:(h + 1) * B_VH].astype(BF16), btab_ref[2])
        o = acc_sc[0] / l_sc[0]
        lam = _diff_lambda(lp_ref, lam_init)
        for h in range(B_HEADS):
            r0 = h * hrows
            o_ref[0, :, h * B_VH:(h + 1) * B_VH] = _diff_finish(
                o[r0:r0 + T_PAD], o[r0 + T_PAD:r0 + hrows], lam, sg_ref[...], lam_init)


def _diff_sample(page_table, q_bd, k_new_t, v_new, btab, lam_p, subln, cache_kt, cache_v2, lam_init, pps):
    db, npages = page_table.shape
    rows = 2 * B_HEADS * T_PAD
    per_b = lambda shape: pl.BlockSpec((1,) + shape, lambda b, s, pt: (b,) + (0,) * len(shape))
    const = lambda shape: pl.BlockSpec(shape, lambda b, s, pt: (0,) * len(shape))
    grid_spec = pltpu.PrefetchScalarGridSpec(
        num_scalar_prefetch=1,
        grid=(db, npages // pps),
        in_specs=[per_b((rows, B_QK)), per_b((B_QK, TK)), per_b((TK, B_V)), const(btab.shape),
                  const(lam_p.shape), const(subln.shape)]
        + _page_specs(pps, B_QK, TK) + _page_specs(pps, TK * B_HEADS, B_VH),
        out_specs=per_b((T_PAD, B_V)),
        scratch_shapes=_flash_scratch(1, rows, B_VH),
    )
    return pl.pallas_call(
        functools.partial(_diff_sample_kernel, pps=pps, npages=npages, lam_init=lam_init),
        out_shape=jax.ShapeDtypeStruct((db, T_PAD, B_V), F32),
        grid_spec=grid_spec,
        compiler_params=_cparams(("parallel", "arbitrary")),
        name="diff_sample",
    )(page_table, q_bd, k_new_t, v_new, btab, lam_p, subln, *([cache_kt] * pps), *([cache_v2] * pps))


def _mid_kernel(x_ref, ma_ref, mb_ref, mk_ref, mv_ref, wout_ref, gx_ref, wq_ref, gm_ref, qg_ref, wo_ref, h2_ref):
    h = (x_ref[0] + _dot(ma_ref[0].astype(BF16), wout_ref[:A_Q, :])
         + _dot(mb_ref[0].astype(BF16), wout_ref[A_Q:, :]))
    hn = _rms(h) * gx_ref[...]
    q = _dot(hn.astype(BF16), wq_ref[...])
    ms = _dot((q * q).astype(BF16), gm_ref[...])
    q = (q * lax.rsqrt(ms + EPS) * qg_ref[...]).astype(BF16)
    mk = mk_ref[0].astype(BF16)
    mv = mv_ref[0].astype(BF16)
    outs = []
    for hh in range(MEM_HEADS):
        sl = slice(hh * MEM_HEAD_DIM, (hh + 1) * MEM_HEAD_DIM)
        s = _dot_nt(q[:, sl], mk[:, sl]) * (MEM_HEAD_DIM ** -0.5)
        p = jnp.exp(s - jnp.max(s, axis=-1, keepdims=True))
        l = jnp.sum(p, axis=-1, keepdims=True)
        outs.append(_dot(p.astype(BF16), mv[:, sl]) / l)
    o = jnp.concatenate(outs, axis=-1)
    h2_ref[0] = h + _dot(o.astype(BF16), wo_ref[...])


def _mid(x, mix_a, mix_b, mk, mv, w_out, g_x, w_q, gmat, q_gain, w_o, tm):
    b, s, d = x.shape
    m_tok = mk.shape[1]
    blk = lambda w: pl.BlockSpec((1, tm, w), lambda bi, i: (bi, i, 0))
    per_b = lambda w: pl.BlockSpec((1, m_tok, w), lambda bi, i: (bi, 0, 0))
    return pl.pallas_call(
        _mid_kernel,
        out_shape=jax.ShapeDtypeStruct((b, s, d), F32),
        grid=(b, s // tm),
        in_specs=[blk(d), blk(A_Q), blk(B_V), per_b(MEM_W), per_b(MEM_W), _const_spec(w_out.shape),
                  _const_spec(g_x.shape), _const_spec(w_q.shape), _const_spec(gmat.shape),
                  _const_spec(q_gain.shape), _const_spec(w_o.shape)],
        out_specs=blk(d),
        compiler_params=_cparams(("parallel", "arbitrary")),
        name="mid",
    )(x, mix_a, mix_b, mk, mv, w_out, g_x, w_q, gmat, q_gain, w_o)


def _ffn_core(h, gn, wg_ref, wu_ref, cw_ref, cb_ref, wd_ref, shifted):
    xb = (_rms(h) * gn).astype(BF16)
    g = _dot(xb, wg_ref[...])
    u = _dot(xb, wu_ref[...])
    gm1, gm2 = shifted(g)
    gc = cb_ref[...] + cw_ref[0:1, :] * gm2 + cw_ref[1:2, :] * gm1 + cw_ref[2:3, :] * g
    a = gc / (1.0 + jnp.exp(-gc)) * u
    return h + _dot(a.astype(BF16), wd_ref[...]), g


def _ffn_prompt_kernel(h_ref, gn_ref, wg_ref, wu_ref, cw_ref, cb_ref, wd_ref, y_ref, tail_ref, carry_ref):
    @pl.when(pl.program_id(1) == 0)
    def _():
        carry_ref[...] = jnp.zeros(carry_ref.shape, F32)

    tm = h_ref.shape[1]
    row = lax.broadcasted_iota(I32, (tm, 1), 0)
    c0 = carry_ref[SUBLANES - 2:SUBLANES - 1, :]
    c1 = carry_ref[SUBLANES - 1:SUBLANES, :]

    def shifted(g):
        gm1 = jnp.where(row == 0, c1, pltpu.roll(g, 1, 0))
        gm2 = jnp.where(row == 0, c0, jnp.where(row == 1, c1, pltpu.roll(g, 2, 0)))
        return gm1, gm2

    y, g = _ffn_core(h_ref[0], gn_ref[...], wg_ref, wu_ref, cw_ref, cb_ref, wd_ref, shifted)
    y_ref[0] = y
    tail = g[tm - SUBLANES:, :]
    carry_ref[...] = tail
    tail_ref[0] = tail


def _ffn_prompt(h, gn, wg, wu, cw, cb, wd, tm):
    b, s, d = h.shape
    f = wg.shape[1]
    blk = pl.BlockSpec((1, tm, d), lambda bi, i: (bi, i, 0))
    wspec = lambda shape: _const_spec(shape, single_buffer=True)
    return pl.pallas_call(
        _ffn_prompt_kernel,
        out_shape=[jax.ShapeDtypeStruct((b, s, d), F32), jax.ShapeDtypeStruct((b, SUBLANES, f), F32)],
        grid=(b, s // tm),
        in_specs=[blk, _const_spec(gn.shape), wspec(wg.shape), wspec(wu.shape), _const_spec(cw.shape),
                  _const_spec(cb.shape), wspec(wd.shape)],
        out_specs=[blk, pl.BlockSpec((1, SUBLANES, f), lambda bi, i: (bi, 0, 0))],
        scratch_shapes=[pltpu.VMEM((SUBLANES, f), F32)],
        compiler_params=_cparams(("arbitrary", "arbitrary")),
        name="ffn_prompt",
    )(h, gn, wg, wu, cw, cb, wd)


def _ffn_sample_kernel(h_ref, gn_ref, wg_ref, wu_ref, cw_ref, cb_ref, wd_ref, st1_ref, st2_ref, y_ref, g_ref):
    m = h_ref.shape[0]
    t = lax.broadcasted_iota(I32, (m, 1), 0) & (T_PAD - 1)

    def shifted(g):
        gm1 = jnp.where(t == 0, st1_ref[...], pltpu.roll(g, 1, 0))
        gm2 = jnp.where(t < 2, st2_ref[...], pltpu.roll(g, 2, 0))
        return gm1, gm2

    y, g = _ffn_core(h_ref[...], gn_ref[...], wg_ref, wu_ref, cw_ref, cb_ref, wd_ref, shifted)
    y_ref[...] = y
    g_ref[...] = g


def _ffn_sample(h2d, gn, wg, wu, cw, cb, wd, st1, st2):
    m, d = h2d.shape
    f = wg.shape[1]
    wspec = lambda shape: _const_spec(shape, single_buffer=True)
    return pl.pallas_call(
        _ffn_sample_kernel,
        out_shape=[jax.ShapeDtypeStruct((m, d), F32), jax.ShapeDtypeStruct((m, f), F32)],
        grid=(1,),
        in_specs=[_const_spec((m, d)), _const_spec(gn.shape), wspec(wg.shape), wspec(wu.shape),
                  _const_spec(cw.shape), _const_spec(cb.shape), wspec(wd.shape), _const_spec((m, f)),
                  _const_spec((m, f))],
        out_specs=[_const_spec((m, d)), _const_spec((m, f))],
        compiler_params=_cparams(("arbitrary",)),
        name="ffn_sample",
    )(h2d, gn, wg, wu, cw, cb, wd, st1, st2)


def _rel_bucket(dist):
    n = jnp.maximum(dist, 0)
    max_exact = NUM_BUCKETS // 2
    nf = jnp.maximum(n, 1).astype(F32)
    log_b = jnp.log(nf / max_exact) / math.log(MAX_DISTANCE / max_exact) * (NUM_BUCKETS - max_exact)
    large = jnp.minimum(max_exact + log_b.astype(I32), NUM_BUCKETS - 1)
    return jnp.where(n < max_exact, n, large)


def _bias_by_dist(dist, causal, bias):
    onehot = jax.nn.one_hot(_rel_bucket(dist), NUM_BUCKETS, dtype=F32)
    vals = jnp.einsum("...k,kh->h...", onehot, bias, precision=lax.Precision.HIGHEST)
    return jnp.where(causal[None], vals, NEG).astype(F32)


def _prompt_bias_tables(bias):
    r = jnp.arange(TQ)[:, None]
    c = jnp.arange(TK)[None, :]
    always = jnp.ones((TQ, TK), bool)
    masked = _bias_by_dist(r - c, ~always, bias)
    t0 = _bias_by_dist(r - c, r >= c, bias)
    t1 = _bias_by_dist(r - c + TK, always, bias)
    t2 = _bias_by_dist(r - c + 2 * TK, always, bias)
    return jnp.stack([masked, t0, t1, t2], axis=1)


def _sample_bias_tables(bias, streams_per_head):
    t = jnp.arange(T_PAD)[:, None]
    c = jnp.arange(TK)[None, :]
    always = jnp.ones((T_PAD, TK), bool)
    far = _bias_by_dist(t - c + 2 * TK, always, bias)
    last = _bias_by_dist(t - c + TK, always, bias)
    new = _bias_by_dist(t - c, c <= t, bias)
    tabs = jnp.stack([far, last, new], axis=0)
    tabs = jnp.repeat(tabs[:, :, None], streams_per_head, axis=2)
    return tabs.reshape(3, -1, TK)


def _group_mean_matrix(width, group):
    idx = jnp.arange(width) // group
    return jnp.where(idx[:, None] == idx[None, :], 1.0 / group, 0.0).astype(BF16)


def _pad_rows(x, rows):
    return jnp.pad(x, ((0, 0), (0, rows - x.shape[1]), (0, 0)))


def _new_keys_t(x, rows):
    return jnp.swapaxes(_pad_rows(x, rows), 1, 2)


def kernel(x_prompt, x_sample, mem_prompt, cache_a_k, cache_a_v, cache_idx_k, cache_b_k, cache_b_v, cache_mem_k, cache_mem_v, state_ffn_conv, page_table, rel_bias, norm_mix, w_in, a_q_norm, a_k_norm, b_q_norm, b_k_norm, diff_lambda, diff_subln, w_out, norm_mem_x, norm_mem_src, w_mem_q, w_mem_kv, mem_q_norm, mem_k_norm, w_mem_o, norm_ffn, w_up, w_gate, ffn_conv_w, ffn_conv_b, w_down):
    depth = w_in.shape[0]
    assert depth == 1, "single-layer trunk"
    layer = 0
    lam_init = 0.8 - 0.6 * math.exp(-0.3 * layer)
    b, s, d = x_prompt.shape
    db, t_new, _ = x_sample.shape
    assert CONV_W - 1 <= t_new <= T_PAD and cache_a_k.shape[2] == TK
    m_tok = mem_prompt.shape[1]
    f = w_up.shape[-1]
    n_pool = cache_a_k.shape[1]
    tkb = min(TKB_MAX, s)
    assert s % tkb == 0 and tkb % TK == 0

    w_in_l = w_in[layer]
    n_front = A_Q + 2 * A_KV + IDX_Q + IDX_DIM + IDX_HEADS
    w_in_p = jnp.concatenate(
        [w_in_l[:, :n_front], jnp.zeros((d, C_QB - n_front), F32), w_in_l[:, n_front:]], axis=1).astype(BF16)
    ones = lambda n: jnp.ones((n,), F32)
    hgain = jnp.concatenate([
        jnp.tile(a_q_norm[layer], A_HEADS), jnp.tile(a_k_norm[layer], A_KV_HEADS), ones(C_QB - C_VA),
        jnp.tile(b_q_norm[layer], 2 * B_HEADS), jnp.tile(b_k_norm[layer], 2 * B_HEADS), ones(B_V)])[None, :]
    gmat64 = _group_mean_matrix(A_Q, A_HEAD_DIM)
    gmat128 = _group_mean_matrix(MEM_W, MEM_HEAD_DIM)
    proj_segs = ((C_QA, A_Q, True, A_Q), (C_KA, A_KV, True, A_KV), (C_VA, A_KV, False, A_KV),
                 (C_QI, IDX_Q, False, IDX_Q), (C_KIW, LANES, False, LANES), (C_KIW, LANES, False, IDX_DIM),
                 (C_QB, B_QK, True, B_QK), (C_KB, B_QK, True, B_QK), (C_VB, B_V, False, B_V))
    g_mix = norm_mix[layer][None, :]
    w_out_b = w_out[layer].astype(BF16)
    w_q_b = w_mem_q[layer].astype(BF16)
    w_o_b = w_mem_o[layer].astype(BF16)
    w_kv_b = w_mem_kv[layer].astype(BF16)
    w_gate_b = w_gate[layer].astype(BF16)
    w_up_b = w_up[layer].astype(BF16)
    w_down_b = w_down[layer].astype(BF16)
    g_memx = norm_mem_x[layer][None, :]
    q_gain = jnp.tile(mem_q_norm[layer], MEM_HEADS)[None, :]
    kv_gain = jnp.concatenate([jnp.tile(mem_k_norm[layer], MEM_HEADS), ones(MEM_W)])[None, :]
    g_ffn = norm_ffn[layer][None, :]
    conv_w = ffn_conv_w[layer]
    conv_b = ffn_conv_b[layer][None, :]
    lam_p = diff_lambda[layer]
    subln = diff_subln[layer][None, :]
    bias_a = rel_bias[:, :A_HEADS]
    bias_b = rel_bias[:, A_HEADS:]

    qa, ka, va, qi, kiw, ki, qb, kb, vb = _norm_proj(
        x_prompt.reshape(b * s, d), g_mix, w_in_p, gmat64, hgain, proj_segs, 256, "proj_prompt")
    r3 = lambda a: a.reshape(b, s, a.shape[-1])
    mix_a = _dsa_prompt(r3(qi), r3(kiw), r3(qa), r3(ka), r3(va), _prompt_bias_tables(bias_a), tkb)
    mix_b = _diff_prompt(r3(qb), r3(kb), r3(vb), _prompt_bias_tables(bias_b), lam_p, subln, lam_init, tkb)
    mk, mv = _norm_proj(mem_prompt.reshape(b * m_tok, d), norm_mem_src[layer][None, :], w_kv_b, gmat128, kv_gain,
                        ((0, MEM_W, True, MEM_W), (MEM_W, MEM_W, False, MEM_W)), 256, "mem_kv")
    h2 = _mid(x_prompt, mix_a, mix_b, mk.reshape(b, m_tok, MEM_W), mv.reshape(b, m_tok, MEM_W),
              w_out_b, g_memx, w_q_b, gmat128, q_gain, w_o_b, 256)
    yp, tail = _ffn_prompt(h2, g_ffn, w_gate_b, w_up_b, conv_w, conv_b, w_down_b, 256)
    conv_p = tail[:, SUBLANES - (CONV_W - 1):, :]

    xs = _pad_rows(x_sample, T_PAD)
    qa, ka_s, va_s, qi, kiw, ki_s, qb, kb_s, vb_s = _norm_proj(
        xs.reshape(db * T_PAD, d), g_mix, w_in_p, gmat64, hgain, proj_segs, db * T_PAD, "proj_sample")
    r3 = lambda a: a.reshape(db, T_PAD, a.shape[-1])
    idx_kt = jnp.transpose(cache_idx_k[layer], (0, 2, 1))
    a_kt = jnp.transpose(cache_a_k[layer], (0, 2, 3, 1)).reshape(n_pool, A_KV, TK)
    a_vt = jnp.transpose(cache_a_v[layer], (0, 2, 3, 1)).reshape(n_pool, A_KV, TK)
    b_kt = jnp.transpose(cache_b_k[layer], (0, 2, 3, 4, 1)).reshape(n_pool, B_QK, TK)
    b_v2 = cache_b_v[layer].reshape(n_pool, TK * B_HEADS, B_VH)
    selb = _dsa_sample_select(page_table, r3(qi), r3(kiw), _new_keys_t(r3(ki_s), TK), idx_kt, t_new, 8)
    q_rows = r3(qa).reshape(db, T_PAD, A_HEADS, A_HEAD_DIM).transpose(0, 2, 1, 3)
    eye_g = jnp.repeat(jnp.eye(A_KV_HEADS, dtype=F32), A_REP, axis=0)
    qa_bd = (q_rows[:, :, :, None, :] * eye_g[None, :, None, :, None]).reshape(db, A_HEADS * T_PAD, A_KV)
    mix_a = _dsa_sample_attn(page_table, qa_bd, _new_keys_t(r3(ka_s), TK), _new_keys_t(r3(va_s), TK), selb,
                             _sample_bias_tables(bias_a, 1), a_kt, a_vt, 8)
    n_str = 2 * B_HEADS
    q_rows = r3(qb).reshape(db, T_PAD, n_str, B_HEAD_DIM).transpose(0, 2, 1, 3)
    qb_bd = (q_rows[:, :, :, None, :] * jnp.eye(n_str, dtype=F32)[None, :, None, :, None]).reshape(
        db, n_str * T_PAD, B_QK)
    mix_b = _diff_sample(page_table, qb_bd, _new_keys_t(r3(kb_s), TK), _pad_rows(r3(vb_s), TK),
                         _sample_bias_tables(bias_b, 2), lam_p, subln, b_kt, b_v2, lam_init, 8)
    h2 = _mid(xs, mix_a, mix_b, cache_mem_k[layer].reshape(db, m_tok, MEM_W),
              cache_mem_v[layer].reshape(db, m_tok, MEM_W), w_out_b, g_memx, w_q_b, gmat128, q_gain, w_o_b, T_PAD)
    state = state_ffn_conv[layer]
    st1 = _pad_rows(state[:, 1:2, :], T_PAD).reshape(db * T_PAD, f)
    st2 = _pad_rows(state, T_PAD).reshape(db * T_PAD, f)
    ys, g_s = _ffn_sample(h2.reshape(db * T_PAD, d), g_ffn, w_gate_b, w_up_b, conv_w, conv_b, w_down_b, st1, st2)
    ys = ys.reshape(db, T_PAD, d)[:, :t_new]
    conv_s = g_s.reshape(db, T_PAD, f)[:, t_new - (CONV_W - 1):t_new]

    def new_rows(a, shape):
        return a.reshape(db, T_PAD, -1)[:, :t_new].reshape((1, db, t_new) + shape)

    return (yp, ys,
            ka.reshape(1, b, s, A_KV_HEADS, A_HEAD_DIM), va.reshape(1, b, s, A_KV_HEADS, A_HEAD_DIM),
            ki.reshape(1, b, s, IDX_DIM), kb.reshape(1, b, s, B_HEADS, 2, B_HEAD_DIM),
            vb.reshape(1, b, s, B_HEADS, 2 * B_HEAD_DIM),
            mk.reshape(1, b, m_tok, MEM_HEADS, MEM_HEAD_DIM), mv.reshape(1, b, m_tok, MEM_HEADS, MEM_HEAD_DIM),
            conv_p[None],
            new_rows(ka_s, (A_KV_HEADS, A_HEAD_DIM)), new_rows(va_s, (A_KV_HEADS, A_HEAD_DIM)),
            new_rows(ki_s, (IDX_DIM,)), new_rows(kb_s, (B_HEADS, 2, B_HEAD_DIM)),
            new_rows(vb_s, (B_HEADS, 2 * B_HEAD_DIM)), conv_s[None])
```

```python
import functools
import math

import jax
import jax.numpy as jnp
import numpy as np
from jax import lax
from jax.experimental import pallas as pl
from jax.experimental.pallas import tpu as pltpu

F32 = jnp.float32
BF16 = jnp.bfloat16
I32 = jnp.int32

EPS = 1e-6
NEG = -1e30
INT_MIN = -(2 ** 31)

A_HEADS = 8
A_KV_HEADS = 2
A_HEAD_DIM = 64
IDX_HEADS = 4
IDX_DIM = 64
TOPK_MAX = 256
B_HEADS = 4
B_HEAD_DIM = 64
MEM_HEADS = 4
MEM_HEAD_DIM = 128
CONV_W = 3
NUM_BUCKETS = 32
MAX_DISTANCE = 128

A_Q = A_HEADS * A_HEAD_DIM
A_KV = A_KV_HEADS * A_HEAD_DIM
IDX_Q = IDX_HEADS * IDX_DIM
B_QK = B_HEADS * 2 * B_HEAD_DIM
B_V = B_HEADS * 2 * B_HEAD_DIM
B_VH = 2 * B_HEAD_DIM
MEM_W = MEM_HEADS * MEM_HEAD_DIM
A_REP = A_HEADS // A_KV_HEADS

LANES = 128
SUBLANES = 8
TQ = 128
TK = 128
TKB_MAX = 512
T_PAD = SUBLANES
VMEM_LIMIT = 56 * 1024 * 1024

C_QA = 0
C_KA = C_QA + A_Q
C_VA = C_KA + A_KV
C_QI = C_VA + A_KV
C_KIW = C_QI + IDX_Q
C_QB = C_KIW + LANES
C_KB = C_QB + B_QK
C_VB = C_KB + B_QK
D_IN_PAD = C_VB + B_V


def _dot(a, b):
    return jnp.dot(a, b, preferred_element_type=F32)


def _dot_nt(a, b):
    return lax.dot_general(a, b, (((1,), (1,)), ((), ())), preferred_element_type=F32)


def _rms(x):
    return x * lax.rsqrt(jnp.mean(x * x, axis=-1, keepdims=True) + EPS)


KEY_NEG_INF = -(2 ** 31) + 0x7FFFFF


def _key_to_float(key):
    bits = key ^ ((key >> 31) & 0x7FFFFFFF)
    return jnp.where(key <= KEY_NEG_INF, -jnp.inf, lax.bitcast_convert_type(bits, F32))


def _kth_largest(count_ge, shape, topk):
    def bit_body(it, key):
        cand = key ^ lax.shift_left(jnp.int32(1), 31 - it)
        return jnp.where(count_ge(_key_to_float(cand)) >= topk, cand, key)

    return _key_to_float(lax.fori_loop(0, 32, bit_body, jnp.full(shape, INT_MIN, I32)))


def _cparams(sem, vmem=VMEM_LIMIT):
    return pltpu.CompilerParams(dimension_semantics=sem, vmem_limit_bytes=vmem)


def _const_spec(shape, single_buffer=False):
    nd = len(shape)
    if single_buffer:
        return pl.BlockSpec(shape, lambda *_: (0,) * nd, pipeline_mode=pl.Buffered(1))
    return pl.BlockSpec(shape, lambda *_: (0,) * nd)


def _norm_proj_kernel(x_ref, g_ref, w_ref, gm_ref, hg_ref, *out_refs, segs):
    xn = _rms(x_ref[...]) * g_ref[...]
    p = _dot(xn.astype(BF16), w_ref[...])
    for (start, width, norm, out_width), o_ref in zip(segs, out_refs):
        s = p[:, start:start + width]
        if norm:
            ms = _dot((s * s).astype(BF16), gm_ref[:width, :width])
            s = s * lax.rsqrt(ms + EPS) * hg_ref[:, start:start + width]
        o_ref[...] = s[:, :out_width]


def _norm_proj(x2d, gain, w, gmat, hgain, segs, tm, name):
    m, d = x2d.shape
    n = w.shape[1]
    return pl.pallas_call(
        functools.partial(_norm_proj_kernel, segs=segs),
        out_shape=[jax.ShapeDtypeStruct((m, ow), F32) for (_, _, _, ow) in segs],
        grid=(m // tm,),
        in_specs=[pl.BlockSpec((tm, d), lambda i: (i, 0)), _const_spec((1, d)), _const_spec((d, n)),
                  _const_spec(gmat.shape), _const_spec((1, n))],
        out_specs=[pl.BlockSpec((tm, ow), lambda i: (i, 0)) for (_, _, _, ow) in segs],
        compiler_params=_cparams(("parallel",)),
        name=name,
    )(x2d, gain, w, gmat, hgain)


def _init_flash(m_sc, l_sc, acc_sc):
    m_sc[...] = jnp.full(m_sc.shape, -jnp.inf, F32)
    l_sc[...] = jnp.zeros(l_sc.shape, F32)
    acc_sc[...] = jnp.zeros(acc_sc.shape, F32)


def _flash_update(st, s, pv_fn, m_sc, l_sc, acc_sc):
    m_old = m_sc[st]
    m_new = jnp.maximum(m_old, jnp.max(s, axis=-1, keepdims=True))
    alpha = jnp.exp(m_old - m_new)
    p = jnp.exp(s - m_new)
    l_sc[st] = alpha * l_sc[st] + jnp.sum(p, axis=-1, keepdims=True)
    acc_sc[st] = alpha * acc_sc[st] + pv_fn(p.astype(BF16))
    m_sc[st] = m_new


def _flash_scratch(streams, rows, dv):
    return [pltpu.VMEM((streams, rows, 1), F32), pltpu.VMEM((streams, rows, 1), F32),
            pltpu.VMEM((streams, rows, dv), F32)]


def _select_mask(sk, thr, need, before):
    tie_ok = jnp.where(sk == thr, jnp.where(before < need, 1.0, 0.0), 0.0)
    real = jnp.where(sk > -jnp.inf, 1.0, 0.0)
    sel = jnp.where(sk > thr, real, tie_ok * real)
    return jnp.where(sel > 0.5, 0.0, NEG)


def _chunk_bias(dtab_ref, h, i, c, ksub):
    return jnp.concatenate([dtab_ref[h, jnp.clip(i - (c * ksub + j) + 1, 0, 3)] for j in range(ksub)], axis=0)


def _flash_update_t(st, s, vt, m_sc, l_sc, acc_sc):
    m_old = m_sc[st]
    m_new = jnp.maximum(m_old, jnp.max(_col_reduce(s, jnp.max), axis=0, keepdims=True))
    alpha = jnp.exp(m_old - m_new)
    p = jnp.exp(s - m_new)
    l_sc[st] = alpha * l_sc[st] + jnp.sum(_col_reduce(p, jnp.sum), axis=0, keepdims=True)
    acc_sc[st] = alpha * acc_sc[st] + _dot(vt, p.astype(BF16))
    m_sc[st] = m_new


def _flash_scratch_t(streams, dv, cols):
    return [pltpu.VMEM((streams, 1, cols), F32), pltpu.VMEM((streams, 1, cols), F32),
            pltpu.VMEM((streams, dv, cols), F32)]


REDUCE_WAYS = 8


def _col_reduce(x, op):
    rows, cols = x.shape
    slabs = rows // SUBLANES
    if slabs % REDUCE_WAYS == 0 and slabs > REDUCE_WAYS:
        x = op(x.reshape(REDUCE_WAYS, slabs // REDUCE_WAYS, SUBLANES, cols), axis=1)
    else:
        x = x.reshape(slabs, SUBLANES, cols)
    return op(x, axis=0)


def _col_count(w):
    return _col_reduce(w, jnp.sum)


def _dsa_prompt_kernel(qi_ref, wit_ref, kiw_ref, qa_ref, ka_ref, vat_ref, dtab_ref, tril_ref, o_ref,
                       skey_ref, selb_ref, m_sc, l_sc, acc_sc, *, topk, tkb):
    i = pl.program_id(1)
    ksub = tkb // TK
    nbig = i // ksub + 1
    qi = qi_ref[0].astype(BF16)
    wit = wit_ref[0] * (IDX_HEADS ** -0.5)
    krow = lax.broadcasted_iota(I32, (tkb, TQ), 0)
    qcol = lax.broadcasted_iota(I32, (tkb, TQ), 1)

    def score_body(c, carry):
        off = pl.multiple_of(c * tkb, tkb)
        kc = kiw_ref[0, pl.ds(off, tkb), 0:IDX_DIM].astype(BF16)
        sc = jnp.zeros((tkb, TQ), F32)
        for h in range(IDX_HEADS):
            d = _dot_nt(kc, qi[:, h * IDX_DIM:(h + 1) * IDX_DIM]) * (IDX_DIM ** -0.5)
            sc = sc + jnp.maximum(d, 0.0) * wit[h:h + 1, :]
        causal = (c * tkb + krow) <= (i * TQ + qcol)
        skey_ref[c] = jnp.where(causal, sc, -jnp.inf)
        return carry

    lax.fori_loop(0, nbig, score_body, 0)

    def count(pred_fn):
        def body(c, cnt):
            return cnt + _col_count(jnp.where(pred_fn(skey_ref[c]), 1.0, 0.0))
        cnt = lax.fori_loop(0, nbig, body, jnp.zeros((SUBLANES, TQ), F32))
        return jnp.sum(cnt, axis=0, keepdims=True)

    thr = _kth_largest(lambda t: count(lambda sk: sk >= t), (1, TQ), topk)
    need = float(topk) - count(lambda sk: sk > thr)
    tril = tril_ref[...]

    def sel_body(c, off):
        sk = skey_ref[c]
        eqf = jnp.where(sk == thr, 1.0, 0.0)
        before = _dot(tril, eqf.astype(BF16)) + off
        selb_ref[c] = _select_mask(sk, thr, need, before)
        return off + jnp.sum(_col_count(eqf), axis=0, keepdims=True)

    lax.fori_loop(0, nbig, sel_body, jnp.zeros((1, TQ), F32))

    _init_flash(m_sc, l_sc, acc_sc)
    scale = A_HEAD_DIM ** -0.5

    def att_body(c, carry):
        off = pl.multiple_of(c * tkb, tkb)
        maskb = selb_ref[c]
        for g in range(A_KV_HEADS):
            g0 = g * A_HEAD_DIM
            kc = ka_ref[0, pl.ds(off, tkb), g0:g0 + A_HEAD_DIM].astype(BF16)
            vt = vat_ref[0, c, g0:g0 + A_HEAD_DIM, :].astype(BF16)
            for r in range(A_REP):
                h = g * A_REP + r
                qh = (qa_ref[0, :, h * A_HEAD_DIM:(h + 1) * A_HEAD_DIM] * scale).astype(BF16)
                s = _dot_nt(kc, qh) + (_chunk_bias(dtab_ref, h, i, c, ksub) + maskb)
                _flash_update_t(h, s, vt, m_sc, l_sc, acc_sc)
        return carry

    lax.fori_loop(0, nbig, att_body, 0)
    for h in range(0, A_HEADS, 2):
        ot = jnp.concatenate([acc_sc[h] / l_sc[h], acc_sc[h + 1] / l_sc[h + 1]], axis=0)
        o_ref[0, :, h * A_HEAD_DIM:(h + 2) * A_HEAD_DIM] = ot.T


def _chunked_t(x, tkb):
    b, s, w = x.shape
    return x.reshape(b, s // tkb, tkb, w).transpose(0, 1, 3, 2)


def _dsa_prompt(qi, kiw, qa, ka, va, dtab, tkb):
    b, s, _ = qa.shape
    nq = s // TQ
    nc = s // tkb
    topk = min(TOPK_MAX, s // 4)
    tril = (jnp.arange(tkb)[None, :] < jnp.arange(tkb)[:, None]).astype(BF16)
    wit = jnp.swapaxes(kiw[:, :, IDX_DIM:IDX_DIM + SUBLANES], 1, 2)
    vat = _chunked_t(va, tkb)
    blk = lambda w: pl.BlockSpec((1, TQ, w), lambda bi, i: (bi, i, 0))
    full = lambda w: pl.BlockSpec((1, s, w), lambda bi, i: (bi, 0, 0))
    return pl.pallas_call(
        functools.partial(_dsa_prompt_kernel, topk=topk, tkb=tkb),
        out_shape=jax.ShapeDtypeStruct((b, s, A_Q), F32),
        grid=(b, nq),
        in_specs=[blk(IDX_Q), pl.BlockSpec((1, SUBLANES, TQ), lambda bi, i: (bi, 0, i)), full(LANES), blk(A_Q),
                  full(A_KV), pl.BlockSpec((1, nc, A_KV, tkb), lambda bi, i: (bi, 0, 0, 0)),
                  _const_spec(dtab.shape), _const_spec(tril.shape)],
        out_specs=blk(A_Q),
        scratch_shapes=[pltpu.VMEM((nc, tkb, TQ), F32), pltpu.VMEM((nc, tkb, TQ), F32)]
        + _flash_scratch_t(A_HEADS, A_HEAD_DIM, TQ),
        compiler_params=_cparams(("parallel", "arbitrary")),
        name="dsa_prompt",
    )(qi, wit, kiw, qa, ka, vat, dtab, tril)


def _diff_lambda(lp_ref, lam_init):
    lp = lp_ref[...]
    s1 = jnp.sum(lp[0:1] * lp[1:2], axis=-1, keepdims=True)
    s2 = jnp.sum(lp[2:3] * lp[3:4], axis=-1, keepdims=True)
    return jnp.exp(s1) - jnp.exp(s2) + lam_init


def _diff_finish(o0, o1, lam, sg, lam_init):
    o = o0 - lam * o1
    return _rms(o) * sg * (1.0 - lam_init)


def _diff_prompt_kernel(q_ref, k_ref, vt_ref, dtab_ref, lp_ref, sg_ref, o_ref, m_sc, l_sc, acc_sc, *, lam_init, tkb):
    i = pl.program_id(1)
    ksub = tkb // TK
    nbig = i // ksub + 1
    scale = B_HEAD_DIM ** -0.5
    _init_flash(m_sc, l_sc, acc_sc)

    def body(c, carry):
        off = pl.multiple_of(c * tkb, tkb)
        for h in range(B_HEADS):
            bias = _chunk_bias(dtab_ref, h, i, c, ksub)
            vt = vt_ref[0, c, h * B_VH:(h + 1) * B_VH, :].astype(BF16)
            for comp in range(2):
                st = h * 2 + comp
                c0 = st * B_HEAD_DIM
                qh = (q_ref[0, :, c0:c0 + B_HEAD_DIM] * scale).astype(BF16)
                kc = k_ref[0, pl.ds(off, tkb), c0:c0 + B_HEAD_DIM].astype(BF16)
                s = _dot_nt(kc, qh) + bias
                _flash_update_t(st, s, vt, m_sc, l_sc, acc_sc)
        return carry

    lax.fori_loop(0, nbig, body, 0)
    lam = _diff_lambda(lp_ref, lam_init)
    for h in range(B_HEADS):
        o0 = (acc_sc[2 * h] / l_sc[2 * h]).T
        o1 = (acc_sc[2 * h + 1] / l_sc[2 * h + 1]).T
        o_ref[0, :, h * B_VH:(h + 1) * B_VH] = _diff_finish(o0, o1, lam, sg_ref[...], lam_init)


def _diff_prompt(qb, kb, vb, dtab, lam_p, subln, lam_init, tkb):
    b, s, _ = qb.shape
    nq = s // TQ
    nc = s // tkb
    blk = lambda w: pl.BlockSpec((1, TQ, w), lambda bi, i: (bi, i, 0))
    full = lambda w: pl.BlockSpec((1, s, w), lambda bi, i: (bi, 0, 0))
    return pl.pallas_call(
        functools.partial(_diff_prompt_kernel, lam_init=lam_init, tkb=tkb),
        out_shape=jax.ShapeDtypeStruct((b, s, B_V), F32),
        grid=(b, nq),
        in_specs=[blk(B_QK), full(B_QK), pl.BlockSpec((1, nc, B_V, tkb), lambda bi, i: (bi, 0, 0, 0)),
                  _const_spec(dtab.shape), _const_spec(lam_p.shape), _const_spec(subln.shape)],
        out_specs=blk(B_V),
        scratch_shapes=_flash_scratch_t(2 * B_HEADS, B_VH, TQ),
        compiler_params=_cparams(("parallel", "arbitrary")),
        name="diff_prompt",
    )(qb, kb, _chunked_t(vb, tkb), dtab, lam_p, subln)


def _page_specs(pps, rows, cols):
    return [pl.BlockSpec((1, rows, cols), functools.partial(lambda b, s, pt, j: (pt[b, s * pps + j], 0, 0), j=j))
            for j in range(pps)]


def _dsa_sample_select_kernel(pt_ref, qi_ref, kiwq_ref, kinew_ref, *rest, pps, npages, topk):
    del pt_ref
    page_refs = rest[:pps]
    tri_ref, selb_ref, skey_ref = rest[pps:]
    s = pl.program_id(1)
    qi = qi_ref[0].astype(BF16)
    wi = kiwq_ref[0][:, IDX_DIM:IDX_DIM + IDX_HEADS] * (IDX_HEADS ** -0.5)

    def scores(kt):
        sc = jnp.zeros((T_PAD, TK), F32)
        for h in range(IDX_HEADS):
            d = _dot(qi[:, h * IDX_DIM:(h + 1) * IDX_DIM], kt) * (IDX_DIM ** -0.5)
            sc = sc + jnp.maximum(d, 0.0) * wi[:, h:h + 1]
        return sc

    for j in range(pps):
        skey_ref[s * pps + j] = scores(page_refs[j][0].astype(BF16))

    @pl.when(s == pl.num_programs(1) - 1)
    def _():
        np1 = npages + 1
        row = lax.broadcasted_iota(I32, (T_PAD, TK), 0)
        col = lax.broadcasted_iota(I32, (T_PAD, TK), 1)
        sc_new = scores(kinew_ref[0].astype(BF16))
        skey_ref[npages] = jnp.where(col <= row, sc_new, -jnp.inf)
        sk = skey_ref[...]

        def count(pred):
            cnt = jnp.sum(jnp.where(pred, 1.0, 0.0), axis=0)
            return jnp.sum(cnt, axis=-1, keepdims=True)

        thr = _kth_largest(lambda t: count(sk >= t[None]), (T_PAD, 1), topk)
        need = float(topk) - count(sk > thr[None])
        eqf = jnp.where(sk == thr[None], 1.0, 0.0)
        before = _dot(eqf.reshape(np1 * T_PAD, TK).astype(BF16), tri_ref[...]).reshape(np1, T_PAD, TK)
        ties = jnp.sum(eqf, axis=-1, keepdims=True)
        off = jnp.zeros((T_PAD, 1), F32)
        for c in range(np1):
            selb_ref[0, c] = _select_mask(sk[c], thr, need, before[c] + off)
            off = off + ties[c]


def _dsa_sample_select(page_table, qi, kiw, ki_new_t, cache_ikt, n_new, pps):
    db, npages = page_table.shape
    topk = min(TOPK_MAX, (npages * TK + n_new) // 4)
    tri = (jnp.arange(TK)[:, None] < jnp.arange(TK)[None, :]).astype(BF16)
    per_b = lambda shape: pl.BlockSpec((1,) + shape, lambda b, s, pt: (b,) + (0,) * len(shape))
    grid_spec = pltpu.PrefetchScalarGridSpec(
        num_scalar_prefetch=1,
        grid=(db, npages // pps),
        in_specs=[per_b((T_PAD, IDX_Q)), per_b((T_PAD, LANES)), per_b((IDX_DIM, TK))]
        + _page_specs(pps, IDX_DIM, TK) + [pl.BlockSpec(tri.shape, lambda b, s, pt: (0, 0))],
        out_specs=per_b((npages + 1, T_PAD, TK)),
        scratch_shapes=[pltpu.VMEM((npages + 1, T_PAD, TK), F32)],
    )
    return pl.pallas_call(
        functools.partial(_dsa_sample_select_kernel, pps=pps, npages=npages, topk=topk),
        out_shape=jax.ShapeDtypeStruct((db, npages + 1, T_PAD, TK), F32),
        grid_spec=grid_spec,
        compiler_params=_cparams(("parallel", "arbitrary")),
        name="dsa_sample_select",
    )(page_table, qi, kiw, ki_new_t, *([cache_ikt] * pps), tri)


def _page_bias(btab_ref, c0, n, npages):
    return jnp.concatenate([btab_ref[jnp.where(c0 + j == npages - 1, 1, 0)] for j in range(n)], axis=1)


def _dsa_sample_attn_kernel(pt_ref, q_ref, knew_ref, vnew_ref, selb_ref, btab_ref, *rest, pps, npages):
    del pt_ref
    k_refs = rest[:pps]
    v_refs = rest[pps:2 * pps]
    o_ref, m_sc, l_sc, acc_sc = rest[2 * pps:]
    s = pl.program_id(1)

    @pl.when(s == 0)
    def _():
        _init_flash(m_sc, l_sc, acc_sc)

    scale = A_HEAD_DIM ** -0.5
    q = (q_ref[0] * scale).astype(BF16)
    rows = A_HEADS * T_PAD

    def step(kt_list, vt_list, bias, mask):
        n = len(kt_list)
        sc = jnp.concatenate([_dot(q, kt.astype(BF16)) for kt in kt_list], axis=1) + bias
        sc = (sc.reshape(A_HEADS, T_PAD, n * TK) + mask[None]).reshape(rows, n * TK)

        def pv(p):
            out = _dot_nt(p[:, 0:TK], vt_list[0].astype(BF16))
            for j in range(1, n):
                out = out + _dot_nt(p[:, j * TK:(j + 1) * TK], vt_list[j].astype(BF16))
            return out

        _flash_update(0, sc, pv, m_sc, l_sc, acc_sc)

    c0 = s * pps
    mask = jnp.concatenate([selb_ref[0, c0 + j] for j in range(pps)], axis=1)
    step([r[0] for r in k_refs], [r[0] for r in v_refs], _page_bias(btab_ref, c0, pps, npages), mask)

    @pl.when(s == pl.num_programs(1) - 1)
    def _():
        step([knew_ref[0]], [vnew_ref[0]], btab_ref[2], selb_ref[0, npages])
        o = acc_sc[0] / l_sc[0]
        for h in range(A_HEADS):
            g0 = (h // A_REP) * A_HEAD_DIM
            o_ref[0, :, h * A_HEAD_DIM:(h + 1) * A_HEAD_DIM] = o[h * T_PAD:(h + 1) * T_PAD, g0:g0 + A_HEAD_DIM]


def _dsa_sample_attn(page_table, q_bd, k_new_t, v_new_t, selb, btab, cache_kt, cache_vt, pps):
    db, npages = page_table.shape
    rows = A_HEADS * T_PAD
    per_b = lambda shape: pl.BlockSpec((1,) + shape, lambda b, s, pt: (b,) + (0,) * len(shape))
    grid_spec = pltpu.PrefetchScalarGridSpec(
        num_scalar_prefetch=1,
        grid=(db, npages // pps),
        in_specs=[per_b((rows, A_KV)), per_b((A_KV, TK)), per_b((A_KV, TK)), per_b((npages + 1, T_PAD, TK)),
                  pl.BlockSpec(btab.shape, lambda b, s, pt: (0, 0, 0))]
        + _page_specs(pps, A_KV, TK) + _page_specs(pps, A_KV, TK),
        out_specs=per_b((T_PAD, A_Q)),
        scratch_shapes=_flash_scratch(1, rows, A_KV),
    )
    return pl.pallas_call(
        functools.partial(_dsa_sample_attn_kernel, pps=pps, npages=npages),
        out_shape=jax.ShapeDtypeStruct((db, T_PAD, A_Q), F32),
        grid_spec=grid_spec,
        compiler_params=_cparams(("parallel", "arbitrary")),
        name="dsa_sample_attn",
    )(page_table, q_bd, k_new_t, v_new_t, selb, btab, *([cache_kt] * pps), *([cache_vt] * pps))


def _diff_sample_kernel(pt_ref, q_ref, knew_ref, vnew_ref, btab_ref, lp_ref, sg_ref, *rest, pps, npages, lam_init):
    del pt_ref
    k_refs = rest[:pps]
    v_refs = rest[pps:2 * pps]
    o_ref, m_sc, l_sc, acc_sc = rest[2 * pps:]
    s = pl.program_id(1)

    @pl.when(s == 0)
    def _():
        _init_flash(m_sc, l_sc, acc_sc)

    scale = B_HEAD_DIM ** -0.5
    q = (q_ref[0] * scale).astype(BF16)
    hrows = 2 * T_PAD

    def step(kt_list, v_fn, bias):
        n = len(kt_list)
        sc = jnp.concatenate([_dot(q, kt.astype(BF16)) for kt in kt_list], axis=1) + bias

        def pv(p):
            outs = []
            for h in range(B_HEADS):
                ph = p[h * hrows:(h + 1) * hrows]
                out = _dot(ph[:, 0:TK], v_fn(0, h))
                for j in range(1, n):
                    out = out + _dot(ph[:, j * TK:(j + 1) * TK], v_fn(j, h))
                outs.append(out)
            return jnp.concatenate(outs, axis=0)

        _flash_update(0, sc, pv, m_sc, l_sc, acc_sc)

    def page_v(j, h):
        return v_refs[j][0, pl.ds(h, TK, stride=B_HEADS), :].astype(BF16)

    step([r[0] for r in k_refs], page_v, _page_bias(btab_ref, s * pps, pps, npages))

    @pl.when(s == pl.num_programs(1) - 1)
    def _():
        step([knew_ref[0]], lambda j, h: vnew_ref[0, :, h * B_VH:(h + 1) * B_VH].astype(BF16), btab_ref[2])
        o = acc_sc[0] / l_sc[0]
        lam = _diff_lambda(lp_ref, lam_init)
        for h in range(B_HEADS):
            r0 = h * hrows
            o_ref[0, :, h * B_VH:(h + 1) * B_VH] = _diff_finish(
                o[r0:r0 + T_PAD], o[r0 + T_PAD:r0 + hrows], lam, sg_ref[...], lam_init)


def _diff_sample(page_table, q_bd, k_new_t, v_new, btab, lam_p, subln, cache_kt, cache_v2, lam_init, pps):
    db, npages = page_table.shape
    rows = 2 * B_HEADS * T_PAD
    per_b = lambda shape: pl.BlockSpec((1,) + shape, lambda b, s, pt: (b,) + (0,) * len(shape))
    const = lambda shape: pl.BlockSpec(shape, lambda b, s, pt: (0,) * len(shape))
    grid_spec = pltpu.PrefetchScalarGridSpec(
        num_scalar_prefetch=1,
        grid=(db, npages // pps),
        in_specs=[per_b((rows, B_QK)), per_b((B_QK, TK)), per_b((TK, B_V)), const(btab.shape),
                  const(lam_p.shape), const(subln.shape)]
        + _page_specs(pps, B_QK, TK) + _page_specs(pps, TK * B_HEADS, B_VH),
        out_specs=per_b((T_PAD, B_V)),
        scratch_shapes=_flash_scratch(1, rows, B_VH),
    )
    return pl.pallas_call(
        functools.partial(_diff_sample_kernel, pps=pps, npages=npages, lam_init=lam_init),
        out_shape=jax.ShapeDtypeStruct((db, T_PAD, B_V), F32),
        grid_spec=grid_spec,
        compiler_params=_cparams(("parallel", "arbitrary")),
        name="diff_sample",
    )(page_table, q_bd, k_new_t, v_new, btab, lam_p, subln, *([cache_kt] * pps), *([cache_v2] * pps))


def _mid_kernel(x_ref, ma_ref, mb_ref, mk_ref, mv_ref, wout_ref, gx_ref, wq_ref, gm_ref, qg_ref, wo_ref, h2_ref):
    h = (x_ref[0] + _dot(ma_ref[0].astype(BF16), wout_ref[:A_Q, :])
         + _dot(mb_ref[0].astype(BF16), wout_ref[A_Q:, :]))
    hn = _rms(h) * gx_ref[...]
    q = _dot(hn.astype(BF16), wq_ref[...])
    ms = _dot((q * q).astype(BF16), gm_ref[...])
    q = (q * lax.rsqrt(ms + EPS) * qg_ref[...]).astype(BF16)
    mk = mk_ref[0].astype(BF16)
    mv = mv_ref[0].astype(BF16)
    outs = []
    for hh in range(MEM_HEADS):
        sl = slice(hh * MEM_HEAD_DIM, (hh + 1) * MEM_HEAD_DIM)
        s = _dot_nt(q[:, sl], mk[:, sl]) * (MEM_HEAD_DIM ** -0.5)
        p = jnp.exp(s - jnp.max(s, axis=-1, keepdims=True))
        l = jnp.sum(p, axis=-1, keepdims=True)
        outs.append(_dot(p.astype(BF16), mv[:, sl]) / l)
    o = jnp.concatenate(outs, axis=-1)
    h2_ref[0] = h + _dot(o.astype(BF16), wo_ref[...])


def _mid(x, mix_a, mix_b, mk, mv, w_out, g_x, w_q, gmat, q_gain, w_o, tm):
    b, s, d = x.shape
    m_tok = mk.shape[1]
    blk = lambda w: pl.BlockSpec((1, tm, w), lambda bi, i: (bi, i, 0))
    per_b = lambda w: pl.BlockSpec((1, m_tok, w), lambda bi, i: (bi, 0, 0))
    return pl.pallas_call(
        _mid_kernel,
        out_shape=jax.ShapeDtypeStruct((b, s, d), F32),
        grid=(b, s // tm),
        in_specs=[blk(d), blk(A_Q), blk(B_V), per_b(MEM_W), per_b(MEM_W), _const_spec(w_out.shape),
                  _const_spec(g_x.shape), _const_spec(w_q.shape), _const_spec(gmat.shape),
                  _const_spec(q_gain.shape), _const_spec(w_o.shape)],
        out_specs=blk(d),
        compiler_params=_cparams(("parallel", "arbitrary")),
        name="mid",
    )(x, mix_a, mix_b, mk, mv, w_out, g_x, w_q, gmat, q_gain, w_o)


def _ffn_core(h, gn, wg_ref, wu_ref, cw_ref, cb_ref, wd_ref, shifted):
    xb = (_rms(h) * gn).astype(BF16)
    g = _dot(xb, wg_ref[...])
    u = _dot(xb, wu_ref[...])
    gm1, gm2 = shifted(g)
    gc = cb_ref[...] + cw_ref[0:1, :] * gm2 + cw_ref[1:2, :] * gm1 + cw_ref[2:3, :] * g
    a = gc / (1.0 + jnp.exp(-gc)) * u
    return h + _dot(a.astype(BF16), wd_ref[...]), g


def _ffn_prompt_kernel(h_ref, gn_ref, wg_ref, wu_ref, cw_ref, cb_ref, wd_ref, y_ref, tail_ref, carry_ref):
    @pl.when(pl.program_id(1) == 0)
    def _():
        carry_ref[...] = jnp.zeros(carry_ref.shape, F32)

    tm = h_ref.shape[1]
    row = lax.broadcasted_iota(I32, (tm, 1), 0)
    c0 = carry_ref[SUBLANES - 2:SUBLANES - 1, :]
    c1 = carry_ref[SUBLANES - 1:SUBLANES, :]

    def shifted(g):
        gm1 = jnp.where(row == 0, c1, pltpu.roll(g, 1, 0))
        gm2 = jnp.where(row == 0, c0, jnp.where(row == 1, c1, pltpu.roll(g, 2, 0)))
        return gm1, gm2

    y, g = _ffn_core(h_ref[0], gn_ref[...], wg_ref, wu_ref, cw_ref, cb_ref, wd_ref, shifted)
    y_ref[0] = y
    tail = g[tm - SUBLANES:, :]
    carry_ref[...] = tail
    tail_ref[0] = tail


def _ffn_prompt(h, gn, wg, wu, cw, cb, wd, tm):
    b, s, d = h.shape
    f = wg.shape[1]
    blk = pl.BlockSpec((1, tm, d), lambda bi, i: (bi, i, 0))
    wspec = lambda shape: _const_spec(shape, single_buffer=True)
    return pl.pallas_call(
        _ffn_prompt_kernel,
        out_shape=[jax.ShapeDtypeStruct((b, s, d), F32), jax.ShapeDtypeStruct((b, SUBLANES, f), F32)],
        grid=(b, s // tm),
        in_specs=[blk, _const_spec(gn.shape), wspec(wg.shape), wspec(wu.shape), _const_spec(cw.shape),
                  _const_spec(cb.shape), wspec(wd.shape)],
        out_specs=[blk, pl.BlockSpec((1, SUBLANES, f), lambda bi, i: (bi, 0, 0))],
        scratch_shapes=[pltpu.VMEM((SUBLANES, f), F32)],
        compiler_params=_cparams(("arbitrary", "arbitrary")),
        name="ffn_prompt",
    )(h, gn, wg, wu, cw, cb, wd)


def _ffn_sample_kernel(h_ref, gn_ref, wg_ref, wu_ref, cw_ref, cb_ref, wd_ref, st1_ref, st2_ref, y_ref, g_ref):
    m = h_ref.shape[0]
    t = lax.broadcasted_iota(I32, (m, 1), 0) & (T_PAD - 1)

    def shifted(g):
        gm1 = jnp.where(t == 0, st1_ref[...], pltpu.roll(g, 1, 0))
        gm2 = jnp.where(t < 2, st2_ref[...], pltpu.roll(g, 2, 0))
        return gm1, gm2

    y, g = _ffn_core(h_ref[...], gn_ref[...], wg_ref, wu_ref, cw_ref, cb_ref, wd_ref, shifted)
    y_ref[...] = y
    g_ref[...] = g


def _ffn_sample(h2d, gn, wg, wu, cw, cb, wd, st1, st2):
    m, d = h2d.shape
    f = wg.shape[1]
    wspec = lambda shape: _const_spec(shape, single_buffer=True)
    return pl.pallas_call(
        _ffn_sample_kernel,
        out_shape=[jax.ShapeDtypeStruct((m, d), F32), jax.ShapeDtypeStruct((m, f), F32)],
        grid=(1,),
        in_specs=[_const_spec((m, d)), _const_spec(gn.shape), wspec(wg.shape), wspec(wu.shape),
                  _const_spec(cw.shape), _const_spec(cb.shape), wspec(wd.shape), _const_spec((m, f)),
                  _const_spec((m, f))],
        out_specs=[_const_spec((m, d)), _const_spec((m, f))],
        compiler_params=_cparams(("arbitrary",)),
        name="ffn_sample",
    )(h2d, gn, wg, wu, cw, cb, wd, st1, st2)


def _rel_bucket(dist):
    n = np.maximum(dist, 0)
    max_exact = NUM_BUCKETS // 2
    nf = np.maximum(n, 1).astype(np.float32)
    log_b = (np.log(nf / np.float32(max_exact)) / np.float32(math.log(MAX_DISTANCE / max_exact))
             * np.float32(NUM_BUCKETS - max_exact))
    large = np.minimum(max_exact + log_b.astype(np.int32), NUM_BUCKETS - 1)
    return np.where(n < max_exact, n, large)


def _bias_by_dist(dist, causal, bias):
    onehot = (_rel_bucket(dist)[..., None] == np.arange(NUM_BUCKETS)).astype(np.float32)
    vals = jnp.einsum("...k,kh->h...", onehot, bias, precision=lax.Precision.HIGHEST)
    return jnp.where(causal[None], vals, NEG).astype(F32)


def _prompt_bias_tables(bias):
    r = np.arange(TQ)[:, None]
    c = np.arange(TK)[None, :]
    always = np.ones((TQ, TK), bool)
    masked = _bias_by_dist(r - c, ~always, bias)
    t0 = _bias_by_dist(r - c, r >= c, bias)
    t1 = _bias_by_dist(r - c + TK, always, bias)
    t2 = _bias_by_dist(r - c + 2 * TK, always, bias)
    return jnp.swapaxes(jnp.stack([masked, t0, t1, t2], axis=1), -1, -2)


def _sample_bias_tables(bias, streams_per_head):
    t = np.arange(T_PAD)[:, None]
    c = np.arange(TK)[None, :]
    always = np.ones((T_PAD, TK), bool)
    far = _bias_by_dist(t - c + 2 * TK, always, bias)
    last = _bias_by_dist(t - c + TK, always, bias)
    new = _bias_by_dist(t - c, c <= t, bias)
    tabs = jnp.stack([far, last, new], axis=0)
    tabs = jnp.repeat(tabs[:, :, None], streams_per_head, axis=2)
    return tabs.reshape(3, -1, TK)


def _group_mean_matrix(width, group):
    idx = jnp.arange(width) // group
    return jnp.where(idx[:, None] == idx[None, :], 1.0 / group, 0.0).astype(BF16)


def _pad_rows(x, rows):
    return jnp.pad(x, ((0, 0), (0, rows - x.shape[1]), (0, 0)))


def _new_keys_t(x, rows):
    return jnp.swapaxes(_pad_rows(x, rows), 1, 2)


def kernel(x_prompt, x_sample, mem_prompt, cache_a_k, cache_a_v, cache_idx_k, cache_b_k, cache_b_v, cache_mem_k, cache_mem_v, state_ffn_conv, page_table, rel_bias, norm_mix, w_in, a_q_norm, a_k_norm, b_q_norm, b_k_norm, diff_lambda, diff_subln, w_out, norm_mem_x, norm_mem_src, w_mem_q, w_mem_kv, mem_q_norm, mem_k_norm, w_mem_o, norm_ffn, w_up, w_gate, ffn_conv_w, ffn_conv_b, w_down):
    depth = w_in.shape[0]
    assert depth == 1, "single-layer trunk"
    layer = 0
    lam_init = 0.8 - 0.6 * math.exp(-0.3 * layer)
    b, s, d = x_prompt.shape
    db, t_new, _ = x_sample.shape
    assert CONV_W - 1 <= t_new <= T_PAD and cache_a_k.shape[2] == TK
    m_tok = mem_prompt.shape[1]
    f = w_up.shape[-1]
    n_pool = cache_a_k.shape[1]
    tkb = min(TKB_MAX, s)
    assert s % tkb == 0 and tkb % TK == 0

    w_in_l = w_in[layer]
    n_front = A_Q + 2 * A_KV + IDX_Q + IDX_DIM + IDX_HEADS
    w_in_p = jnp.concatenate(
        [w_in_l[:, :n_front], jnp.zeros((d, C_QB - n_front), F32), w_in_l[:, n_front:]], axis=1).astype(BF16)
    ones = lambda n: jnp.ones((n,), F32)
    hgain = jnp.concatenate([
        jnp.tile(a_q_norm[layer], A_HEADS), jnp.tile(a_k_norm[layer], A_KV_HEADS), ones(C_QB - C_VA),
        jnp.tile(b_q_norm[layer], 2 * B_HEADS), jnp.tile(b_k_norm[layer], 2 * B_HEADS), ones(B_V)])[None, :]
    gmat64 = _group_mean_matrix(A_Q, A_HEAD_DIM)
    gmat128 = _group_mean_matrix(MEM_W, MEM_HEAD_DIM)
    proj_segs = ((C_QA, A_Q, True, A_Q), (C_KA, A_KV, True, A_KV), (C_VA, A_KV, False, A_KV),
                 (C_QI, IDX_Q, False, IDX_Q), (C_KIW, LANES, False, LANES), (C_KIW, LANES, False, IDX_DIM),
                 (C_QB, B_QK, True, B_QK), (C_KB, B_QK, True, B_QK), (C_VB, B_V, False, B_V))
    g_mix = norm_mix[layer][None, :]
    w_out_b = w_out[layer].astype(BF16)
    w_q_b = w_mem_q[layer].astype(BF16)
    w_o_b = w_mem_o[layer].astype(BF16)
    w_kv_b = w_mem_kv[layer].astype(BF16)
    w_gate_b = w_gate[layer].astype(BF16)
    w_up_b = w_up[layer].astype(BF16)
    w_down_b = w_down[layer].astype(BF16)
    g_memx = norm_mem_x[layer][None, :]
    q_gain = jnp.tile(mem_q_norm[layer], MEM_HEADS)[None, :]
    kv_gain = jnp.concatenate([jnp.tile(mem_k_norm[layer], MEM_HEADS), ones(MEM_W)])[None, :]
    g_ffn = norm_ffn[layer][None, :]
    conv_w = ffn_conv_w[layer]
    conv_b = ffn_conv_b[layer][None, :]
    lam_p = diff_lambda[layer]
    subln = diff_subln[layer][None, :]
    bias_a = rel_bias[:, :A_HEADS]
    bias_b = rel_bias[:, A_HEADS:]

    qa, ka, va, qi, kiw, ki, qb, kb, vb = _norm_proj(
        x_prompt.reshape(b * s, d), g_mix, w_in_p, gmat64, hgain, proj_segs, 256, "proj_prompt")
    r3 = lambda a: a.reshape(b, s, a.shape[-1])
    mix_a = _dsa_prompt(r3(qi), r3(kiw), r3(qa), r3(ka), r3(va), _prompt_bias_tables(bias_a), tkb)
    mix_b = _diff_prompt(r3(qb), r3(kb), r3(vb), _prompt_bias_tables(bias_b), lam_p, subln, lam_init, tkb)
    mk, mv = _norm_proj(mem_prompt.reshape(b * m_tok, d), norm_mem_src[layer][None, :], w_kv_b, gmat128, kv_gain,
                        ((0, MEM_W, True, MEM_W), (MEM_W, MEM_W, False, MEM_W)), 256, "mem_kv")
    h2 = _mid(x_prompt, mix_a, mix_b, mk.reshape(b, m_tok, MEM_W), mv.reshape(b, m_tok, MEM_W),
              w_out_b, g_memx, w_q_b, gmat128, q_gain, w_o_b, 256)
    yp, tail = _ffn_prompt(h2, g_ffn, w_gate_b, w_up_b, conv_w, conv_b, w_down_b, 256)
    conv_p = tail[:, SUBLANES - (CONV_W - 1):, :]

    xs = _pad_rows(x_sample, T_PAD)
    qa, ka_s, va_s, qi, kiw, ki_s, qb, kb_s, vb_s = _norm_proj(
        xs.reshape(db * T_PAD, d), g_mix, w_in_p, gmat64, hgain, proj_segs, db * T_PAD, "proj_sample")
    r3 = lambda a: a.reshape(db, T_PAD, a.shape[-1])
    idx_kt = jnp.transpose(cache_idx_k[layer], (0, 2, 1))
    a_kt = jnp.transpose(cache_a_k[layer], (0, 2, 3, 1)).reshape(n_pool, A_KV, TK)
    a_vt = jnp.transpose(cache_a_v[layer], (0, 2, 3, 1)).reshape(n_pool, A_KV, TK)
    b_kt = jnp.transpose(cache_b_k[layer], (0, 2, 3, 4, 1)).reshape(n_pool, B_QK, TK)
    b_v2 = cache_b_v[layer].reshape(n_pool, TK * B_HEADS, B_VH)
    selb = _dsa_sample_select(page_table, r3(qi), r3(kiw), _new_keys_t(r3(ki_s), TK), idx_kt, t_new, 8)
    q_rows = r3(qa).reshape(db, T_PAD, A_HEADS, A_HEAD_DIM).transpose(0, 2, 1, 3)
    eye_g = jnp.repeat(jnp.eye(A_KV_HEADS, dtype=F32), A_REP, axis=0)
    qa_bd = (q_rows[:, :, :, None, :] * eye_g[None, :, None, :, None]).reshape(db, A_HEADS * T_PAD, A_KV)
    mix_a = _dsa_sample_attn(page_table, qa_bd, _new_keys_t(r3(ka_s), TK), _new_keys_t(r3(va_s), TK), selb,
                             _sample_bias_tables(bias_a, 1), a_kt, a_vt, 8)
    n_str = 2 * B_HEADS
    q_rows = r3(qb).reshape(db, T_PAD, n_str, B_HEAD_DIM).transpose(0, 2, 1, 3)
    qb_bd = (q_rows[:, :, :, None, :] * jnp.eye(n_str, dtype=F32)[None, :, None, :, None]).reshape(
        db, n_str * T_PAD, B_QK)
    mix_b = _diff_sample(page_table, qb_bd, _new_keys_t(r3(kb_s), TK), _pad_rows(r3(vb_s), TK),
                         _sample_bias_tables(bias_b, 2), lam_p, subln, b_kt, b_v2, lam_init, 8)
    h2 = _mid(xs, mix_a, mix_b, cache_mem_k[layer].reshape(db, m_tok, MEM_W),
              cache_mem_v[layer].reshape(db, m_tok, MEM_W), w_out_b, g_memx, w_q_b, gmat128, q_gain, w_o_b, T_PAD)
    state = state_ffn_conv[layer]
    st1 = _pad_rows(state[:, 1:2, :], T_PAD).reshape(db * T_PAD, f)
    st2 = _pad_rows(state, T_PAD).reshape(db * T_PAD, f)
    ys, g_s = _ffn_sample(h2.reshape(db * T_PAD, d), g_ffn, w_gate_b, w_up_b, conv_w, conv_b, w_down_b, st1, st2)
    ys = ys.reshape(db, T_PAD, d)[:, :t_new]
    conv_s = g_s.reshape(db, T_PAD, f)[:, t_new - (CONV_W - 1):t_new]

    def new_rows(a, shape):
        return a.reshape(db, T_PAD, -1)[:, :t_new].reshape((1, db, t_new) + shape)

    return (yp, ys,
            ka.reshape(1, b, s, A_KV_HEADS, A_HEAD_DIM), va.reshape(1, b, s, A_KV_HEADS, A_HEAD_DIM),
            ki.reshape(1, b, s, IDX_DIM), kb.reshape(1, b, s, B_HEADS, 2, B_HEAD_DIM),
            vb.reshape(1, b, s, B_HEADS, 2 * B_HEAD_DIM),
            mk.reshape(1, b, m_tok, MEM_HEADS, MEM_HEAD_DIM), mv.reshape(1, b, m_tok, MEM_HEADS, MEM_HEAD_DIM),
            conv_p[None],
            new_rows(ka_s, (A_KV_HEADS, A_HEAD_DIM)), new_rows(va_s, (A_KV_HEADS, A_HEAD_DIM)),
            new_rows(ki_s, (IDX_DIM,)), new_rows(kb_s, (B_HEADS, 2, B_HEAD_DIM)),
            new_rows(vb_s, (B_HEADS, 2 * B_HEAD_DIM)), conv_s[None])
```

```python
import functools
import math

import jax
import jax.numpy as jnp
import numpy as np
from jax import lax
from jax.experimental import pallas as pl
from jax.experimental.pallas import tpu as pltpu

F32 = jnp.float32
BF16 = jnp.bfloat16
I32 = jnp.int32

EPS = 1e-6
NEG = -1e30
LOG2E = math.log2(math.e)
INT_MIN = -(2 ** 31)

A_HEADS = 8
A_KV_HEADS = 2
A_HEAD_DIM = 64
IDX_HEADS = 4
IDX_DIM = 64
TOPK_MAX = 256
B_HEADS = 4
B_HEAD_DIM = 64
MEM_HEADS = 4
MEM_HEAD_DIM = 128
CONV_W = 3
NUM_BUCKETS = 32
MAX_DISTANCE = 128

A_Q = A_HEADS * A_HEAD_DIM
A_KV = A_KV_HEADS * A_HEAD_DIM
IDX_Q = IDX_HEADS * IDX_DIM
B_QK = B_HEADS * 2 * B_HEAD_DIM
B_V = B_HEADS * 2 * B_HEAD_DIM
B_VH = 2 * B_HEAD_DIM
MEM_W = MEM_HEADS * MEM_HEAD_DIM
A_REP = A_HEADS // A_KV_HEADS

LANES = 128
SUBLANES = 8
TQ = 128
TK = 128
TKB_MAX = 512
T_PAD = SUBLANES
SELECT_PAGES_PER_STEP = 32
DIFF_PAGES_PER_STEP = 16
VMEM_LIMIT = 56 * 1024 * 1024

C_QA = 0
C_KA = C_QA + A_Q
C_VA = C_KA + A_KV
C_QI = C_VA + A_KV
C_KIW = C_QI + IDX_Q
C_QB = C_KIW + LANES
C_KB = C_QB + B_QK
C_VB = C_KB + B_QK
D_IN_PAD = C_VB + B_V


def _dot(a, b):
    return jnp.dot(a, b, preferred_element_type=F32)


def _dot_nt(a, b):
    return lax.dot_general(a, b, (((1,), (1,)), ((), ())), preferred_element_type=F32)


def _rms(x):
    return x * lax.rsqrt(jnp.mean(x * x, axis=-1, keepdims=True) + EPS)


KEY_NEG_INF = -(2 ** 31) + 0x7FFFFF


def _key_to_float(key):
    bits = key ^ ((key >> 31) & 0x7FFFFFFF)
    return jnp.where(key <= KEY_NEG_INF, -jnp.inf, lax.bitcast_convert_type(bits, F32))


def _kth_largest(count_ge, shape, topk):
    def bit_body(it, key):
        cand = key ^ lax.shift_left(jnp.int32(1), 31 - it)
        return jnp.where(count_ge(_key_to_float(cand)) >= topk, cand, key)

    return _key_to_float(lax.fori_loop(0, 32, bit_body, jnp.full(shape, INT_MIN, I32)))


def _cparams(sem, vmem=VMEM_LIMIT):
    return pltpu.CompilerParams(dimension_semantics=sem, vmem_limit_bytes=vmem)


def _const_spec(shape, single_buffer=False):
    nd = len(shape)
    if single_buffer:
        return pl.BlockSpec(shape, lambda *_: (0,) * nd, pipeline_mode=pl.Buffered(1))
    return pl.BlockSpec(shape, lambda *_: (0,) * nd)


def _norm_proj_kernel(x_ref, g_ref, w_ref, gm_ref, hg_ref, *out_refs, segs):
    xn = _rms(x_ref[...]) * g_ref[...]
    p = _dot(xn.astype(BF16), w_ref[...])
    for (start, width, norm, out_width), o_ref in zip(segs, out_refs):
        s = p[:, start:start + width]
        if norm:
            ms = _dot((s * s).astype(BF16), gm_ref[:width, :width])
            s = s * lax.rsqrt(ms + EPS) * hg_ref[:, start:start + width]
        o_ref[...] = s[:, :out_width]


def _norm_proj(x2d, gain, w, gmat, hgain, segs, tm, name):
    m, d = x2d.shape
    n = w.shape[1]
    return pl.pallas_call(
        functools.partial(_norm_proj_kernel, segs=segs),
        out_shape=[jax.ShapeDtypeStruct((m, ow), F32) for (_, _, _, ow) in segs],
        grid=(m // tm,),
        in_specs=[pl.BlockSpec((tm, d), lambda i: (i, 0)), _const_spec((1, d)), _const_spec((d, n)),
                  _const_spec(gmat.shape), _const_spec((1, n))],
        out_specs=[pl.BlockSpec((tm, ow), lambda i: (i, 0)) for (_, _, _, ow) in segs],
        compiler_params=_cparams(("parallel",)),
        name=name,
    )(x2d, gain, w, gmat, hgain)


def _init_flash(m_sc, l_sc, acc_sc):
    m_sc[...] = jnp.full(m_sc.shape, -jnp.inf, F32)
    l_sc[...] = jnp.zeros(l_sc.shape, F32)
    acc_sc[...] = jnp.zeros(acc_sc.shape, F32)


def _flash_update(st, s, pv_fn, m_sc, l_sc, acc_sc):
    m_old = m_sc[st]
    m_new = jnp.maximum(m_old, jnp.max(s, axis=-1, keepdims=True))
    alpha = jnp.exp(m_old - m_new)
    p = jnp.exp(s - m_new)
    l_sc[st] = alpha * l_sc[st] + jnp.sum(p, axis=-1, keepdims=True)
    acc_sc[st] = alpha * acc_sc[st] + pv_fn(p.astype(BF16))
    m_sc[st] = m_new


def _flash_scratch(streams, rows, dv):
    return [pltpu.VMEM((streams, rows, 1), F32), pltpu.VMEM((streams, rows, 1), F32),
            pltpu.VMEM((streams, rows, dv), F32)]


def _select_mask(sk, thr, need, before):
    tie_ok = jnp.where(sk == thr, jnp.where(before < need, 1.0, 0.0), 0.0)
    real = jnp.where(sk > -jnp.inf, 1.0, 0.0)
    sel = jnp.where(sk > thr, real, tie_ok * real)
    return jnp.where(sel > 0.5, 0.0, NEG)


def _chunk_bias(dtab_ref, h, i, c, ksub):
    return jnp.concatenate([dtab_ref[h, jnp.clip(i - (c * ksub + j) + 1, 0, 3)] for j in range(ksub)], axis=0)


def _flash_update_t(st, s, vt, m_sc, l_sc, acc_sc):
    m_old = m_sc[st]
    m_new = jnp.maximum(m_old, jnp.max(_col_reduce(s, jnp.max), axis=0, keepdims=True))
    alpha = jnp.exp2(m_old - m_new)
    p = jnp.exp2(s - m_new)
    l_sc[st] = alpha * l_sc[st] + jnp.sum(_col_reduce(p, jnp.sum), axis=0, keepdims=True)
    acc_sc[st] = alpha * acc_sc[st] + _dot(vt, p.astype(BF16))
    m_sc[st] = m_new


def _flash_scratch_t(streams, dv, cols):
    return [pltpu.VMEM((streams, 1, cols), F32), pltpu.VMEM((streams, 1, cols), F32),
            pltpu.VMEM((streams, dv, cols), F32)]


REDUCE_WAYS = 8


def _col_reduce(x, op):
    rows, cols = x.shape
    slabs = rows // SUBLANES
    if slabs % REDUCE_WAYS == 0 and slabs > REDUCE_WAYS:
        x = op(x.reshape(REDUCE_WAYS, slabs // REDUCE_WAYS, SUBLANES, cols), axis=1)
    else:
        x = x.reshape(slabs, SUBLANES, cols)
    return op(x, axis=0)


def _col_count(w):
    return _col_reduce(w, jnp.sum)


def _dsa_prompt_kernel(qi_ref, wit_ref, kiw_ref, qa_ref, ka_ref, vat_ref, dtab_ref, tril_ref, o_ref,
                       skey_ref, selb_ref, m_sc, l_sc, acc_sc, *, topk, tkb):
    i = pl.program_id(1)
    ksub = tkb // TK
    nbig = i // ksub + 1
    qi = qi_ref[0].astype(BF16)
    wit = wit_ref[0] * (IDX_HEADS ** -0.5)
    krow = lax.broadcasted_iota(I32, (tkb, TQ), 0)
    qcol = lax.broadcasted_iota(I32, (tkb, TQ), 1)

    def score_body(c, carry):
        off = pl.multiple_of(c * tkb, tkb)
        kc = kiw_ref[0, pl.ds(off, tkb), 0:IDX_DIM].astype(BF16)
        sc = jnp.zeros((tkb, TQ), F32)
        for h in range(IDX_HEADS):
            d = _dot_nt(kc, qi[:, h * IDX_DIM:(h + 1) * IDX_DIM]) * (IDX_DIM ** -0.5)
            sc = sc + jnp.maximum(d, 0.0) * wit[h:h + 1, :]
        causal = (c * tkb + krow) <= (i * TQ + qcol)
        skey_ref[c] = jnp.where(causal, sc, -jnp.inf)
        return carry

    lax.fori_loop(0, nbig, score_body, 0)

    def count(pred_fn):
        def body(c, cnt):
            return cnt + _col_count(jnp.where(pred_fn(skey_ref[c]), 1.0, 0.0))
        cnt = lax.fori_loop(0, nbig, body, jnp.zeros((SUBLANES, TQ), F32))
        return jnp.sum(cnt, axis=0, keepdims=True)

    thr = _kth_largest(lambda t: count(lambda sk: sk >= t), (1, TQ), topk)
    need = float(topk) - count(lambda sk: sk > thr)
    tril = tril_ref[...]

    def sel_body(c, off):
        sk = skey_ref[c]
        eqf = jnp.where(sk == thr, 1.0, 0.0)
        before = _dot(tril, eqf.astype(BF16)) + off
        selb_ref[c] = _select_mask(sk, thr, need, before)
        return off + jnp.sum(_col_count(eqf), axis=0, keepdims=True)

    lax.fori_loop(0, nbig, sel_body, jnp.zeros((1, TQ), F32))

    _init_flash(m_sc, l_sc, acc_sc)
    scale = A_HEAD_DIM ** -0.5 * LOG2E
    qg = [jnp.concatenate([qa_ref[0, :, h * A_HEAD_DIM:(h + 1) * A_HEAD_DIM] * scale
                           for h in range(g * A_REP, (g + 1) * A_REP)], axis=0).astype(BF16)
          for g in range(A_KV_HEADS)]

    def att_body(c, carry):
        off = pl.multiple_of(c * tkb, tkb)
        maskb = selb_ref[c]
        for g in range(A_KV_HEADS):
            g0 = g * A_HEAD_DIM
            kc = ka_ref[0, pl.ds(off, tkb), g0:g0 + A_HEAD_DIM].astype(BF16)
            vt = vat_ref[0, c, g0:g0 + A_HEAD_DIM, :].astype(BF16)
            sg = _dot_nt(kc, qg[g])
            for r in range(A_REP):
                h = g * A_REP + r
                s = sg[:, r * TQ:(r + 1) * TQ] + (_chunk_bias(dtab_ref, h, i, c, ksub) + maskb)
                _flash_update_t(h, s, vt, m_sc, l_sc, acc_sc)
        return carry

    lax.fori_loop(0, nbig, att_body, 0)
    for h in range(0, A_HEADS, 2):
        ot = jnp.concatenate([acc_sc[h] / l_sc[h], acc_sc[h + 1] / l_sc[h + 1]], axis=0)
        o_ref[0, :, h * A_HEAD_DIM:(h + 2) * A_HEAD_DIM] = ot.T


def _chunked_t(x, tkb):
    b, s, w = x.shape
    return x.reshape(b, s // tkb, tkb, w).transpose(0, 1, 3, 2)


def _dsa_prompt(qi, kiw, qa, ka, va, dtab, tkb):
    b, s, _ = qa.shape
    nq = s // TQ
    nc = s // tkb
    topk = min(TOPK_MAX, s // 4)
    tril = (jnp.arange(tkb)[None, :] < jnp.arange(tkb)[:, None]).astype(BF16)
    wit = jnp.swapaxes(kiw[:, :, IDX_DIM:IDX_DIM + SUBLANES], 1, 2)
    vat = _chunked_t(va, tkb)
    blk = lambda w: pl.BlockSpec((1, TQ, w), lambda bi, i: (bi, i, 0))
    full = lambda w: pl.BlockSpec((1, s, w), lambda bi, i: (bi, 0, 0))
    return pl.pallas_call(
        functools.partial(_dsa_prompt_kernel, topk=topk, tkb=tkb),
        out_shape=jax.ShapeDtypeStruct((b, s, A_Q), F32),
        grid=(b, nq),
        in_specs=[blk(IDX_Q), pl.BlockSpec((1, SUBLANES, TQ), lambda bi, i: (bi, 0, i)), full(LANES), blk(A_Q),
                  full(A_KV), pl.BlockSpec((1, nc, A_KV, tkb), lambda bi, i: (bi, 0, 0, 0)),
                  _const_spec(dtab.shape), _const_spec(tril.shape)],
        out_specs=blk(A_Q),
        scratch_shapes=[pltpu.VMEM((nc, tkb, TQ), F32), pltpu.VMEM((nc, tkb, TQ), F32)]
        + _flash_scratch_t(A_HEADS, A_HEAD_DIM, TQ),
        compiler_params=_cparams(("parallel", "arbitrary")),
        name="dsa_prompt",
    )(qi, wit, kiw, qa, ka, vat, dtab, tril)


def _diff_lambda(lp_ref, lam_init):
    lp = lp_ref[...]
    s1 = jnp.sum(lp[0:1] * lp[1:2], axis=-1, keepdims=True)
    s2 = jnp.sum(lp[2:3] * lp[3:4], axis=-1, keepdims=True)
    return jnp.exp(s1) - jnp.exp(s2) + lam_init


def _diff_finish(o0, o1, lam, sg, lam_init):
    o = o0 - lam * o1
    return _rms(o) * sg * (1.0 - lam_init)


def _diff_prompt_kernel(q_ref, k_ref, vt_ref, dtab_ref, lp_ref, sg_ref, o_ref, m_sc, l_sc, acc_sc, *, lam_init, tkb):
    i = pl.program_id(1)
    ksub = tkb // TK
    nbig = i // ksub + 1
    scale = B_HEAD_DIM ** -0.5 * LOG2E
    _init_flash(m_sc, l_sc, acc_sc)

    def body(c, carry):
        off = pl.multiple_of(c * tkb, tkb)
        for h in range(B_HEADS):
            bias = _chunk_bias(dtab_ref, h, i, c, ksub)
            vt = vt_ref[0, c, h * B_VH:(h + 1) * B_VH, :].astype(BF16)
            for comp in range(2):
                st = h * 2 + comp
                c0 = st * B_HEAD_DIM
                qh = (q_ref[0, :, c0:c0 + B_HEAD_DIM] * scale).astype(BF16)
                kc = k_ref[0, pl.ds(off, tkb), c0:c0 + B_HEAD_DIM].astype(BF16)
                s = _dot_nt(kc, qh) + bias
                _flash_update_t(st, s, vt, m_sc, l_sc, acc_sc)
        return carry

    lax.fori_loop(0, nbig, body, 0)
    lam = _diff_lambda(lp_ref, lam_init)
    for h in range(B_HEADS):
        o0 = (acc_sc[2 * h] / l_sc[2 * h]).T
        o1 = (acc_sc[2 * h + 1] / l_sc[2 * h + 1]).T
        o_ref[0, :, h * B_VH:(h + 1) * B_VH] = _diff_finish(o0, o1, lam, sg_ref[...], lam_init)


def _diff_prompt(qb, kb, vb, dtab, lam_p, subln, lam_init, tkb):
    b, s, _ = qb.shape
    nq = s // TQ
    nc = s // tkb
    blk = lambda w: pl.BlockSpec((1, TQ, w), lambda bi, i: (bi, i, 0))
    full = lambda w: pl.BlockSpec((1, s, w), lambda bi, i: (bi, 0, 0))
    return pl.pallas_call(
        functools.partial(_diff_prompt_kernel, lam_init=lam_init, tkb=tkb),
        out_shape=jax.ShapeDtypeStruct((b, s, B_V), F32),
        grid=(b, nq),
        in_specs=[blk(B_QK), full(B_QK), pl.BlockSpec((1, nc, B_V, tkb), lambda bi, i: (bi, 0, 0, 0)),
                  _const_spec(dtab.shape), _const_spec(lam_p.shape), _const_spec(subln.shape)],
        out_specs=blk(B_V),
        scratch_shapes=_flash_scratch_t(2 * B_HEADS, B_VH, TQ),
        compiler_params=_cparams(("parallel", "arbitrary")),
        name="diff_prompt",
    )(qb, kb, _chunked_t(vb, tkb), dtab, lam_p, subln)


def _page_specs(pps, rows, cols):
    return [pl.BlockSpec((1, rows, cols), functools.partial(lambda b, s, pt, j: (pt[b, s * pps + j], 0, 0), j=j))
            for j in range(pps)]


def _all_page_specs(npages, rows, cols):
    return [pl.BlockSpec((1, rows, cols), functools.partial(lambda b, pt, j: (pt[b, j], 0, 0), j=j))
            for j in range(npages)]


def _dsa_sample_select_kernel(pt_ref, q_ref, wi_ref, kinew_ref, *rest, pps, npages, topk, group):
    del pt_ref
    page_refs = rest[:pps]
    tri_ref, selb_ref, sc_ref = rest[pps:]
    b = pl.program_id(0)
    s = pl.program_id(1)
    np1 = npages + 1
    q = q_ref[0].astype(BF16)
    wi = wi_ref[0] * (IDX_HEADS ** -0.5)

    def scores(kt):
        d = _dot(q, kt) * (IDX_DIM ** -0.5)
        return jnp.sum((jnp.maximum(d, 0.0) * wi).reshape(IDX_HEADS, T_PAD, TK), axis=0)

    base = b * np1
    for j in range(pps):
        sc_ref[base + s * pps + j] = scores(page_refs[j][0].astype(BF16))

    last_page_step = s == pl.num_programs(1) - 1

    @pl.when(last_page_step)
    def _():
        row = lax.broadcasted_iota(I32, (T_PAD, TK), 0)
        col = lax.broadcasted_iota(I32, (T_PAD, TK), 1)
        sc_ref[base + npages] = jnp.where(col <= row, scores(kinew_ref[0].astype(BF16)), -jnp.inf)

    @pl.when(last_page_step & (b == pl.num_programs(0) - 1))
    def _():
        def group_body(gi, carry):
            sk = sc_ref[pl.ds(gi * (group * np1), group * np1)].reshape(group, np1, T_PAD, TK)

            def count(pred):
                cnt = jnp.sum(jnp.where(pred, 1.0, 0.0), axis=1)
                return jnp.sum(cnt, axis=-1, keepdims=True)

            thr = _kth_largest(lambda t: count(sk >= t[:, None]), (group, T_PAD, 1), topk)
            need = float(topk) - count(sk > thr[:, None])
            eqf = jnp.where(sk == thr[:, None], 1.0, 0.0)
            before = _dot(eqf.reshape(group * np1 * T_PAD, TK).astype(BF16), tri_ref[...]).reshape(sk.shape)
            ties = jnp.sum(eqf, axis=-1, keepdims=True)
            off = jnp.zeros((group, T_PAD, 1), F32)
            for c in range(np1):
                selb_ref[pl.ds(gi * group, group), c] = _select_mask(sk[:, c], thr, need, before[:, c] + off)
                off = off + ties[:, c]
            return carry

        lax.fori_loop(0, pl.num_programs(0) // group, group_body, 0)


def _dsa_sample_select(page_table, q_rows, wi_rows, ki_new_t, cache_ikt, n_new, pps):
    db, npages = page_table.shape
    np1 = npages + 1
    topk = min(TOPK_MAX, (npages * TK + n_new) // 4)
    group = math.gcd(db, SUBLANES)
    tri = (jnp.arange(TK)[:, None] < jnp.arange(TK)[None, :]).astype(BF16)
    per_b = lambda shape: pl.BlockSpec((1,) + shape, lambda b, s, pt: (b,) + (0,) * len(shape))
    grid_spec = pltpu.PrefetchScalarGridSpec(
        num_scalar_prefetch=1,
        grid=(db, npages // pps),
        in_specs=[per_b(q_rows.shape[1:]), per_b(wi_rows.shape[1:]), per_b((IDX_DIM, TK))]
        + _page_specs(pps, IDX_DIM, TK) + [pl.BlockSpec(tri.shape, lambda b, s, pt: (0, 0))],
        out_specs=pl.BlockSpec((db, np1, T_PAD, TK), lambda b, s, pt: (0, 0, 0, 0)),
        scratch_shapes=[pltpu.VMEM((db * np1, T_PAD, TK), F32)],
    )
    return pl.pallas_call(
        functools.partial(_dsa_sample_select_kernel, pps=pps, npages=npages, topk=topk, group=group),
        out_shape=jax.ShapeDtypeStruct((db, np1, T_PAD, TK), F32),
        grid_spec=grid_spec,
        compiler_params=_cparams(("arbitrary", "arbitrary")),
        name="dsa_sample_select",
    )(page_table, q_rows, wi_rows, ki_new_t, *([cache_ikt] * pps), tri)


def _page_bias(btab_ref, c0, n, npages):
    return jnp.concatenate([btab_ref[jnp.where(c0 + j == npages - 1, 1, 0)] for j in range(n)], axis=1)


def _dsa_sample_attn_kernel(pt_ref, q_ref, knew_ref, vnew_ref, selb_ref, btab_ref, *rest, npages):
    del pt_ref
    o_ref = rest[2 * npages]
    kts = [r[0] for r in rest[:npages]] + [knew_ref[0]]
    vts = [r[0] for r in rest[npages:2 * npages]] + [vnew_ref[0]]
    n = npages + 1
    rows = A_HEADS * T_PAD
    scale = A_HEAD_DIM ** -0.5
    q = (q_ref[0] * scale).astype(BF16)
    sc = jnp.concatenate([_dot(q, kt.astype(BF16)) for kt in kts], axis=1)
    bias = jnp.concatenate([btab_ref[0]] * (npages - 1) + [btab_ref[1], btab_ref[2]], axis=1)
    mask = jnp.concatenate([selb_ref[0, c] for c in range(n)], axis=1)
    sc = ((sc + bias).reshape(A_HEADS, T_PAD, n * TK) + mask[None]).reshape(rows, n * TK)
    p = jnp.exp(sc - jnp.max(sc, axis=-1, keepdims=True))
    l = jnp.sum(p, axis=-1, keepdims=True)
    p = p.astype(BF16)
    acc = _dot_nt(p[:, 0:TK], vts[0].astype(BF16))
    for j in range(1, n):
        acc = acc + _dot_nt(p[:, j * TK:(j + 1) * TK], vts[j].astype(BF16))
    o = acc / l
    for h in range(A_HEADS):
        g0 = (h // A_REP) * A_HEAD_DIM
        o_ref[0, :, h * A_HEAD_DIM:(h + 1) * A_HEAD_DIM] = o[h * T_PAD:(h + 1) * T_PAD, g0:g0 + A_HEAD_DIM]


def _dsa_sample_attn(page_table, q_bd, k_new_t, v_new_t, selb, btab, cache_kt, cache_vt):
    db, npages = page_table.shape
    rows = A_HEADS * T_PAD
    per_b = lambda shape: pl.BlockSpec((1,) + shape, lambda b, pt: (b,) + (0,) * len(shape))
    grid_spec = pltpu.PrefetchScalarGridSpec(
        num_scalar_prefetch=1,
        grid=(db,),
        in_specs=[per_b((rows, A_KV)), per_b((A_KV, TK)), per_b((A_KV, TK)), per_b((npages + 1, T_PAD, TK)),
                  pl.BlockSpec(btab.shape, lambda b, pt: (0, 0, 0))]
        + _all_page_specs(npages, A_KV, TK) + _all_page_specs(npages, A_KV, TK),
        out_specs=per_b((T_PAD, A_Q)),
    )
    return pl.pallas_call(
        functools.partial(_dsa_sample_attn_kernel, npages=npages),
        out_shape=jax.ShapeDtypeStruct((db, T_PAD, A_Q), F32),
        grid_spec=grid_spec,
        compiler_params=_cparams(("parallel",)),
        name="dsa_sample_attn",
    )(page_table, q_bd, k_new_t, v_new_t, selb, btab, *([cache_kt] * npages), *([cache_vt] * npages))


def _diff_sample_kernel(pt_ref, q_ref, knew_ref, vnew_ref, btab_ref, lp_ref, sg_ref, *rest, pps, npages, lam_init):
    del pt_ref
    k_refs = rest[:pps]
    v_refs = rest[pps:2 * pps]
    o_ref, m_sc, l_sc, acc_sc = rest[2 * pps:]
    s = pl.program_id(1)

    @pl.when(s == 0)
    def _():
        _init_flash(m_sc, l_sc, acc_sc)

    scale = B_HEAD_DIM ** -0.5
    q = (q_ref[0] * scale).astype(BF16)
    hrows = 2 * T_PAD

    def step(kt_list, v_fn, bias):
        n = len(kt_list)
        sc = jnp.concatenate([_dot(q, kt.astype(BF16)) for kt in kt_list], axis=1) + bias

        def pv(p):
            outs = []
            for h in range(B_HEADS):
                ph = p[h * hrows:(h + 1) * hrows]
                out = _dot(ph[:, 0:TK], v_fn(0, h))
                for j in range(1, n):
                    out = out + _dot(ph[:, j * TK:(j + 1) * TK], v_fn(j, h))
                outs.append(out)
            return jnp.concatenate(outs, axis=0)

        _flash_update(0, sc, pv, m_sc, l_sc, acc_sc)

    def page_v(j, h):
        return v_refs[j][0, pl.ds(h, TK, stride=B_HEADS), :].astype(BF16)

    step([r[0] for r in k_refs], page_v, _page_bias(btab_ref, s * pps, pps, npages))

    @pl.when(s == pl.num_programs(1) - 1)
    def _():
        step([knew_ref[0]], lambda j, h: vnew_ref[0, :, h * B_VH:(h + 1) * B_VH].astype(BF16), btab_ref[2])
        o = acc_sc[0] / l_sc[0]
        lam = _diff_lambda(lp_ref, lam_init)
        for h in range(B_HEADS):
            r0 = h * hrows
            o_ref[0, :, h * B_VH:(h + 1) * B_VH] = _diff_finish(
                o[r0:r0 + T_PAD], o[r0 + T_PAD:r0 + hrows], lam, sg_ref[...], lam_init)


def _diff_sample(page_table, q_bd, k_new_t, v_new, btab, lam_p, subln, cache_kt, cache_v2, lam_init, pps):
    db, npages = page_table.shape
    rows = 2 * B_HEADS * T_PAD
    per_b = lambda shape: pl.BlockSpec((1,) + shape, lambda b, s, pt: (b,) + (0,) * len(shape))
    const = lambda shape: pl.BlockSpec(shape, lambda b, s, pt: (0,) * len(shape))
    grid_spec = pltpu.PrefetchScalarGridSpec(
        num_scalar_prefetch=1,
        grid=(db, npages // pps),
        in_specs=[per_b((rows, B_QK)), per_b((B_QK, TK)), per_b((TK, B_V)), const(btab.shape),
                  const(lam_p.shape), const(subln.shape)]
        + _page_specs(pps, B_QK, TK) + _page_specs(pps, TK * B_HEADS, B_VH),
        out_specs=per_b((T_PAD, B_V)),
        scratch_shapes=_flash_scratch(1, rows, B_VH),
    )
    return pl.pallas_call(
        functools.partial(_diff_sample_kernel, pps=pps, npages=npages, lam_init=lam_init),
        out_shape=jax.ShapeDtypeStruct((db, T_PAD, B_V), F32),
        grid_spec=grid_spec,
        compiler_params=_cparams(("parallel", "arbitrary")),
        name="diff_sample",
    )(page_table, q_bd, k_new_t, v_new, btab, lam_p, subln, *([cache_kt] * pps), *([cache_v2] * pps))


def _mid_kernel(x_ref, ma_ref, mb_ref, mk_ref, mv_ref, wout_ref, gx_ref, wq_ref, gm_ref, qg_ref, wo_ref, h2_ref):
    h = (x_ref[0] + _dot(ma_ref[0].astype(BF16), wout_ref[:A_Q, :])
         + _dot(mb_ref[0].astype(BF16), wout_ref[A_Q:, :]))
    hn = _rms(h) * gx_ref[...]
    q = _dot(hn.astype(BF16), wq_ref[...])
    ms = _dot((q * q).astype(BF16), gm_ref[...])
    q = (q * lax.rsqrt(ms + EPS) * qg_ref[...]).astype(BF16)
    mk = mk_ref[0].astype(BF16)
    mv = mv_ref[0].astype(BF16)
    outs = []
    for hh in range(MEM_HEADS):
        sl = slice(hh * MEM_HEAD_DIM, (hh + 1) * MEM_HEAD_DIM)
        s = _dot_nt(q[:, sl], mk[:, sl]) * (MEM_HEAD_DIM ** -0.5)
        p = jnp.exp(s - jnp.max(s, axis=-1, keepdims=True))
        l = jnp.sum(p, axis=-1, keepdims=True)
        outs.append(_dot(p.astype(BF16), mv[:, sl]) / l)
    o = jnp.concatenate(outs, axis=-1)
    h2_ref[0] = h + _dot(o.astype(BF16), wo_ref[...])


def _mid(x, mix_a, mix_b, mk, mv, w_out, g_x, w_q, gmat, q_gain, w_o, tm):
    b, s, d = x.shape
    m_tok = mk.shape[1]
    blk = lambda w: pl.BlockSpec((1, tm, w), lambda bi, i: (bi, i, 0))
    per_b = lambda w: pl.BlockSpec((1, m_tok, w), lambda bi, i: (bi, 0, 0))
    return pl.pallas_call(
        _mid_kernel,
        out_shape=jax.ShapeDtypeStruct((b, s, d), F32),
        grid=(b, s // tm),
        in_specs=[blk(d), blk(A_Q), blk(B_V), per_b(MEM_W), per_b(MEM_W), _const_spec(w_out.shape),
                  _const_spec(g_x.shape), _const_spec(w_q.shape), _const_spec(gmat.shape),
                  _const_spec(q_gain.shape), _const_spec(w_o.shape)],
        out_specs=blk(d),
        compiler_params=_cparams(("parallel", "arbitrary")),
        name="mid",
    )(x, mix_a, mix_b, mk, mv, w_out, g_x, w_q, gmat, q_gain, w_o)


def _ffn_core(h, gn, wg_ref, wu_ref, cw_ref, cb_ref, wd_ref, shifted):
    xb = (_rms(h) * gn).astype(BF16)
    g = _dot(xb, wg_ref[...])
    u = _dot(xb, wu_ref[...])
    gm1, gm2 = shifted(g)
    gc = cb_ref[...] + cw_ref[0:1, :] * gm2 + cw_ref[1:2, :] * gm1 + cw_ref[2:3, :] * g
    a = gc / (1.0 + jnp.exp(-gc)) * u
    return h + _dot(a.astype(BF16), wd_ref[...]), g


def _ffn_prompt_kernel(h_ref, gn_ref, wg_ref, wu_ref, cw_ref, cb_ref, wd_ref, y_ref, tail_ref, carry_ref):
    @pl.when(pl.program_id(1) == 0)
    def _():
        carry_ref[...] = jnp.zeros(carry_ref.shape, F32)

    tm = h_ref.shape[1]
    row = lax.broadcasted_iota(I32, (tm, 1), 0)
    c0 = carry_ref[SUBLANES - 2:SUBLANES - 1, :]
    c1 = carry_ref[SUBLANES - 1:SUBLANES, :]

    def shifted(g):
        gm1 = jnp.where(row == 0, c1, pltpu.roll(g, 1, 0))
        gm2 = jnp.where(row == 0, c0, jnp.where(row == 1, c1, pltpu.roll(g, 2, 0)))
        return gm1, gm2

    y, g = _ffn_core(h_ref[0], gn_ref[...], wg_ref, wu_ref, cw_ref, cb_ref, wd_ref, shifted)
    y_ref[0] = y
    tail = g[tm - SUBLANES:, :]
    carry_ref[...] = tail
    tail_ref[0] = tail


def _ffn_prompt(h, gn, wg, wu, cw, cb, wd, tm):
    b, s, d = h.shape
    f = wg.shape[1]
    blk = pl.BlockSpec((1, tm, d), lambda bi, i: (bi, i, 0))
    wspec = lambda shape: _const_spec(shape, single_buffer=True)
    return pl.pallas_call(
        _ffn_prompt_kernel,
        out_shape=[jax.ShapeDtypeStruct((b, s, d), F32), jax.ShapeDtypeStruct((b, SUBLANES, f), F32)],
        grid=(b, s // tm),
        in_specs=[blk, _const_spec(gn.shape), wspec(wg.shape), wspec(wu.shape), _const_spec(cw.shape),
                  _const_spec(cb.shape), wspec(wd.shape)],
        out_specs=[blk, pl.BlockSpec((1, SUBLANES, f), lambda bi, i: (bi, 0, 0))],
        scratch_shapes=[pltpu.VMEM((SUBLANES, f), F32)],
        compiler_params=_cparams(("arbitrary", "arbitrary")),
        name="ffn_prompt",
    )(h, gn, wg, wu, cw, cb, wd)


def _ffn_sample_kernel(h_ref, gn_ref, wg_ref, wu_ref, cw_ref, cb_ref, wd_ref, st1_ref, st2_ref, y_ref, g_ref):
    m = h_ref.shape[0]
    t = lax.broadcasted_iota(I32, (m, 1), 0) & (T_PAD - 1)

    def shifted(g):
        gm1 = jnp.where(t == 0, st1_ref[...], pltpu.roll(g, 1, 0))
        gm2 = jnp.where(t < 2, st2_ref[...], pltpu.roll(g, 2, 0))
        return gm1, gm2

    y, g = _ffn_core(h_ref[...], gn_ref[...], wg_ref, wu_ref, cw_ref, cb_ref, wd_ref, shifted)
    y_ref[...] = y
    g_ref[...] = g


def _ffn_sample(h2d, gn, wg, wu, cw, cb, wd, st1, st2):
    m, d = h2d.shape
    f = wg.shape[1]
    wspec = lambda shape: _const_spec(shape, single_buffer=True)
    return pl.pallas_call(
        _ffn_sample_kernel,
        out_shape=[jax.ShapeDtypeStruct((m, d), F32), jax.ShapeDtypeStruct((m, f), F32)],
        grid=(1,),
        in_specs=[_const_spec((m, d)), _const_spec(gn.shape), wspec(wg.shape), wspec(wu.shape),
                  _const_spec(cw.shape), _const_spec(cb.shape), wspec(wd.shape), _const_spec((m, f)),
                  _const_spec((m, f))],
        out_specs=[_const_spec((m, d)), _const_spec((m, f))],
        compiler_params=_cparams(("arbitrary",)),
        name="ffn_sample",
    )(h2d, gn, wg, wu, cw, cb, wd, st1, st2)


def _rel_bucket(dist):
    n = np.maximum(dist, 0)
    max_exact = NUM_BUCKETS // 2
    nf = np.maximum(n, 1).astype(np.float32)
    log_b = (np.log(nf / np.float32(max_exact)) / np.float32(math.log(MAX_DISTANCE / max_exact))
             * np.float32(NUM_BUCKETS - max_exact))
    large = np.minimum(max_exact + log_b.astype(np.int32), NUM_BUCKETS - 1)
    return np.where(n < max_exact, n, large)


def _bias_by_dist(dist, causal, bias):
    onehot = (_rel_bucket(dist)[..., None] == np.arange(NUM_BUCKETS)).astype(np.float32)
    vals = jnp.einsum("...k,kh->h...", onehot, bias, precision=lax.Precision.HIGHEST)
    return jnp.where(causal[None], vals, NEG).astype(F32)


def _prompt_bias_tables(bias):
    r = np.arange(TQ)[:, None]
    c = np.arange(TK)[None, :]
    always = np.ones((TQ, TK), bool)
    masked = _bias_by_dist(r - c, ~always, bias)
    t0 = _bias_by_dist(r - c, r >= c, bias)
    t1 = _bias_by_dist(r - c + TK, always, bias)
    t2 = _bias_by_dist(r - c + 2 * TK, always, bias)
    return jnp.swapaxes(jnp.stack([masked, t0, t1, t2], axis=1), -1, -2) * LOG2E


def _sample_bias_tables(bias, streams_per_head):
    t = np.arange(T_PAD)[:, None]
    c = np.arange(TK)[None, :]
    always = np.ones((T_PAD, TK), bool)
    far = _bias_by_dist(t - c + 2 * TK, always, bias)
    last = _bias_by_dist(t - c + TK, always, bias)
    new = _bias_by_dist(t - c, c <= t, bias)
    tabs = jnp.stack([far, last, new], axis=0)
    tabs = jnp.repeat(tabs[:, :, None], streams_per_head, axis=2)
    return tabs.reshape(3, -1, TK)


def _group_mean_matrix(width, group):
    idx = jnp.arange(width) // group
    return jnp.where(idx[:, None] == idx[None, :], 1.0 / group, 0.0).astype(BF16)


def _pad_rows(x, rows):
    return jnp.pad(x, ((0, 0), (0, rows - x.shape[1]), (0, 0)))


def _new_keys_t(x, rows):
    return jnp.swapaxes(_pad_rows(x, rows), 1, 2)


def kernel(x_prompt, x_sample, mem_prompt, cache_a_k, cache_a_v, cache_idx_k, cache_b_k, cache_b_v, cache_mem_k, cache_mem_v, state_ffn_conv, page_table, rel_bias, norm_mix, w_in, a_q_norm, a_k_norm, b_q_norm, b_k_norm, diff_lambda, diff_subln, w_out, norm_mem_x, norm_mem_src, w_mem_q, w_mem_kv, mem_q_norm, mem_k_norm, w_mem_o, norm_ffn, w_up, w_gate, ffn_conv_w, ffn_conv_b, w_down):
    depth = w_in.shape[0]
    assert depth == 1, "single-layer trunk"
    layer = 0
    lam_init = 0.8 - 0.6 * math.exp(-0.3 * layer)
    b, s, d = x_prompt.shape
    db, t_new, _ = x_sample.shape
    assert CONV_W - 1 <= t_new <= T_PAD and cache_a_k.shape[2] == TK
    m_tok = mem_prompt.shape[1]
    f = w_up.shape[-1]
    n_pool = cache_a_k.shape[1]
    tkb = min(TKB_MAX, s)
    assert s % tkb == 0 and tkb % TK == 0

    w_in_l = w_in[layer]
    n_front = A_Q + 2 * A_KV + IDX_Q + IDX_DIM + IDX_HEADS
    w_in_p = jnp.concatenate(
        [w_in_l[:, :n_front], jnp.zeros((d, C_QB - n_front), F32), w_in_l[:, n_front:]], axis=1).astype(BF16)
    ones = lambda n: jnp.ones((n,), F32)
    hgain = jnp.concatenate([
        jnp.tile(a_q_norm[layer], A_HEADS), jnp.tile(a_k_norm[layer], A_KV_HEADS), ones(C_QB - C_VA),
        jnp.tile(b_q_norm[layer], 2 * B_HEADS), jnp.tile(b_k_norm[layer], 2 * B_HEADS), ones(B_V)])[None, :]
    gmat64 = _group_mean_matrix(A_Q, A_HEAD_DIM)
    gmat128 = _group_mean_matrix(MEM_W, MEM_HEAD_DIM)
    proj_segs = ((C_QA, A_Q, True, A_Q), (C_KA, A_KV, True, A_KV), (C_VA, A_KV, False, A_KV),
                 (C_QI, IDX_Q, False, IDX_Q), (C_KIW, LANES, False, LANES), (C_KIW, LANES, False, IDX_DIM),
                 (C_QB, B_QK, True, B_QK), (C_KB, B_QK, True, B_QK), (C_VB, B_V, False, B_V))
    g_mix = norm_mix[layer][None, :]
    w_out_b = w_out[layer].astype(BF16)
    w_q_b = w_mem_q[layer].astype(BF16)
    w_o_b = w_mem_o[layer].astype(BF16)
    w_kv_b = w_mem_kv[layer].astype(BF16)
    w_gate_b = w_gate[layer].astype(BF16)
    w_up_b = w_up[layer].astype(BF16)
    w_down_b = w_down[layer].astype(BF16)
    g_memx = norm_mem_x[layer][None, :]
    q_gain = jnp.tile(mem_q_norm[layer], MEM_HEADS)[None, :]
    kv_gain = jnp.concatenate([jnp.tile(mem_k_norm[layer], MEM_HEADS), ones(MEM_W)])[None, :]
    g_ffn = norm_ffn[layer][None, :]
    conv_w = ffn_conv_w[layer]
    conv_b = ffn_conv_b[layer][None, :]
    lam_p = diff_lambda[layer]
    subln = diff_subln[layer][None, :]
    bias_a = rel_bias[:, :A_HEADS]
    bias_b = rel_bias[:, A_HEADS:]

    qa, ka, va, qi, kiw, ki, qb, kb, vb = _norm_proj(
        x_prompt.reshape(b * s, d), g_mix, w_in_p, gmat64, hgain, proj_segs, 256, "proj_prompt")
    r3 = lambda a: a.reshape(b, s, a.shape[-1])
    mix_a = _dsa_prompt(r3(qi), r3(kiw), r3(qa), r3(ka), r3(va), _prompt_bias_tables(bias_a), tkb)
    mix_b = _diff_prompt(r3(qb), r3(kb), r3(vb), _prompt_bias_tables(bias_b), lam_p, subln, lam_init, tkb)
    mk, mv = _norm_proj(mem_prompt.reshape(b * m_tok, d), norm_mem_src[layer][None, :], w_kv_b, gmat128, kv_gain,
                        ((0, MEM_W, True, MEM_W), (MEM_W, MEM_W, False, MEM_W)), 256, "mem_kv")
    h2 = _mid(x_prompt, mix_a, mix_b, mk.reshape(b, m_tok, MEM_W), mv.reshape(b, m_tok, MEM_W),
              w_out_b, g_memx, w_q_b, gmat128, q_gain, w_o_b, 256)
    yp, tail = _ffn_prompt(h2, g_ffn, w_gate_b, w_up_b, conv_w, conv_b, w_down_b, 256)
    conv_p = tail[:, SUBLANES - (CONV_W - 1):, :]

    xs = _pad_rows(x_sample, T_PAD)
    qa, ka_s, va_s, qi, kiw, ki_s, qb, kb_s, vb_s = _norm_proj(
        xs.reshape(db * T_PAD, d), g_mix, w_in_p, gmat64, hgain, proj_segs, db * T_PAD, "proj_sample")
    r3 = lambda a: a.reshape(db, T_PAD, a.shape[-1])
    idx_kt = jnp.transpose(cache_idx_k[layer], (0, 2, 1))
    a_kt = jnp.transpose(cache_a_k[layer], (0, 2, 3, 1)).reshape(n_pool, A_KV, TK)
    a_vt = jnp.transpose(cache_a_v[layer], (0, 2, 3, 1)).reshape(n_pool, A_KV, TK)
    b_kt = jnp.transpose(cache_b_k[layer], (0, 2, 3, 4, 1)).reshape(n_pool, B_QK, TK)
    b_v2 = cache_b_v[layer].reshape(n_pool, TK * B_HEADS, B_VH)
    n_pages = page_table.shape[1]
    qi_rows = r3(qi).reshape(db, T_PAD, IDX_HEADS, IDX_DIM).transpose(0, 2, 1, 3).reshape(db, IDX_HEADS * T_PAD, IDX_DIM)
    wi_rows = r3(kiw)[:, :, IDX_DIM:IDX_DIM + IDX_HEADS].transpose(0, 2, 1).reshape(db, IDX_HEADS * T_PAD, 1)
    selb = _dsa_sample_select(page_table, qi_rows, wi_rows, _new_keys_t(r3(ki_s), TK), idx_kt, t_new,
                              math.gcd(n_pages, SELECT_PAGES_PER_STEP))
    q_rows = r3(qa).reshape(db, T_PAD, A_HEADS, A_HEAD_DIM).transpose(0, 2, 1, 3)
    eye_g = jnp.repeat(jnp.eye(A_KV_HEADS, dtype=F32), A_REP, axis=0)
    qa_bd = (q_rows[:, :, :, None, :] * eye_g[None, :, None, :, None]).reshape(db, A_HEADS * T_PAD, A_KV)
    mix_a = _dsa_sample_attn(page_table, qa_bd, _new_keys_t(r3(ka_s), TK), _new_keys_t(r3(va_s), TK), selb,
                             _sample_bias_tables(bias_a, 1), a_kt, a_vt)
    n_str = 2 * B_HEADS
    q_rows = r3(qb).reshape(db, T_PAD, n_str, B_HEAD_DIM).transpose(0, 2, 1, 3)
    qb_bd = (q_rows[:, :, :, None, :] * jnp.eye(n_str, dtype=F32)[None, :, None, :, None]).reshape(
        db, n_str * T_PAD, B_QK)
    mix_b = _diff_sample(page_table, qb_bd, _new_keys_t(r3(kb_s), TK), _pad_rows(r3(vb_s), TK),
                         _sample_bias_tables(bias_b, 2), lam_p, subln, b_kt, b_v2, lam_init,
                         math.gcd(n_pages, DIFF_PAGES_PER_STEP))
    h2 = _mid(xs, mix_a, mix_b, cache_mem_k[layer].reshape(db, m_tok, MEM_W),
              cache_mem_v[layer].reshape(db, m_tok, MEM_W), w_out_b, g_memx, w_q_b, gmat128, q_gain, w_o_b, T_PAD)
    state = state_ffn_conv[layer]
    st1 = _pad_rows(state[:, 1:2, :], T_PAD).reshape(db * T_PAD, f)
    st2 = _pad_rows(state, T_PAD).reshape(db * T_PAD, f)
    ys, g_s = _ffn_sample(h2.reshape(db * T_PAD, d), g_ffn, w_gate_b, w_up_b, conv_w, conv_b, w_down_b, st1, st2)
    ys = ys.reshape(db, T_PAD, d)[:, :t_new]
    conv_s = g_s.reshape(db, T_PAD, f)[:, t_new - (CONV_W - 1):t_new]

    def new_rows(a, shape):
        return a.reshape(db, T_PAD, -1)[:, :t_new].reshape((1, db, t_new) + shape)

    return (yp, ys,
            ka.reshape(1, b, s, A_KV_HEADS, A_HEAD_DIM), va.reshape(1, b, s, A_KV_HEADS, A_HEAD_DIM),
            ki.reshape(1, b, s, IDX_DIM), kb.reshape(1, b, s, B_HEADS, 2, B_HEAD_DIM),
            vb.reshape(1, b, s, B_HEADS, 2 * B_HEAD_DIM),
            mk.reshape(1, b, m_tok, MEM_HEADS, MEM_HEAD_DIM), mv.reshape(1, b, m_tok, MEM_HEADS, MEM_HEAD_DIM),
            conv_p[None],
            new_rows(ka_s, (A_KV_HEADS, A_HEAD_DIM)), new_rows(va_s, (A_KV_HEADS, A_HEAD_DIM)),
            new_rows(ki_s, (IDX_DIM,)), new_rows(kb_s, (B_HEADS, 2, B_HEAD_DIM)),
            new_rows(vb_s, (B_HEADS, 2 * B_HEAD_DIM)), conv_s[None])
```

```python
import functools
import math

import jax
import jax.numpy as jnp
import numpy as np
from jax import lax
from jax.experimental import pallas as pl
from jax.experimental.pallas import tpu as pltpu

F32 = jnp.float32
BF16 = jnp.bfloat16
I32 = jnp.int32

EPS = 1e-6
NEG = -1e30
LOG2E = math.log2(math.e)
INT_MIN = -(2 ** 31)

A_HEADS = 8
A_KV_HEADS = 2
A_HEAD_DIM = 64
IDX_HEADS = 4
IDX_DIM = 64
TOPK_MAX = 256
B_HEADS = 4
B_HEAD_DIM = 64
MEM_HEADS = 4
MEM_HEAD_DIM = 128
CONV_W = 3
NUM_BUCKETS = 32
MAX_DISTANCE = 128

A_Q = A_HEADS * A_HEAD_DIM
A_KV = A_KV_HEADS * A_HEAD_DIM
IDX_Q = IDX_HEADS * IDX_DIM
B_QK = B_HEADS * 2 * B_HEAD_DIM
B_V = B_HEADS * 2 * B_HEAD_DIM
B_VH = 2 * B_HEAD_DIM
MEM_W = MEM_HEADS * MEM_HEAD_DIM
A_REP = A_HEADS // A_KV_HEADS

LANES = 128
SUBLANES = 8
TQ = 128
TK = 128
TKB_MAX = 512
T_PAD = SUBLANES
TM_TOKENS = 512
FEATURE_MAJOR = 0
CHUNKED_T = -1
SELECT_PAGES_PER_STEP = 32
DIFF_PAGES_PER_STEP = 16
VMEM_LIMIT = 56 * 1024 * 1024

C_QA = 0
C_KA = C_QA + A_Q
C_VA = C_KA + A_KV
C_QI = C_VA + A_KV
C_KIW = C_QI + IDX_Q
C_QB = C_KIW + LANES
C_KB = C_QB + B_QK
C_VB = C_KB + B_QK
D_IN_PAD = C_VB + B_V


def _dot(a, b):
    return jnp.dot(a, b, preferred_element_type=F32)


def _dot_nt(a, b):
    return lax.dot_general(a, b, (((1,), (1,)), ((), ())), preferred_element_type=F32)


def _rms(x):
    return x * lax.rsqrt(jnp.mean(x * x, axis=-1, keepdims=True) + EPS)


KEY_NEG_INF = -(2 ** 31) + 0x7FFFFF


def _key_to_float(key):
    bits = key ^ ((key >> 31) & 0x7FFFFFFF)
    return jnp.where(key <= KEY_NEG_INF, -jnp.inf, lax.bitcast_convert_type(bits, F32))


def _kth_largest(count_ge, shape, topk):
    def bit_body(it, key):
        cand = key ^ lax.shift_left(jnp.int32(1), 31 - it)
        return jnp.where(count_ge(_key_to_float(cand)) >= topk, cand, key)

    return _key_to_float(lax.fori_loop(0, 32, bit_body, jnp.full(shape, INT_MIN, I32)))


def _cparams(sem, vmem=VMEM_LIMIT):
    return pltpu.CompilerParams(dimension_semantics=sem, vmem_limit_bytes=vmem)


def _const_spec(shape, single_buffer=False):
    nd = len(shape)
    if single_buffer:
        return pl.BlockSpec(shape, lambda *_: (0,) * nd, pipeline_mode=pl.Buffered(1))
    return pl.BlockSpec(shape, lambda *_: (0,) * nd)


def _norm_proj_kernel(x_ref, g_ref, w_ref, gm_ref, hg_ref, *out_refs, segs):
    xn = _rms(x_ref[...]) * g_ref[...]
    p = _dot(xn.astype(BF16), w_ref[...])
    tm = x_ref.shape[0]
    done = {}
    for (start, width, norm, out_width, split), o_ref in zip(segs, out_refs):
        if (start, width, norm) not in done:
            s = p[:, start:start + width]
            if norm:
                ms = _dot((s * s).astype(BF16), gm_ref[:width, :width])
                s = s * lax.rsqrt(ms + EPS) * hg_ref[:, start:start + width]
            done[(start, width, norm)] = s
        s = done[(start, width, norm)]
        if split == 1:
            o_ref[...] = s[:, :out_width]
        elif split == FEATURE_MAJOR:
            o_ref[0] = s.T[:out_width, :]
        elif split == CHUNKED_T:
            o_ref[0, 0] = s.T[:out_width, :]
        else:
            pw = out_width // split
            for j in range(split):
                o_ref[pl.ds(j, tm, stride=split), :] = s[:, j * pw:(j + 1) * pw]


def _norm_proj(x2d, gain, w, gmat, hgain, segs, tm, name, seq=None):
    m, d = x2d.shape
    n = w.shape[1]
    tiles = None if seq is None else seq // tm

    def shape_spec(ow, sp):
        if sp == FEATURE_MAJOR:
            return (jax.ShapeDtypeStruct((m // seq, ow, seq), F32),
                    pl.BlockSpec((1, ow, tm), lambda i: (i // tiles, 0, i % tiles)))
        if sp == CHUNKED_T:
            return (jax.ShapeDtypeStruct((m // seq, tiles, ow, tm), F32),
                    pl.BlockSpec((1, 1, ow, tm), lambda i: (i // tiles, i % tiles, 0, 0)))
        return (jax.ShapeDtypeStruct((m * sp, ow // sp), F32), pl.BlockSpec((tm * sp, ow // sp), lambda i: (i, 0)))

    shapes, specs = zip(*[shape_spec(ow, sp) for (_, _, _, ow, sp) in segs])
    return pl.pallas_call(
        functools.partial(_norm_proj_kernel, segs=segs),
        out_shape=list(shapes),
        grid=(m // tm,),
        in_specs=[pl.BlockSpec((tm, d), lambda i: (i, 0)), _const_spec((1, d)), _const_spec((d, n)),
                  _const_spec(gmat.shape), _const_spec((1, n))],
        out_specs=list(specs),
        compiler_params=_cparams(("parallel",)),
        name=name,
    )(x2d, gain, w, gmat, hgain)


def _init_flash(m_sc, l_sc, acc_sc):
    m_sc[...] = jnp.full(m_sc.shape, -jnp.inf, F32)
    l_sc[...] = jnp.zeros(l_sc.shape, F32)
    acc_sc[...] = jnp.zeros(acc_sc.shape, F32)


def _flash_update(st, s, pv_fn, m_sc, l_sc, acc_sc):
    m_old = m_sc[st]
    m_new = jnp.maximum(m_old, jnp.max(s, axis=-1, keepdims=True))
    alpha = jnp.exp(m_old - m_new)
    p = jnp.exp(s - m_new)
    l_sc[st] = alpha * l_sc[st] + jnp.sum(p, axis=-1, keepdims=True)
    acc_sc[st] = alpha * acc_sc[st] + pv_fn(p.astype(BF16))
    m_sc[st] = m_new


def _flash_scratch(streams, rows, dv):
    return [pltpu.VMEM((streams, rows, 1), F32), pltpu.VMEM((streams, rows, 1), F32),
            pltpu.VMEM((streams, rows, dv), F32)]


def _select_mask(sk, thr, need, before):
    tie_ok = jnp.where(sk == thr, jnp.where(before < need, 1.0, 0.0), 0.0)
    real = jnp.where(sk > -jnp.inf, 1.0, 0.0)
    sel = jnp.where(sk > thr, real, tie_ok * real)
    return jnp.where(sel > 0.5, 0.0, NEG)


def _chunk_bias(dtab_ref, h, i, c, ksub):
    return jnp.concatenate([dtab_ref[h, jnp.clip(i - (c * ksub + j) + 1, 0, 3)] for j in range(ksub)], axis=0)


def _flash_update_t(sts, s_list, vt, m_sc, l_sc, acc_sc):
    for st, s in zip(sts, s_list):
        m_old = m_sc[st]
        m_new = jnp.maximum(m_old, jnp.max(_col_reduce(s, jnp.max), axis=0, keepdims=True))
        alpha = jnp.exp2(m_old - m_new)
        p = jnp.exp2(s - m_new)
        l_sc[st] = alpha * l_sc[st] + jnp.sum(_col_reduce(p, jnp.sum), axis=0, keepdims=True)
        acc_sc[st] = alpha * acc_sc[st] + _dot(vt, p.astype(BF16))
        m_sc[st] = m_new


def _flash_scratch_t(streams, dv, cols):
    return [pltpu.VMEM((streams, 1, cols), F32), pltpu.VMEM((streams, 1, cols), F32),
            pltpu.VMEM((streams, dv, cols), F32)]


REDUCE_WAYS = 8


def _col_reduce(x, op):
    rows, cols = x.shape
    slabs = rows // SUBLANES
    if slabs % REDUCE_WAYS == 0 and slabs > REDUCE_WAYS:
        x = op(x.reshape(REDUCE_WAYS, slabs // REDUCE_WAYS, SUBLANES, cols), axis=1)
    else:
        x = x.reshape(slabs, SUBLANES, cols)
    return op(x, axis=0)


def _col_count(w):
    return _col_reduce(w, jnp.sum)


def _dsa_prompt_kernel(qi_ref, wit_ref, kiw_ref, qa_ref, ka_ref, vat_ref, dtab_ref, tril_ref, o_ref,
                       skey_ref, selb_ref, m_sc, l_sc, acc_sc, *, topk, tkb):
    i = pl.program_id(1)
    ksub = tkb // TK
    nbig = i // ksub + 1
    qi = qi_ref[0].astype(BF16)
    wit = wit_ref[0] * (IDX_HEADS ** -0.5)
    krow = lax.broadcasted_iota(I32, (tkb, TQ), 0)
    qcol = lax.broadcasted_iota(I32, (tkb, TQ), 1)

    def score_body(c, carry):
        off = pl.multiple_of(c * tkb, tkb)
        kc = kiw_ref[0, pl.ds(off, tkb), 0:IDX_DIM].astype(BF16)
        sc = jnp.zeros((tkb, TQ), F32)
        for h in range(IDX_HEADS):
            d = _dot_nt(kc, qi[:, h * IDX_DIM:(h + 1) * IDX_DIM]) * (IDX_DIM ** -0.5)
            sc = sc + jnp.maximum(d, 0.0) * wit[h:h + 1, :]
        causal = (c * tkb + krow) <= (i * TQ + qcol)
        skey_ref[c] = jnp.where(causal, sc, -jnp.inf)
        return carry

    lax.fori_loop(0, nbig, score_body, 0)

    def count(pred_fn):
        def body(c, cnt):
            return cnt + _col_count(jnp.where(pred_fn(skey_ref[c]), 1.0, 0.0))
        cnt = lax.fori_loop(0, nbig, body, jnp.zeros((SUBLANES, TQ), F32))
        return jnp.sum(cnt, axis=0, keepdims=True)

    thr = _kth_largest(lambda t: count(lambda sk: sk >= t), (1, TQ), topk)
    need = float(topk) - count(lambda sk: sk > thr)
    tril = tril_ref[...]

    def sel_body(c, off):
        sk = skey_ref[c]
        eqf = jnp.where(sk == thr, 1.0, 0.0)
        before = _dot(tril, eqf.astype(BF16)) + off
        selb_ref[c] = _select_mask(sk, thr, need, before)
        return off + jnp.sum(_col_count(eqf), axis=0, keepdims=True)

    lax.fori_loop(0, nbig, sel_body, jnp.zeros((1, TQ), F32))

    _init_flash(m_sc, l_sc, acc_sc)
    scale = A_HEAD_DIM ** -0.5 * LOG2E
    qg = [jnp.concatenate([qa_ref[0, :, h * A_HEAD_DIM:(h + 1) * A_HEAD_DIM] * scale
                           for h in range(g * A_REP, (g + 1) * A_REP)], axis=0).astype(BF16)
          for g in range(A_KV_HEADS)]

    def att_body(c, carry):
        off = pl.multiple_of(c * tkb, tkb)
        maskb = selb_ref[c]
        for g in range(A_KV_HEADS):
            g0 = g * A_HEAD_DIM
            kc = ka_ref[0, pl.ds(off, tkb), g0:g0 + A_HEAD_DIM].astype(BF16)
            vt = vat_ref[0, c, g0:g0 + A_HEAD_DIM, :].astype(BF16)
            sg = _dot_nt(kc, qg[g])
            heads = range(g * A_REP, (g + 1) * A_REP)
            s_list = [sg[:, r * TQ:(r + 1) * TQ] + (_chunk_bias(dtab_ref, h, i, c, ksub) + maskb)
                      for r, h in enumerate(heads)]
            _flash_update_t(list(heads), s_list, vt, m_sc, l_sc, acc_sc)
        return carry

    lax.fori_loop(0, nbig, att_body, 0)
    for h in range(0, A_HEADS, 2):
        ot = jnp.concatenate([acc_sc[h] / l_sc[h], acc_sc[h + 1] / l_sc[h + 1]], axis=0)
        o_ref[0, :, h * A_HEAD_DIM:(h + 2) * A_HEAD_DIM] = ot.T


def _dsa_prompt(qi, kiw, qa, ka, vat, dtab, tkb):
    b, s, _ = qa.shape
    nq = s // TQ
    nc = s // tkb
    topk = min(TOPK_MAX, s // 4)
    tril = (jnp.arange(tkb)[None, :] < jnp.arange(tkb)[:, None]).astype(BF16)
    wit = jnp.swapaxes(kiw[:, :, IDX_DIM:IDX_DIM + SUBLANES], 1, 2)
    blk = lambda w: pl.BlockSpec((1, TQ, w), lambda bi, i: (bi, i, 0))
    full = lambda w: pl.BlockSpec((1, s, w), lambda bi, i: (bi, 0, 0))
    return pl.pallas_call(
        functools.partial(_dsa_prompt_kernel, topk=topk, tkb=tkb),
        out_shape=jax.ShapeDtypeStruct((b, s, A_Q), F32),
        grid=(b, nq),
        in_specs=[blk(IDX_Q), pl.BlockSpec((1, SUBLANES, TQ), lambda bi, i: (bi, 0, i)), full(LANES), blk(A_Q),
                  full(A_KV), pl.BlockSpec((1, nc, A_KV, tkb), lambda bi, i: (bi, 0, 0, 0)),
                  _const_spec(dtab.shape), _const_spec(tril.shape)],
        out_specs=blk(A_Q),
        scratch_shapes=[pltpu.VMEM((nc, tkb, TQ), F32), pltpu.VMEM((nc, tkb, TQ), F32)]
        + _flash_scratch_t(A_HEADS, A_HEAD_DIM, TQ),
        compiler_params=_cparams(("parallel", "arbitrary")),
        name="dsa_prompt",
    )(qi, wit, kiw, qa, ka, vat, dtab, tril)


def _diff_lambda(lp_ref, lam_init):
    lp = lp_ref[...]
    s1 = jnp.sum(lp[0:1] * lp[1:2], axis=-1, keepdims=True)
    s2 = jnp.sum(lp[2:3] * lp[3:4], axis=-1, keepdims=True)
    return jnp.exp(s1) - jnp.exp(s2) + lam_init


def _diff_finish(o0, o1, lam, sg, lam_init):
    o = o0 - lam * o1
    return _rms(o) * sg * (1.0 - lam_init)


def _diff_prompt_kernel(q_ref, k_ref, vt_ref, dtab_ref, lp_ref, sg_ref, o_ref, m_sc, l_sc, acc_sc, *, lam_init, tkb):
    i = pl.program_id(1)
    ksub = tkb // TK
    nbig = i // ksub + 1
    scale = B_HEAD_DIM ** -0.5 * LOG2E
    _init_flash(m_sc, l_sc, acc_sc)

    def body(c, carry):
        off = pl.multiple_of(c * tkb, tkb)
        for h in range(B_HEADS):
            bias = _chunk_bias(dtab_ref, h, i, c, ksub)
            vt = vt_ref[0, c, h * B_VH:(h + 1) * B_VH, :].astype(BF16)
            s_list = []
            for comp in range(2):
                c0 = (h * 2 + comp) * B_HEAD_DIM
                qh = (q_ref[0, :, c0:c0 + B_HEAD_DIM] * scale).astype(BF16)
                kc = k_ref[0, pl.ds(off, tkb), c0:c0 + B_HEAD_DIM].astype(BF16)
                s_list.append(_dot_nt(kc, qh) + bias)
            _flash_update_t([2 * h, 2 * h + 1], s_list, vt, m_sc, l_sc, acc_sc)
        return carry

    lax.fori_loop(0, nbig, body, 0)
    lam = _diff_lambda(lp_ref, lam_init)
    for h in range(B_HEADS):
        o0 = (acc_sc[2 * h] / l_sc[2 * h]).T
        o1 = (acc_sc[2 * h + 1] / l_sc[2 * h + 1]).T
        o_ref[0, :, h * B_VH:(h + 1) * B_VH] = _diff_finish(o0, o1, lam, sg_ref[...], lam_init)


def _diff_prompt(qb, kb, vbt, dtab, lam_p, subln, lam_init, tkb):
    b, s, _ = qb.shape
    nq = s // TQ
    nc = s // tkb
    blk = lambda w: pl.BlockSpec((1, TQ, w), lambda bi, i: (bi, i, 0))
    full = lambda w: pl.BlockSpec((1, s, w), lambda bi, i: (bi, 0, 0))
    return pl.pallas_call(
        functools.partial(_diff_prompt_kernel, lam_init=lam_init, tkb=tkb),
        out_shape=jax.ShapeDtypeStruct((b, s, B_V), F32),
        grid=(b, nq),
        in_specs=[blk(B_QK), full(B_QK), pl.BlockSpec((1, nc, B_V, tkb), lambda bi, i: (bi, 0, 0, 0)),
                  _const_spec(dtab.shape), _const_spec(lam_p.shape), _const_spec(subln.shape)],
        out_specs=blk(B_V),
        scratch_shapes=_flash_scratch_t(2 * B_HEADS, B_VH, TQ),
        compiler_params=_cparams(("parallel", "arbitrary")),
        name="diff_prompt",
    )(qb, kb, vbt, dtab, lam_p, subln)


def _page_specs(pps, rows, cols):
    return [pl.BlockSpec((1, rows, cols), functools.partial(lambda b, s, pt, j: (pt[b, s * pps + j], 0, 0), j=j))
            for j in range(pps)]


def _all_page_specs(npages, rows, cols):
    return [pl.BlockSpec((1, rows, cols), functools.partial(lambda b, pt, j: (pt[b, j], 0, 0), j=j))
            for j in range(npages)]


def _dsa_sample_select_kernel(pt_ref, q_ref, wi_ref, kinew_ref, *rest, pps, npages, topk, group):
    del pt_ref
    page_refs = rest[:pps]
    tri_ref, selb_ref, sc_ref = rest[pps:]
    b = pl.program_id(0)
    s = pl.program_id(1)
    np1 = npages + 1
    q = q_ref[0].astype(BF16)
    wi = wi_ref[0] * (IDX_HEADS ** -0.5)

    def scores(kt):
        d = _dot(q, kt) * (IDX_DIM ** -0.5)
        return jnp.sum((jnp.maximum(d, 0.0) * wi).reshape(IDX_HEADS, T_PAD, TK), axis=0)

    base = b * np1
    for j in range(pps):
        sc_ref[base + s * pps + j] = scores(page_refs[j][0].astype(BF16))

    last_page_step = s == pl.num_programs(1) - 1

    @pl.when(last_page_step)
    def _():
        row = lax.broadcasted_iota(I32, (T_PAD, TK), 0)
        col = lax.broadcasted_iota(I32, (T_PAD, TK), 1)
        sc_ref[base + npages] = jnp.where(col <= row, scores(kinew_ref[0].astype(BF16)), -jnp.inf)

    @pl.when(last_page_step & (b == pl.num_programs(0) - 1))
    def _():
        def group_body(gi, carry):
            sk = sc_ref[pl.ds(gi * (group * np1), group * np1)].reshape(group, np1, T_PAD, TK)

            def count(pred):
                cnt = jnp.sum(jnp.where(pred, 1.0, 0.0), axis=1)
                return jnp.sum(cnt, axis=-1, keepdims=True)

            thr = _kth_largest(lambda t: count(sk >= t[:, None]), (group, T_PAD, 1), topk)
            need = float(topk) - count(sk > thr[:, None])
            eqf = jnp.where(sk == thr[:, None], 1.0, 0.0)
            before = _dot(eqf.reshape(group * np1 * T_PAD, TK).astype(BF16), tri_ref[...]).reshape(sk.shape)
            ties = jnp.sum(eqf, axis=-1, keepdims=True)
            off = jnp.zeros((group, T_PAD, 1), F32)
            for c in range(np1):
                selb_ref[pl.ds(gi * group, group), c] = _select_mask(sk[:, c], thr, need, before[:, c] + off)
                off = off + ties[:, c]
            return carry

        lax.fori_loop(0, pl.num_programs(0) // group, group_body, 0)


def _dsa_sample_select(page_table, q_rows, wi_rows, ki_new_t, cache_ikt, n_new, pps):
    db, npages = page_table.shape
    np1 = npages + 1
    topk = min(TOPK_MAX, (npages * TK + n_new) // 4)
    group = math.gcd(db, SUBLANES)
    tri = (jnp.arange(TK)[:, None] < jnp.arange(TK)[None, :]).astype(BF16)
    per_b = lambda shape: pl.BlockSpec((1,) + shape, lambda b, s, pt: (b,) + (0,) * len(shape))
    grid_spec = pltpu.PrefetchScalarGridSpec(
        num_scalar_prefetch=1,
        grid=(db, npages // pps),
        in_specs=[per_b(q_rows.shape[1:]), per_b(wi_rows.shape[1:]), per_b((IDX_DIM, TK))]
        + _page_specs(pps, IDX_DIM, TK) + [pl.BlockSpec(tri.shape, lambda b, s, pt: (0, 0))],
        out_specs=pl.BlockSpec((db, np1, T_PAD, TK), lambda b, s, pt: (0, 0, 0, 0)),
        scratch_shapes=[pltpu.VMEM((db * np1, T_PAD, TK), F32)],
    )
    return pl.pallas_call(
        functools.partial(_dsa_sample_select_kernel, pps=pps, npages=npages, topk=topk, group=group),
        out_shape=jax.ShapeDtypeStruct((db, np1, T_PAD, TK), F32),
        grid_spec=grid_spec,
        compiler_params=_cparams(("arbitrary", "arbitrary")),
        name="dsa_sample_select",
    )(page_table, q_rows, wi_rows, ki_new_t, *([cache_ikt] * pps), tri)


def _page_bias(btab_ref, c0, n, npages):
    return jnp.concatenate([btab_ref[jnp.where(c0 + j == npages - 1, 1, 0)] for j in range(n)], axis=1)


def _dsa_sample_attn_kernel(pt_ref, q_ref, knew_ref, vnew_ref, selb_ref, btab_ref, *rest, npages):
    del pt_ref
    o_ref = rest[2 * npages]
    kts = [r[0] for r in rest[:npages]] + [knew_ref[0]]
    vts = [r[0] for r in rest[npages:2 * npages]] + [vnew_ref[0]]
    n = npages + 1
    rows = A_HEADS * T_PAD
    scale = A_HEAD_DIM ** -0.5
    q = (q_ref[0] * scale).astype(BF16)
    sc = jnp.concatenate([_dot(q, kt.astype(BF16)) for kt in kts], axis=1)
    bias = jnp.concatenate([btab_ref[0]] * (npages - 1) + [btab_ref[1], btab_ref[2]], axis=1)
    mask = jnp.concatenate([selb_ref[0, c] for c in range(n)], axis=1)
    sc = ((sc + bias).reshape(A_HEADS, T_PAD, n * TK) + mask[None]).reshape(rows, n * TK)
    p = jnp.exp(sc - jnp.max(sc, axis=-1, keepdims=True))
    l = jnp.sum(p, axis=-1, keepdims=True)
    p = p.astype(BF16)
    acc = _dot_nt(p[:, 0:TK], vts[0].astype(BF16))
    for j in range(1, n):
        acc = acc + _dot_nt(p[:, j * TK:(j + 1) * TK], vts[j].astype(BF16))
    o = acc / l
    for h in range(A_HEADS):
        g0 = (h // A_REP) * A_HEAD_DIM
        o_ref[0, :, h * A_HEAD_DIM:(h + 1) * A_HEAD_DIM] = o[h * T_PAD:(h + 1) * T_PAD, g0:g0 + A_HEAD_DIM]


def _dsa_sample_attn(page_table, q_bd, k_new_t, v_new_t, selb, btab, cache_kt, cache_vt):
    db, npages = page_table.shape
    rows = A_HEADS * T_PAD
    per_b = lambda shape: pl.BlockSpec((1,) + shape, lambda b, pt: (b,) + (0,) * len(shape))
    grid_spec = pltpu.PrefetchScalarGridSpec(
        num_scalar_prefetch=1,
        grid=(db,),
        in_specs=[per_b((rows, A_KV)), per_b((A_KV, TK)), per_b((A_KV, TK)), per_b((npages + 1, T_PAD, TK)),
                  pl.BlockSpec(btab.shape, lambda b, pt: (0, 0, 0))]
        + _all_page_specs(npages, A_KV, TK) + _all_page_specs(npages, A_KV, TK),
        out_specs=per_b((T_PAD, A_Q)),
    )
    return pl.pallas_call(
        functools.partial(_dsa_sample_attn_kernel, npages=npages),
        out_shape=jax.ShapeDtypeStruct((db, T_PAD, A_Q), F32),
        grid_spec=grid_spec,
        compiler_params=_cparams(("parallel",)),
        name="dsa_sample_attn",
    )(page_table, q_bd, k_new_t, v_new_t, selb, btab, *([cache_kt] * npages), *([cache_vt] * npages))


def _diff_sample_kernel(pt_ref, q_ref, knew_ref, vnew_ref, btab_ref, lp_ref, sg_ref, *rest, pps, npages, lam_init):
    del pt_ref
    k_refs = rest[:pps]
    v_refs = rest[pps:2 * pps]
    o_ref, m_sc, l_sc, acc_sc = rest[2 * pps:]
    s = pl.program_id(1)

    @pl.when(s == 0)
    def _():
        _init_flash(m_sc, l_sc, acc_sc)

    scale = B_HEAD_DIM ** -0.5
    q = (q_ref[0] * scale).astype(BF16)
    hrows = 2 * T_PAD

    def step(kt_list, v_fn, bias):
        n = len(kt_list)
        sc = jnp.concatenate([_dot(q, kt.astype(BF16)) for kt in kt_list], axis=1) + bias

        def pv(p):
            outs = []
            for h in range(B_HEADS):
                ph = p[h * hrows:(h + 1) * hrows]
                out = _dot(ph[:, 0:TK], v_fn(0, h))
                for j in range(1, n):
                    out = out + _dot(ph[:, j * TK:(j + 1) * TK], v_fn(j, h))
                outs.append(out)
            return jnp.concatenate(outs, axis=0)

        _flash_update(0, sc, pv, m_sc, l_sc, acc_sc)

    def page_v(j, h):
        return v_refs[j][0, pl.ds(h, TK, stride=B_HEADS), :].astype(BF16)

    step([r[0] for r in k_refs], page_v, _page_bias(btab_ref, s * pps, pps, npages))

    @pl.when(s == pl.num_programs(1) - 1)
    def _():
        step([knew_ref[0]], lambda j, h: vnew_ref[0, :, h * B_VH:(h + 1) * B_VH].astype(BF16), btab_ref[2])
        o = acc_sc[0] / l_sc[0]
        lam = _diff_lambda(lp_ref, lam_init)
        for h in range(B_HEADS):
            r0 = h * hrows
            o_ref[0, :, h * B_VH:(h + 1) * B_VH] = _diff_finish(
                o[r0:r0 + T_PAD], o[r0 + T_PAD:r0 + hrows], lam, sg_ref[...], lam_init)


def _diff_sample(page_table, q_bd, k_new_t, v_new, btab, lam_p, subln, cache_kt, cache_v2, lam_init, pps):
    db, npages = page_table.shape
    rows = 2 * B_HEADS * T_PAD
    per_b = lambda shape: pl.BlockSpec((1,) + shape, lambda b, s, pt: (b,) + (0,) * len(shape))
    const = lambda shape: pl.BlockSpec(shape, lambda b, s, pt: (0,) * len(shape))
    grid_spec = pltpu.PrefetchScalarGridSpec(
        num_scalar_prefetch=1,
        grid=(db, npages // pps),
        in_specs=[per_b((rows, B_QK)), per_b((B_QK, TK)), per_b((TK, B_V)), const(btab.shape),
                  const(lam_p.shape), const(subln.shape)]
        + _page_specs(pps, B_QK, TK) + _page_specs(pps, TK * B_HEADS, B_VH),
        out_specs=per_b((T_PAD, B_V)),
        scratch_shapes=_flash_scratch(1, rows, B_VH),
    )
    return pl.pallas_call(
        functools.partial(_diff_sample_kernel, pps=pps, npages=npages, lam_init=lam_init),
        out_shape=jax.ShapeDtypeStruct((db, T_PAD, B_V), F32),
        grid_spec=grid_spec,
        compiler_params=_cparams(("parallel", "arbitrary")),
        name="diff_sample",
    )(page_table, q_bd, k_new_t, v_new, btab, lam_p, subln, *([cache_kt] * pps), *([cache_v2] * pps))


def _mid_kernel(x_ref, ma_ref, mb_ref, mk_ref, mv_ref, wout_ref, gx_ref, wq_ref, gm_ref, qg_ref, wo_ref, h2_ref):
    h = (x_ref[0] + _dot(ma_ref[0].astype(BF16), wout_ref[:A_Q, :])
         + _dot(mb_ref[0].astype(BF16), wout_ref[A_Q:, :]))
    hn = _rms(h) * gx_ref[...]
    q = _dot(hn.astype(BF16), wq_ref[...])
    ms = _dot((q * q).astype(BF16), gm_ref[...])
    q = (q * lax.rsqrt(ms + EPS) * qg_ref[...]).astype(BF16)
    m_tok = mk_ref.shape[1] // MEM_HEADS
    outs = []
    for hh in range(MEM_HEADS):
        sl = slice(hh * MEM_HEAD_DIM, (hh + 1) * MEM_HEAD_DIM)
        mk = mk_ref[0, pl.ds(hh, m_tok, stride=MEM_HEADS), :].astype(BF16)
        mv = mv_ref[0, pl.ds(hh, m_tok, stride=MEM_HEADS), :].astype(BF16)
        s = _dot_nt(q[:, sl], mk) * (MEM_HEAD_DIM ** -0.5)
        p = jnp.exp(s - jnp.max(s, axis=-1, keepdims=True))
        l = jnp.sum(p, axis=-1, keepdims=True)
        outs.append(_dot(p.astype(BF16), mv) / l)
    o = jnp.concatenate(outs, axis=-1)
    h2_ref[0] = h + _dot(o.astype(BF16), wo_ref[...])


def _mid(x, mix_a, mix_b, mk, mv, w_out, g_x, w_q, gmat, q_gain, w_o, tm):
    b, s, d = x.shape
    blk = lambda w: pl.BlockSpec((1, tm, w), lambda bi, i: (bi, i, 0))
    per_b = lambda w: pl.BlockSpec((1, mk.shape[1], w), lambda bi, i: (bi, 0, 0))
    return pl.pallas_call(
        _mid_kernel,
        out_shape=jax.ShapeDtypeStruct((b, s, d), F32),
        grid=(b, s // tm),
        in_specs=[blk(d), blk(A_Q), blk(B_V), per_b(MEM_HEAD_DIM), per_b(MEM_HEAD_DIM), _const_spec(w_out.shape),
                  _const_spec(g_x.shape), _const_spec(w_q.shape), _const_spec(gmat.shape),
                  _const_spec(q_gain.shape), _const_spec(w_o.shape)],
        out_specs=blk(d),
        compiler_params=_cparams(("parallel", "arbitrary")),
        name="mid",
    )(x, mix_a, mix_b, mk, mv, w_out, g_x, w_q, gmat, q_gain, w_o)


def _ffn_core(h, gn, wg_ref, wu_ref, cw_ref, cb_ref, wd_ref, shifted):
    xb = (_rms(h) * gn).astype(BF16)
    g = _dot(xb, wg_ref[...])
    u = _dot(xb, wu_ref[...])
    gm1, gm2 = shifted(g)
    gc = cb_ref[...] + cw_ref[0:1, :] * gm2 + cw_ref[1:2, :] * gm1 + cw_ref[2:3, :] * g
    a = gc / (1.0 + jnp.exp(-gc)) * u
    return h + _dot(a.astype(BF16), wd_ref[...]), g


def _ffn_prompt_kernel(h_ref, gn_ref, wg_ref, wu_ref, cw_ref, cb_ref, wd_ref, y_ref, tail_ref, carry_ref):
    @pl.when(pl.program_id(1) == 0)
    def _():
        carry_ref[...] = jnp.zeros(carry_ref.shape, F32)

    tm = h_ref.shape[1]
    row = lax.broadcasted_iota(I32, (tm, 1), 0)
    c0 = carry_ref[SUBLANES - 2:SUBLANES - 1, :]
    c1 = carry_ref[SUBLANES - 1:SUBLANES, :]

    def shifted(g):
        gm1 = jnp.where(row == 0, c1, pltpu.roll(g, 1, 0))
        gm2 = jnp.where(row == 0, c0, jnp.where(row == 1, c1, pltpu.roll(g, 2, 0)))
        return gm1, gm2

    y, g = _ffn_core(h_ref[0], gn_ref[...], wg_ref, wu_ref, cw_ref, cb_ref, wd_ref, shifted)
    y_ref[0] = y
    tail = g[tm - SUBLANES:, :]
    carry_ref[...] = tail
    tail_ref[0] = tail


def _ffn_prompt(h, gn, wg, wu, cw, cb, wd, tm):
    b, s, d = h.shape
    f = wg.shape[1]
    blk = pl.BlockSpec((1, tm, d), lambda bi, i: (bi, i, 0))
    wspec = lambda shape: _const_spec(shape, single_buffer=True)
    return pl.pallas_call(
        _ffn_prompt_kernel,
        out_shape=[jax.ShapeDtypeStruct((b, s, d), F32), jax.ShapeDtypeStruct((b, SUBLANES, f), F32)],
        grid=(b, s // tm),
        in_specs=[blk, _const_spec(gn.shape), wspec(wg.shape), wspec(wu.shape), _const_spec(cw.shape),
                  _const_spec(cb.shape), wspec(wd.shape)],
        out_specs=[blk, pl.BlockSpec((1, SUBLANES, f), lambda bi, i: (bi, 0, 0))],
        scratch_shapes=[pltpu.VMEM((SUBLANES, f), F32)],
        compiler_params=_cparams(("arbitrary", "arbitrary")),
        name="ffn_prompt",
    )(h, gn, wg, wu, cw, cb, wd)


def _ffn_sample_kernel(h_ref, gn_ref, wg_ref, wu_ref, cw_ref, cb_ref, wd_ref, st1_ref, st2_ref, y_ref, g_ref):
    m = h_ref.shape[0]
    t = lax.broadcasted_iota(I32, (m, 1), 0) & (T_PAD - 1)

    def shifted(g):
        gm1 = jnp.where(t == 0, st1_ref[...], pltpu.roll(g, 1, 0))
        gm2 = jnp.where(t < 2, st2_ref[...], pltpu.roll(g, 2, 0))
        return gm1, gm2

    y, g = _ffn_core(h_ref[...], gn_ref[...], wg_ref, wu_ref, cw_ref, cb_ref, wd_ref, shifted)
    y_ref[...] = y
    g_ref[...] = g


def _ffn_sample(h2d, gn, wg, wu, cw, cb, wd, st1, st2):
    m, d = h2d.shape
    f = wg.shape[1]
    wspec = lambda shape: _const_spec(shape, single_buffer=True)
    return pl.pallas_call(
        _ffn_sample_kernel,
        out_shape=[jax.ShapeDtypeStruct((m, d), F32), jax.ShapeDtypeStruct((m, f), F32)],
        grid=(1,),
        in_specs=[_const_spec((m, d)), _const_spec(gn.shape), wspec(wg.shape), wspec(wu.shape),
                  _const_spec(cw.shape), _const_spec(cb.shape), wspec(wd.shape), _const_spec((m, f)),
                  _const_spec((m, f))],
        out_specs=[_const_spec((m, d)), _const_spec((m, f))],
        compiler_params=_cparams(("arbitrary",)),
        name="ffn_sample",
    )(h2d, gn, wg, wu, cw, cb, wd, st1, st2)


def _rel_bucket(dist):
    n = np.maximum(dist, 0)
    max_exact = NUM_BUCKETS // 2
    nf = np.maximum(n, 1).astype(np.float32)
    log_b = (np.log(nf / np.float32(max_exact)) / np.float32(math.log(MAX_DISTANCE / max_exact))
             * np.float32(NUM_BUCKETS - max_exact))
    large = np.minimum(max_exact + log_b.astype(np.int32), NUM_BUCKETS - 1)
    return np.where(n < max_exact, n, large)


def _bias_by_dist(dist, causal, bias):
    onehot = (_rel_bucket(dist)[..., None] == np.arange(NUM_BUCKETS)).astype(np.float32)
    vals = jnp.einsum("...k,kh->h...", onehot, bias, precision=lax.Precision.HIGHEST)
    return jnp.where(causal[None], vals, NEG).astype(F32)


def _prompt_bias_tables(bias):
    r = np.arange(TQ)[:, None]
    c = np.arange(TK)[None, :]
    always = np.ones((TQ, TK), bool)
    masked = _bias_by_dist(r - c, ~always, bias)
    t0 = _bias_by_dist(r - c, r >= c, bias)
    t1 = _bias_by_dist(r - c + TK, always, bias)
    t2 = _bias_by_dist(r - c + 2 * TK, always, bias)
    return jnp.swapaxes(jnp.stack([masked, t0, t1, t2], axis=1), -1, -2) * LOG2E


def _sample_bias_tables(bias, streams_per_head):
    t = np.arange(T_PAD)[:, None]
    c = np.arange(TK)[None, :]
    always = np.ones((T_PAD, TK), bool)
    far = _bias_by_dist(t - c + 2 * TK, always, bias)
    last = _bias_by_dist(t - c + TK, always, bias)
    new = _bias_by_dist(t - c, c <= t, bias)
    tabs = jnp.stack([far, last, new], axis=0)
    tabs = jnp.repeat(tabs[:, :, None], streams_per_head, axis=2)
    return tabs.reshape(3, -1, TK)


def _group_mean_matrix(width, group):
    idx = jnp.arange(width) // group
    return jnp.where(idx[:, None] == idx[None, :], 1.0 / group, 0.0).astype(BF16)


def _pad_rows(x, rows):
    return jnp.pad(x, ((0, 0), (0, rows - x.shape[1]), (0, 0)))


def _new_keys_t(x, rows):
    return jnp.swapaxes(_pad_rows(x, rows), 1, 2)


def kernel(x_prompt, x_sample, mem_prompt, cache_a_k, cache_a_v, cache_idx_k, cache_b_k, cache_b_v, cache_mem_k, cache_mem_v, state_ffn_conv, page_table, rel_bias, norm_mix, w_in, a_q_norm, a_k_norm, b_q_norm, b_k_norm, diff_lambda, diff_subln, w_out, norm_mem_x, norm_mem_src, w_mem_q, w_mem_kv, mem_q_norm, mem_k_norm, w_mem_o, norm_ffn, w_up, w_gate, ffn_conv_w, ffn_conv_b, w_down):
    depth = w_in.shape[0]
    assert depth == 1, "single-layer trunk"
    layer = 0
    lam_init = 0.8 - 0.6 * math.exp(-0.3 * layer)
    b, s, d = x_prompt.shape
    db, t_new, _ = x_sample.shape
    assert CONV_W - 1 <= t_new <= T_PAD and cache_a_k.shape[2] == TK
    m_tok = mem_prompt.shape[1]
    f = w_up.shape[-1]
    n_pool = cache_a_k.shape[1]
    tkb = min(TKB_MAX, s)
    assert s % tkb == 0 and tkb % TK == 0

    w_in_l = w_in[layer]
    n_front = A_Q + 2 * A_KV + IDX_Q + IDX_DIM + IDX_HEADS
    w_in_p = jnp.concatenate(
        [w_in_l[:, :n_front], jnp.zeros((d, C_QB - n_front), F32), w_in_l[:, n_front:]], axis=1).astype(BF16)
    ones = lambda n: jnp.ones((n,), F32)
    hgain = jnp.concatenate([
        jnp.tile(a_q_norm[layer], A_HEADS), jnp.tile(a_k_norm[layer], A_KV_HEADS), ones(C_QB - C_VA),
        jnp.tile(b_q_norm[layer], 2 * B_HEADS), jnp.tile(b_k_norm[layer], 2 * B_HEADS), ones(B_V)])[None, :]
    gmat64 = _group_mean_matrix(A_Q, A_HEAD_DIM)
    gmat128 = _group_mean_matrix(MEM_W, MEM_HEAD_DIM)
    proj_segs = ((C_QA, A_Q, True, A_Q, 1), (C_KA, A_KV, True, A_KV, 1), (C_VA, A_KV, False, A_KV, 1),
                 (C_QI, IDX_Q, False, IDX_Q, 1), (C_KIW, LANES, False, LANES, 1), (C_KIW, LANES, False, IDX_DIM, 1),
                 (C_QB, B_QK, True, B_QK, 1), (C_KB, B_QK, True, B_QK, 1), (C_VB, B_V, False, B_V, 1))
    prompt_segs = ((C_QA, A_Q, True, A_Q, 1), (C_KA, A_KV, True, A_KV, 1), (C_QI, IDX_Q, False, IDX_Q, 1),
                   (C_KIW, LANES, False, LANES, 1), (C_QB, B_QK, True, B_QK, 1), (C_KB, B_QK, True, B_QK, 1),
                   (C_VA, A_KV, False, A_KV, CHUNKED_T), (C_VB, B_V, False, B_V, CHUNKED_T),
                   (C_KA, A_KV, True, A_KV, FEATURE_MAJOR), (C_VA, A_KV, False, A_KV, FEATURE_MAJOR),
                   (C_KIW, LANES, False, IDX_DIM, FEATURE_MAJOR), (C_KB, B_QK, True, B_QK, FEATURE_MAJOR),
                   (C_VB, B_V, False, B_V, B_HEADS))
    g_mix = norm_mix[layer][None, :]
    w_out_b = w_out[layer].astype(BF16)
    w_q_b = w_mem_q[layer].astype(BF16)
    w_o_b = w_mem_o[layer].astype(BF16)
    w_kv_b = w_mem_kv[layer].astype(BF16)
    w_gate_b = w_gate[layer].astype(BF16)
    w_up_b = w_up[layer].astype(BF16)
    w_down_b = w_down[layer].astype(BF16)
    g_memx = norm_mem_x[layer][None, :]
    q_gain = jnp.tile(mem_q_norm[layer], MEM_HEADS)[None, :]
    kv_gain = jnp.concatenate([jnp.tile(mem_k_norm[layer], MEM_HEADS), ones(MEM_W)])[None, :]
    g_ffn = norm_ffn[layer][None, :]
    conv_w = ffn_conv_w[layer]
    conv_b = ffn_conv_b[layer][None, :]
    lam_p = diff_lambda[layer]
    subln = diff_subln[layer][None, :]
    bias_a = rel_bias[:, :A_HEADS]
    bias_b = rel_bias[:, A_HEADS:]

    assert tkb == TM_TOKENS, "the projection emits one value chunk per token tile"
    qa, ka, qi, kiw, qb, kb, va_ch, vb_ch, ka_t, va_t, ki_t, kb_t, vb_c = _norm_proj(
        x_prompt.reshape(b * s, d), g_mix, w_in_p, gmat64, hgain, prompt_segs, TM_TOKENS, "proj_prompt", seq=s)

    def from_feature_major(a_t, shape):
        nd = len(shape)
        return jnp.transpose(a_t.reshape((b,) + shape + (s,)), (0, nd + 1) + tuple(range(1, nd + 1)))[None]
    r3 = lambda a: a.reshape(b, s, a.shape[-1])
    mix_a = _dsa_prompt(r3(qi), r3(kiw), r3(qa), r3(ka), va_ch, _prompt_bias_tables(bias_a), tkb)
    mix_b = _diff_prompt(r3(qb), r3(kb), vb_ch, _prompt_bias_tables(bias_b), lam_p, subln, lam_init, tkb)
    mk, mv = _norm_proj(mem_prompt.reshape(b * m_tok, d), norm_mem_src[layer][None, :], w_kv_b, gmat128, kv_gain,
                        ((0, MEM_W, True, MEM_W, MEM_HEADS), (MEM_W, MEM_W, False, MEM_W, MEM_HEADS)), 256, "mem_kv")
    mem_rows = lambda a: a.reshape(-1, m_tok * MEM_HEADS, MEM_HEAD_DIM)
    h2 = _mid(x_prompt, mix_a, mix_b, mem_rows(mk), mem_rows(mv), w_out_b, g_memx, w_q_b, gmat128, q_gain, w_o_b,
              TM_TOKENS)
    yp, tail = _ffn_prompt(h2, g_ffn, w_gate_b, w_up_b, conv_w, conv_b, w_down_b, 256)
    conv_p = tail[:, SUBLANES - (CONV_W - 1):, :]

    xs = _pad_rows(x_sample, T_PAD)
    qa, ka_s, va_s, qi, kiw, ki_s, qb, kb_s, vb_s = _norm_proj(
        xs.reshape(db * T_PAD, d), g_mix, w_in_p, gmat64, hgain, proj_segs, db * T_PAD, "proj_sample")
    r3 = lambda a: a.reshape(db, T_PAD, a.shape[-1])
    idx_kt = jnp.transpose(cache_idx_k[layer], (0, 2, 1))
    a_kt = jnp.transpose(cache_a_k[layer], (0, 2, 3, 1)).reshape(n_pool, A_KV, TK)
    a_vt = jnp.transpose(cache_a_v[layer], (0, 2, 3, 1)).reshape(n_pool, A_KV, TK)
    b_kt = jnp.transpose(cache_b_k[layer], (0, 2, 3, 4, 1)).reshape(n_pool, B_QK, TK)
    b_v2 = cache_b_v[layer].reshape(n_pool, TK * B_HEADS, B_VH)
    n_pages = page_table.shape[1]
    qi_rows = r3(qi).reshape(db, T_PAD, IDX_HEADS, IDX_DIM).transpose(0, 2, 1, 3).reshape(db, IDX_HEADS * T_PAD, IDX_DIM)
    wi_rows = r3(kiw)[:, :, IDX_DIM:IDX_DIM + IDX_HEADS].transpose(0, 2, 1).reshape(db, IDX_HEADS * T_PAD, 1)
    selb = _dsa_sample_select(page_table, qi_rows, wi_rows, _new_keys_t(r3(ki_s), TK), idx_kt, t_new,
                              math.gcd(n_pages, SELECT_PAGES_PER_STEP))
    q_rows = r3(qa).reshape(db, T_PAD, A_HEADS, A_HEAD_DIM).transpose(0, 2, 1, 3)
    eye_g = jnp.repeat(jnp.eye(A_KV_HEADS, dtype=F32), A_REP, axis=0)
    qa_bd = (q_rows[:, :, :, None, :] * eye_g[None, :, None, :, None]).reshape(db, A_HEADS * T_PAD, A_KV)
    mix_a = _dsa_sample_attn(page_table, qa_bd, _new_keys_t(r3(ka_s), TK), _new_keys_t(r3(va_s), TK), selb,
                             _sample_bias_tables(bias_a, 1), a_kt, a_vt)
    n_str = 2 * B_HEADS
    q_rows = r3(qb).reshape(db, T_PAD, n_str, B_HEAD_DIM).transpose(0, 2, 1, 3)
    qb_bd = (q_rows[:, :, :, None, :] * jnp.eye(n_str, dtype=F32)[None, :, None, :, None]).reshape(
        db, n_str * T_PAD, B_QK)
    mix_b = _diff_sample(page_table, qb_bd, _new_keys_t(r3(kb_s), TK), _pad_rows(r3(vb_s), TK),
                         _sample_bias_tables(bias_b, 2), lam_p, subln, b_kt, b_v2, lam_init,
                         math.gcd(n_pages, DIFF_PAGES_PER_STEP))
    h2 = _mid(xs, mix_a, mix_b, mem_rows(cache_mem_k[layer]), mem_rows(cache_mem_v[layer]), w_out_b, g_memx, w_q_b,
              gmat128, q_gain, w_o_b, T_PAD)
    state = state_ffn_conv[layer]
    st1 = _pad_rows(state[:, 1:2, :], T_PAD).reshape(db * T_PAD, f)
    st2 = _pad_rows(state, T_PAD).reshape(db * T_PAD, f)
    ys, g_s = _ffn_sample(h2.reshape(db * T_PAD, d), g_ffn, w_gate_b, w_up_b, conv_w, conv_b, w_down_b, st1, st2)
    ys = ys.reshape(db, T_PAD, d)[:, :t_new]
    conv_s = g_s.reshape(db, T_PAD, f)[:, t_new - (CONV_W - 1):t_new]

    def new_rows(a, shape):
        return a.reshape(db, T_PAD, -1)[:, :t_new].reshape((1, db, t_new) + shape)

    return (yp, ys,
            from_feature_major(ka_t, (A_KV_HEADS, A_HEAD_DIM)), from_feature_major(va_t, (A_KV_HEADS, A_HEAD_DIM)),
            from_feature_major(ki_t, (IDX_DIM,)), from_feature_major(kb_t, (B_HEADS, 2, B_HEAD_DIM)),
            vb_c.reshape(1, b, s, B_HEADS, 2 * B_HEAD_DIM),
            mk.reshape(1, b, m_tok, MEM_HEADS, MEM_HEAD_DIM), mv.reshape(1, b, m_tok, MEM_HEADS, MEM_HEAD_DIM),
            conv_p[None],
            new_rows(ka_s, (A_KV_HEADS, A_HEAD_DIM)), new_rows(va_s, (A_KV_HEADS, A_HEAD_DIM)),
            new_rows(ki_s, (IDX_DIM,)), new_rows(kb_s, (B_HEADS, 2, B_HEAD_DIM)),
            new_rows(vb_s, (B_HEADS, 2 * B_HEAD_DIM)), conv_s[None])
```

```python
import functools
import math

import jax
import jax.numpy as jnp
import numpy as np
from jax import lax
from jax.experimental import pallas as pl
from jax.experimental.pallas import tpu as pltpu

F32 = jnp.float32
BF16 = jnp.bfloat16
I32 = jnp.int32

EPS = 1e-6
NEG = -1e30
LOG2E = math.log2(math.e)
INT_MIN = -(2 ** 31)

A_HEADS = 8
A_KV_HEADS = 2
A_HEAD_DIM = 64
IDX_HEADS = 4
IDX_DIM = 64
TOPK_MAX = 256
B_HEADS = 4
B_HEAD_DIM = 64
MEM_HEADS = 4
MEM_HEAD_DIM = 128
CONV_W = 3
NUM_BUCKETS = 32
MAX_DISTANCE = 128

A_Q = A_HEADS * A_HEAD_DIM
A_KV = A_KV_HEADS * A_HEAD_DIM
IDX_Q = IDX_HEADS * IDX_DIM
B_QK = B_HEADS * 2 * B_HEAD_DIM
B_V = B_HEADS * 2 * B_HEAD_DIM
B_VH = 2 * B_HEAD_DIM
MEM_W = MEM_HEADS * MEM_HEAD_DIM
A_REP = A_HEADS // A_KV_HEADS

LANES = 128
SUBLANES = 8
TQ_DSA = 512
TQ_DIFF = 128
TK = 128
TKB_MAX = 512
T_PAD = SUBLANES
TM_TOKENS = 512
FEATURE_MAJOR = 0
CHUNKED_T = -1
SELECT_PAGES_PER_STEP = 32
DIFF_PAGES_PER_STEP = 16
VMEM_LIMIT = 56 * 1024 * 1024

C_QA = 0
C_KA = C_QA + A_Q
C_VA = C_KA + A_KV
C_QI = C_VA + A_KV
C_KIW = C_QI + IDX_Q
C_QB = C_KIW + LANES
C_KB = C_QB + B_QK
C_VB = C_KB + B_QK
D_IN_PAD = C_VB + B_V


def _dot(a, b):
    return jnp.dot(a, b, preferred_element_type=F32)


def _dot_nt(a, b):
    return lax.dot_general(a, b, (((1,), (1,)), ((), ())), preferred_element_type=F32)


def _rms(x):
    return x * lax.rsqrt(jnp.mean(x * x, axis=-1, keepdims=True) + EPS)


KEY_NEG_INF = -(2 ** 31) + 0x7FFFFF


def _key_to_float(key):
    bits = key ^ ((key >> 31) & 0x7FFFFFFF)
    return jnp.where(key <= KEY_NEG_INF, -jnp.inf, lax.bitcast_convert_type(bits, F32))


def _kth_largest(count_ge, shape, topk):
    def bit_body(it, key):
        cand = key ^ lax.shift_left(jnp.int32(1), 31 - it)
        return jnp.where(count_ge(_key_to_float(cand)) >= topk, cand, key)

    return _key_to_float(lax.fori_loop(0, 32, bit_body, jnp.full(shape, INT_MIN, I32)))


def _cparams(sem, vmem=VMEM_LIMIT):
    return pltpu.CompilerParams(dimension_semantics=sem, vmem_limit_bytes=vmem)


def _const_spec(shape, single_buffer=False):
    nd = len(shape)
    if single_buffer:
        return pl.BlockSpec(shape, lambda *_: (0,) * nd, pipeline_mode=pl.Buffered(1))
    return pl.BlockSpec(shape, lambda *_: (0,) * nd)


def _norm_proj_kernel(x_ref, g_ref, w_ref, gm_ref, hg_ref, *out_refs, segs):
    xn = _rms(x_ref[...]) * g_ref[...]
    p = _dot(xn.astype(BF16), w_ref[...])
    tm = x_ref.shape[0]
    done = {}
    for (start, width, norm, out_width, split), o_ref in zip(segs, out_refs):
        if (start, width, norm) not in done:
            s = p[:, start:start + width]
            if norm:
                ms = _dot((s * s).astype(BF16), gm_ref[:width, :width])
                s = s * lax.rsqrt(ms + EPS) * hg_ref[:, start:start + width]
            done[(start, width, norm)] = s
        s = done[(start, width, norm)]
        if split == 1:
            o_ref[...] = s[:, :out_width]
        elif split == FEATURE_MAJOR:
            o_ref[0] = s.T[:out_width, :]
        elif split == CHUNKED_T:
            o_ref[0, 0] = s.T[:out_width, :]
        else:
            pw = out_width // split
            for j in range(split):
                o_ref[pl.ds(j, tm, stride=split), :] = s[:, j * pw:(j + 1) * pw]


def _norm_proj(x2d, gain, w, gmat, hgain, segs, tm, name, seq=None):
    m, d = x2d.shape
    n = w.shape[1]
    tiles = None if seq is None else seq // tm

    def shape_spec(ow, sp):
        if sp == FEATURE_MAJOR:
            return (jax.ShapeDtypeStruct((m // seq, ow, seq), F32),
                    pl.BlockSpec((1, ow, tm), lambda i: (i // tiles, 0, i % tiles)))
        if sp == CHUNKED_T:
            return (jax.ShapeDtypeStruct((m // seq, tiles, ow, tm), F32),
                    pl.BlockSpec((1, 1, ow, tm), lambda i: (i // tiles, i % tiles, 0, 0)))
        return (jax.ShapeDtypeStruct((m * sp, ow // sp), F32), pl.BlockSpec((tm * sp, ow // sp), lambda i: (i, 0)))

    shapes, specs = zip(*[shape_spec(ow, sp) for (_, _, _, ow, sp) in segs])
    return pl.pallas_call(
        functools.partial(_norm_proj_kernel, segs=segs),
        out_shape=list(shapes),
        grid=(m // tm,),
        in_specs=[pl.BlockSpec((tm, d), lambda i: (i, 0)), _const_spec((1, d)), _const_spec((d, n)),
                  _const_spec(gmat.shape), _const_spec((1, n))],
        out_specs=list(specs),
        compiler_params=_cparams(("parallel",)),
        name=name,
    )(x2d, gain, w, gmat, hgain)


def _init_flash(m_sc, l_sc, acc_sc):
    m_sc[...] = jnp.full(m_sc.shape, -jnp.inf, F32)
    l_sc[...] = jnp.zeros(l_sc.shape, F32)
    acc_sc[...] = jnp.zeros(acc_sc.shape, F32)


def _flash_update(st, s, pv_fn, m_sc, l_sc, acc_sc):
    m_old = m_sc[st]
    m_new = jnp.maximum(m_old, jnp.max(s, axis=-1, keepdims=True))
    alpha = jnp.exp(m_old - m_new)
    p = jnp.exp(s - m_new)
    l_sc[st] = alpha * l_sc[st] + jnp.sum(p, axis=-1, keepdims=True)
    acc_sc[st] = alpha * acc_sc[st] + pv_fn(p.astype(BF16))
    m_sc[st] = m_new


def _flash_scratch(streams, rows, dv):
    return [pltpu.VMEM((streams, rows, 1), F32), pltpu.VMEM((streams, rows, 1), F32),
            pltpu.VMEM((streams, rows, dv), F32)]


def _select_mask(sk, thr, need, before):
    tie_ok = jnp.where(sk == thr, jnp.where(before < need, 1.0, 0.0), 0.0)
    real = jnp.where(sk > -jnp.inf, 1.0, 0.0)
    sel = jnp.where(sk > thr, real, tie_ok * real)
    return jnp.where(sel > 0.5, 0.0, NEG)


def _chunk_bias(dtab_ref, h, i, c, ksub, tq):
    qsub = tq // TK
    return jnp.concatenate(
        [jnp.concatenate([dtab_ref[h, jnp.clip(i * qsub + a - (c * ksub + j) + 1, 0, 3)] for a in range(qsub)], axis=1)
         for j in range(ksub)], axis=0)


def _flash_update_t(st, s, vt, m_sc, l_sc, acc_sc):
    m_old = m_sc[st]
    m_new = jnp.maximum(m_old, jnp.max(_col_reduce(s, jnp.max), axis=0, keepdims=True))
    alpha = jnp.exp2(m_old - m_new)
    p = jnp.exp2(s - m_new)
    l_sc[st] = alpha * l_sc[st] + jnp.sum(_col_reduce(p, jnp.sum), axis=0, keepdims=True)
    acc_sc[st] = alpha * acc_sc[st] + _dot(vt, p.astype(BF16))
    m_sc[st] = m_new


def _flash_scratch_t(streams, dv, cols):
    return [pltpu.VMEM((streams, 1, cols), F32), pltpu.VMEM((streams, 1, cols), F32),
            pltpu.VMEM((streams, dv, cols), F32)]


REDUCE_WAYS = 8


def _col_reduce(x, op):
    rows, cols = x.shape
    slabs = rows // SUBLANES
    if slabs % REDUCE_WAYS == 0 and slabs > REDUCE_WAYS:
        x = op(x.reshape(REDUCE_WAYS, slabs // REDUCE_WAYS, SUBLANES, cols), axis=1)
    else:
        x = x.reshape(slabs, SUBLANES, cols)
    return op(x, axis=0)


def _col_count(w):
    return _col_reduce(w, jnp.sum)


def _dsa_prompt_kernel(qi_ref, wit_ref, kiw_ref, qa_ref, ka_ref, vat_ref, dtab_ref, tril_ref, o_ref,
                       skey_ref, selb_ref, m_sc, l_sc, acc_sc, *, topk, tkb):
    tq = TQ_DSA
    i = pl.program_id(1)
    ksub = tkb // TK
    nbig = (i * tq + tq - 1) // tkb + 1
    qi = qi_ref[0].astype(BF16)
    wit = wit_ref[0] * (IDX_HEADS ** -0.5)
    krow = lax.broadcasted_iota(I32, (tkb, tq), 0)
    qcol = lax.broadcasted_iota(I32, (tkb, tq), 1)

    def score_body(c, carry):
        off = pl.multiple_of(c * tkb, tkb)
        kc = kiw_ref[0, pl.ds(off, tkb), 0:IDX_DIM].astype(BF16)
        sc = jnp.zeros((tkb, tq), F32)
        for h in range(IDX_HEADS):
            d = _dot_nt(kc, qi[:, h * IDX_DIM:(h + 1) * IDX_DIM]) * (IDX_DIM ** -0.5)
            sc = sc + jnp.maximum(d, 0.0) * wit[h:h + 1, :]
        causal = (c * tkb + krow) <= (i * tq + qcol)
        skey_ref[c] = jnp.where(causal, sc, -jnp.inf)
        return carry

    lax.fori_loop(0, nbig, score_body, 0)

    def count(pred_fn):
        def body(c, cnt):
            return cnt + _col_count(jnp.where(pred_fn(skey_ref[c]), 1.0, 0.0))
        cnt = lax.fori_loop(0, nbig, body, jnp.zeros((SUBLANES, tq), F32))
        return jnp.sum(cnt, axis=0, keepdims=True)

    thr = _kth_largest(lambda t: count(lambda sk: sk >= t), (1, tq), topk)
    need = float(topk) - count(lambda sk: sk > thr)
    tril = tril_ref[...]

    def sel_body(c, off):
        sk = skey_ref[c]
        eqf = jnp.where(sk == thr, 1.0, 0.0)
        before = _dot(tril, eqf.astype(BF16)) + off
        selb_ref[c] = _select_mask(sk, thr, need, before)
        return off + jnp.sum(_col_count(eqf), axis=0, keepdims=True)

    lax.fori_loop(0, nbig, sel_body, jnp.zeros((1, tq), F32))

    _init_flash(m_sc, l_sc, acc_sc)
    scale = A_HEAD_DIM ** -0.5 * LOG2E
    qg = [jnp.concatenate([qa_ref[0, :, h * A_HEAD_DIM:(h + 1) * A_HEAD_DIM] * scale
                           for h in range(g * A_REP, (g + 1) * A_REP)], axis=0).astype(BF16)
          for g in range(A_KV_HEADS)]

    def att_body(c, carry):
        off = pl.multiple_of(c * tkb, tkb)
        maskb = selb_ref[c]
        for g in range(A_KV_HEADS):
            g0 = g * A_HEAD_DIM
            kc = ka_ref[0, pl.ds(off, tkb), g0:g0 + A_HEAD_DIM].astype(BF16)
            vt = vat_ref[0, c, g0:g0 + A_HEAD_DIM, :].astype(BF16)
            sg = _dot_nt(kc, qg[g])
            for r in range(A_REP):
                h = g * A_REP + r
                s = sg[:, r * tq:(r + 1) * tq] + (_chunk_bias(dtab_ref, h, i, c, ksub, tq) + maskb)
                _flash_update_t(h, s, vt, m_sc, l_sc, acc_sc)
        return carry

    lax.fori_loop(0, nbig, att_body, 0)
    for h in range(0, A_HEADS, 2):
        ot = jnp.concatenate([acc_sc[h] / l_sc[h], acc_sc[h + 1] / l_sc[h + 1]], axis=0)
        o_ref[0, :, h * A_HEAD_DIM:(h + 2) * A_HEAD_DIM] = ot.T


def _dsa_prompt(qi, kiw, qa, ka, vat, dtab, tkb):
    b, s, _ = qa.shape
    tq = TQ_DSA
    nq = s // tq
    nc = s // tkb
    topk = min(TOPK_MAX, s // 4)
    tril = (jnp.arange(tkb)[None, :] < jnp.arange(tkb)[:, None]).astype(BF16)
    wit = jnp.swapaxes(kiw[:, :, IDX_DIM:IDX_DIM + SUBLANES], 1, 2)
    blk = lambda w: pl.BlockSpec((1, tq, w), lambda bi, i: (bi, i, 0))
    full = lambda w: pl.BlockSpec((1, s, w), lambda bi, i: (bi, 0, 0))
    return pl.pallas_call(
        functools.partial(_dsa_prompt_kernel, topk=topk, tkb=tkb),
        out_shape=jax.ShapeDtypeStruct((b, s, A_Q), F32),
        grid=(b, nq),
        in_specs=[blk(IDX_Q), pl.BlockSpec((1, SUBLANES, tq), lambda bi, i: (bi, 0, i)), full(LANES), blk(A_Q),
                  full(A_KV), pl.BlockSpec((1, nc, A_KV, tkb), lambda bi, i: (bi, 0, 0, 0)),
                  _const_spec(dtab.shape), _const_spec(tril.shape)],
        out_specs=blk(A_Q),
        scratch_shapes=[pltpu.VMEM((nc, tkb, tq), F32), pltpu.VMEM((nc, tkb, tq), F32)]
        + _flash_scratch_t(A_HEADS, A_HEAD_DIM, tq),
        compiler_params=_cparams(("parallel", "arbitrary")),
        name="dsa_prompt",
    )(qi, wit, kiw, qa, ka, vat, dtab, tril)


def _diff_lambda(lp_ref, lam_init):
    lp = lp_ref[...]
    s1 = jnp.sum(lp[0:1] * lp[1:2], axis=-1, keepdims=True)
    s2 = jnp.sum(lp[2:3] * lp[3:4], axis=-1, keepdims=True)
    return jnp.exp(s1) - jnp.exp(s2) + lam_init


def _diff_finish(o0, o1, lam, sg, lam_init):
    o = o0 - lam * o1
    return _rms(o) * sg * (1.0 - lam_init)


def _diff_prompt_kernel(q_ref, k_ref, vt_ref, dtab_ref, lp_ref, sg_ref, o_ref, m_sc, l_sc, acc_sc, *, lam_init, tkb):
    tq = TQ_DIFF
    i = pl.program_id(1)
    ksub = tkb // TK
    nbig = (i * tq + tq - 1) // tkb + 1
    scale = B_HEAD_DIM ** -0.5 * LOG2E
    _init_flash(m_sc, l_sc, acc_sc)

    def body(c, carry):
        off = pl.multiple_of(c * tkb, tkb)
        for h in range(B_HEADS):
            bias = _chunk_bias(dtab_ref, h, i, c, ksub, tq)
            vt = vt_ref[0, c, h * B_VH:(h + 1) * B_VH, :].astype(BF16)
            for comp in range(2):
                st = h * 2 + comp
                c0 = st * B_HEAD_DIM
                qh = (q_ref[0, :, c0:c0 + B_HEAD_DIM] * scale).astype(BF16)
                kc = k_ref[0, pl.ds(off, tkb), c0:c0 + B_HEAD_DIM].astype(BF16)
                _flash_update_t(st, _dot_nt(kc, qh) + bias, vt, m_sc, l_sc, acc_sc)
        return carry

    lax.fori_loop(0, nbig, body, 0)
    lam = _diff_lambda(lp_ref, lam_init)
    for h in range(B_HEADS):
        o0 = (acc_sc[2 * h] / l_sc[2 * h]).T
        o1 = (acc_sc[2 * h + 1] / l_sc[2 * h + 1]).T
        o_ref[0, :, h * B_VH:(h + 1) * B_VH] = _diff_finish(o0, o1, lam, sg_ref[...], lam_init)


def _diff_prompt(qb, kb, vbt, dtab, lam_p, subln, lam_init, tkb):
    b, s, _ = qb.shape
    tq = TQ_DIFF
    nq = s // tq
    nc = s // tkb
    blk = lambda w: pl.BlockSpec((1, tq, w), lambda bi, i: (bi, i, 0))
    full = lambda w: pl.BlockSpec((1, s, w), lambda bi, i: (bi, 0, 0))
    return pl.pallas_call(
        functools.partial(_diff_prompt_kernel, lam_init=lam_init, tkb=tkb),
        out_shape=jax.ShapeDtypeStruct((b, s, B_V), F32),
        grid=(b, nq),
        in_specs=[blk(B_QK), full(B_QK), pl.BlockSpec((1, nc, B_V, tkb), lambda bi, i: (bi, 0, 0, 0)),
                  _const_spec(dtab.shape), _const_spec(lam_p.shape), _const_spec(subln.shape)],
        out_specs=blk(B_V),
        scratch_shapes=_flash_scratch_t(2 * B_HEADS, B_VH, tq),
        compiler_params=_cparams(("parallel", "arbitrary")),
        name="diff_prompt",
    )(qb, kb, vbt, dtab, lam_p, subln)


def _page_specs(pps, rows, cols):
    return [pl.BlockSpec((1, rows, cols), functools.partial(lambda b, s, pt, j: (pt[b, s * pps + j], 0, 0), j=j))
            for j in range(pps)]


def _all_page_specs(npages, rows, cols):
    return [pl.BlockSpec((1, rows, cols), functools.partial(lambda b, pt, j: (pt[b, j], 0, 0), j=j))
            for j in range(npages)]


def _dsa_sample_select_kernel(pt_ref, q_ref, wi_ref, kinew_ref, *rest, pps, npages, topk, group):
    del pt_ref
    page_refs = rest[:pps]
    tri_ref, selb_ref, sc_ref = rest[pps:]
    b = pl.program_id(0)
    s = pl.program_id(1)
    np1 = npages + 1
    q = q_ref[0].astype(BF16)
    wi = wi_ref[0] * (IDX_HEADS ** -0.5)

    def scores(kt):
        d = _dot(q, kt) * (IDX_DIM ** -0.5)
        return jnp.sum((jnp.maximum(d, 0.0) * wi).reshape(IDX_HEADS, T_PAD, TK), axis=0)

    base = b * np1
    for j in range(pps):
        sc_ref[base + s * pps + j] = scores(page_refs[j][0].astype(BF16))

    last_page_step = s == pl.num_programs(1) - 1

    @pl.when(last_page_step)
    def _():
        row = lax.broadcasted_iota(I32, (T_PAD, TK), 0)
        col = lax.broadcasted_iota(I32, (T_PAD, TK), 1)
        sc_ref[base + npages] = jnp.where(col <= row, scores(kinew_ref[0].astype(BF16)), -jnp.inf)

    @pl.when(last_page_step & (b == pl.num_programs(0) - 1))
    def _():
        def group_body(gi, carry):
            sk = sc_ref[pl.ds(gi * (group * np1), group * np1)].reshape(group, np1, T_PAD, TK)

            def count(pred):
                cnt = jnp.sum(jnp.where(pred, 1.0, 0.0), axis=1)
                return jnp.sum(cnt, axis=-1, keepdims=True)

            thr = _kth_largest(lambda t: count(sk >= t[:, None]), (group, T_PAD, 1), topk)
            need = float(topk) - count(sk > thr[:, None])
            eqf = jnp.where(sk == thr[:, None], 1.0, 0.0)
            before = _dot(eqf.reshape(group * np1 * T_PAD, TK).astype(BF16), tri_ref[...]).reshape(sk.shape)
            ties = jnp.sum(eqf, axis=-1, keepdims=True)
            off = jnp.zeros((group, T_PAD, 1), F32)
            for c in range(np1):
                selb_ref[pl.ds(gi * group, group), c] = _select_mask(sk[:, c], thr, need, before[:, c] + off)
                off = off + ties[:, c]
            return carry

        lax.fori_loop(0, pl.num_programs(0) // group, group_body, 0)


def _dsa_sample_select(page_table, q_rows, wi_rows, ki_new_t, cache_ikt, n_new, pps):
    db, npages = page_table.shape
    np1 = npages + 1
    topk = min(TOPK_MAX, (npages * TK + n_new) // 4)
    group = math.gcd(db, SUBLANES)
    tri = (jnp.arange(TK)[:, None] < jnp.arange(TK)[None, :]).astype(BF16)
    per_b = lambda shape: pl.BlockSpec((1,) + shape, lambda b, s, pt: (b,) + (0,) * len(shape))
    grid_spec = pltpu.PrefetchScalarGridSpec(
        num_scalar_prefetch=1,
        grid=(db, npages // pps),
        in_specs=[per_b(q_rows.shape[1:]), per_b(wi_rows.shape[1:]), per_b((IDX_DIM, TK))]
        + _page_specs(pps, IDX_DIM, TK) + [pl.BlockSpec(tri.shape, lambda b, s, pt: (0, 0))],
        out_specs=pl.BlockSpec((db, np1, T_PAD, TK), lambda b, s, pt: (0, 0, 0, 0)),
        scratch_shapes=[pltpu.VMEM((db * np1, T_PAD, TK), F32)],
    )
    return pl.pallas_call(
        functools.partial(_dsa_sample_select_kernel, pps=pps, npages=npages, topk=topk, group=group),
        out_shape=jax.ShapeDtypeStruct((db, np1, T_PAD, TK), F32),
        grid_spec=grid_spec,
        compiler_params=_cparams(("arbitrary", "arbitrary")),
        name="dsa_sample_select",
    )(page_table, q_rows, wi_rows, ki_new_t, *([cache_ikt] * pps), tri)


def _page_bias(btab_ref, c0, n, npages):
    return jnp.concatenate([btab_ref[jnp.where(c0 + j == npages - 1, 1, 0)] for j in range(n)], axis=1)


def _dsa_sample_attn_kernel(pt_ref, q_ref, knew_ref, vnew_ref, selb_ref, btab_ref, *rest, npages):
    del pt_ref
    o_ref = rest[2 * npages]
    kts = [r[0] for r in rest[:npages]] + [knew_ref[0]]
    vts = [r[0] for r in rest[npages:2 * npages]] + [vnew_ref[0]]
    n = npages + 1
    rows = A_HEADS * T_PAD
    scale = A_HEAD_DIM ** -0.5
    q = (q_ref[0] * scale).astype(BF16)
    sc = jnp.concatenate([_dot(q, kt.astype(BF16)) for kt in kts], axis=1)
    bias = jnp.concatenate([btab_ref[0]] * (npages - 1) + [btab_ref[1], btab_ref[2]], axis=1)
    mask = jnp.concatenate([selb_ref[0, c] for c in range(n)], axis=1)
    sc = ((sc + bias).reshape(A_HEADS, T_PAD, n * TK) + mask[None]).reshape(rows, n * TK)
    p = jnp.exp(sc - jnp.max(sc, axis=-1, keepdims=True))
    l = jnp.sum(p, axis=-1, keepdims=True)
    p = p.astype(BF16)
    acc = _dot_nt(p[:, 0:TK], vts[0].astype(BF16))
    for j in range(1, n):
        acc = acc + _dot_nt(p[:, j * TK:(j + 1) * TK], vts[j].astype(BF16))
    o = acc / l
    for h in range(A_HEADS):
        g0 = (h // A_REP) * A_HEAD_DIM
        o_ref[0, :, h * A_HEAD_DIM:(h + 1) * A_HEAD_DIM] = o[h * T_PAD:(h + 1) * T_PAD, g0:g0 + A_HEAD_DIM]


def _dsa_sample_attn(page_table, q_bd, k_new_t, v_new_t, selb, btab, cache_kt, cache_vt):
    db, npages = page_table.shape
    rows = A_HEADS * T_PAD
    per_b = lambda shape: pl.BlockSpec((1,) + shape, lambda b, pt: (b,) + (0,) * len(shape))
    grid_spec = pltpu.PrefetchScalarGridSpec(
        num_scalar_prefetch=1,
        grid=(db,),
        in_specs=[per_b((rows, A_KV)), per_b((A_KV, TK)), per_b((A_KV, TK)), per_b((npages + 1, T_PAD, TK)),
                  pl.BlockSpec(btab.shape, lambda b, pt: (0, 0, 0))]
        + _all_page_specs(npages, A_KV, TK) + _all_page_specs(npages, A_KV, TK),
        out_specs=per_b((T_PAD, A_Q)),
    )
    return pl.pallas_call(
        functools.partial(_dsa_sample_attn_kernel, npages=npages),
        out_shape=jax.ShapeDtypeStruct((db, T_PAD, A_Q), F32),
        grid_spec=grid_spec,
        compiler_params=_cparams(("parallel",)),
        name="dsa_sample_attn",
    )(page_table, q_bd, k_new_t, v_new_t, selb, btab, *([cache_kt] * npages), *([cache_vt] * npages))


def _diff_sample_kernel(pt_ref, q_ref, knew_ref, vnew_ref, btab_ref, lp_ref, sg_ref, *rest, pps, npages, lam_init):
    del pt_ref
    k_refs = rest[:pps]
    v_refs = rest[pps:2 * pps]
    o_ref, m_sc, l_sc, acc_sc = rest[2 * pps:]
    s = pl.program_id(1)

    @pl.when(s == 0)
    def _():
        _init_flash(m_sc, l_sc, acc_sc)

    scale = B_HEAD_DIM ** -0.5
    q = (q_ref[0] * scale).astype(BF16)
    hrows = 2 * T_PAD

    def step(kt_list, v_fn, bias):
        n = len(kt_list)
        sc = jnp.concatenate([_dot(q, kt.astype(BF16)) for kt in kt_list], axis=1) + bias

        def pv(p):
            outs = []
            for h in range(B_HEADS):
                ph = p[h * hrows:(h + 1) * hrows]
                out = _dot(ph[:, 0:TK], v_fn(0, h))
                for j in range(1, n):
                    out = out + _dot(ph[:, j * TK:(j + 1) * TK], v_fn(j, h))
                outs.append(out)
            return jnp.concatenate(outs, axis=0)

        _flash_update(0, sc, pv, m_sc, l_sc, acc_sc)

    def page_v(j, h):
        return v_refs[j][0, pl.ds(h, TK, stride=B_HEADS), :].astype(BF16)

    step([r[0] for r in k_refs], page_v, _page_bias(btab_ref, s * pps, pps, npages))

    @pl.when(s == pl.num_programs(1) - 1)
    def _():
        step([knew_ref[0]], lambda j, h: vnew_ref[0, :, h * B_VH:(h + 1) * B_VH].astype(BF16), btab_ref[2])
        o = acc_sc[0] / l_sc[0]
        lam = _diff_lambda(lp_ref, lam_init)
        for h in range(B_HEADS):
            r0 = h * hrows
            o_ref[0, :, h * B_VH:(h + 1) * B_VH] = _diff_finish(
                o[r0:r0 + T_PAD], o[r0 + T_PAD:r0 + hrows], lam, sg_ref[...], lam_init)


def _diff_sample(page_table, q_bd, k_new_t, v_new, btab, lam_p, subln, cache_kt, cache_v2, lam_init, pps):
    db, npages = page_table.shape
    rows = 2 * B_HEADS * T_PAD
    per_b = lambda shape: pl.BlockSpec((1,) + shape, lambda b, s, pt: (b,) + (0,) * len(shape))
    const = lambda shape: pl.BlockSpec(shape, lambda b, s, pt: (0,) * len(shape))
    grid_spec = pltpu.PrefetchScalarGridSpec(
        num_scalar_prefetch=1,
        grid=(db, npages // pps),
        in_specs=[per_b((rows, B_QK)), per_b((B_QK, TK)), per_b((TK, B_V)), const(btab.shape),
                  const(lam_p.shape), const(subln.shape)]
        + _page_specs(pps, B_QK, TK) + _page_specs(pps, TK * B_HEADS, B_VH),
        out_specs=per_b((T_PAD, B_V)),
        scratch_shapes=_flash_scratch(1, rows, B_VH),
    )
    return pl.pallas_call(
        functools.partial(_diff_sample_kernel, pps=pps, npages=npages, lam_init=lam_init),
        out_shape=jax.ShapeDtypeStruct((db, T_PAD, B_V), F32),
        grid_spec=grid_spec,
        compiler_params=_cparams(("parallel", "arbitrary")),
        name="diff_sample",
    )(page_table, q_bd, k_new_t, v_new, btab, lam_p, subln, *([cache_kt] * pps), *([cache_v2] * pps))


def _mid_kernel(x_ref, ma_ref, mb_ref, mk_ref, mv_ref, wout_ref, gx_ref, wq_ref, gm_ref, qg_ref, wo_ref, h2_ref):
    h = (x_ref[0] + _dot(ma_ref[0].astype(BF16), wout_ref[:A_Q, :])
         + _dot(mb_ref[0].astype(BF16), wout_ref[A_Q:, :]))
    hn = _rms(h) * gx_ref[...]
    q = _dot(hn.astype(BF16), wq_ref[...])
    ms = _dot((q * q).astype(BF16), gm_ref[...])
    q = (q * lax.rsqrt(ms + EPS) * qg_ref[...]).astype(BF16)
    m_tok = mk_ref.shape[1] // MEM_HEADS
    outs = []
    for hh in range(MEM_HEADS):
        sl = slice(hh * MEM_HEAD_DIM, (hh + 1) * MEM_HEAD_DIM)
        mk = mk_ref[0, pl.ds(hh, m_tok, stride=MEM_HEADS), :].astype(BF16)
        mv = mv_ref[0, pl.ds(hh, m_tok, stride=MEM_HEADS), :].astype(BF16)
        s = _dot_nt(q[:, sl], mk) * (MEM_HEAD_DIM ** -0.5)
        p = jnp.exp(s - jnp.max(s, axis=-1, keepdims=True))
        l = jnp.sum(p, axis=-1, keepdims=True)
        outs.append(_dot(p.astype(BF16), mv) / l)
    o = jnp.concatenate(outs, axis=-1)
    h2_ref[0] = h + _dot(o.astype(BF16), wo_ref[...])


def _mid(x, mix_a, mix_b, mk, mv, w_out, g_x, w_q, gmat, q_gain, w_o, tm):
    b, s, d = x.shape
    blk = lambda w: pl.BlockSpec((1, tm, w), lambda bi, i: (bi, i, 0))
    per_b = lambda w: pl.BlockSpec((1, mk.shape[1], w), lambda bi, i: (bi, 0, 0))
    return pl.pallas_call(
        _mid_kernel,
        out_shape=jax.ShapeDtypeStruct((b, s, d), F32),
        grid=(b, s // tm),
        in_specs=[blk(d), blk(A_Q), blk(B_V), per_b(MEM_HEAD_DIM), per_b(MEM_HEAD_DIM), _const_spec(w_out.shape),
                  _const_spec(g_x.shape), _const_spec(w_q.shape), _const_spec(gmat.shape),
                  _const_spec(q_gain.shape), _const_spec(w_o.shape)],
        out_specs=blk(d),
        compiler_params=_cparams(("parallel", "arbitrary")),
        name="mid",
    )(x, mix_a, mix_b, mk, mv, w_out, g_x, w_q, gmat, q_gain, w_o)


def _ffn_core(h, gn, wg_ref, wu_ref, cw_ref, cb_ref, wd_ref, shifted):
    xb = (_rms(h) * gn).astype(BF16)
    g = _dot(xb, wg_ref[...])
    u = _dot(xb, wu_ref[...])
    gm1, gm2 = shifted(g)
    gc = cb_ref[...] + cw_ref[0:1, :] * gm2 + cw_ref[1:2, :] * gm1 + cw_ref[2:3, :] * g
    a = gc / (1.0 + jnp.exp(-gc)) * u
    return h + _dot(a.astype(BF16), wd_ref[...]), g


def _ffn_prompt_kernel(h_ref, gn_ref, wg_ref, wu_ref, cw_ref, cb_ref, wd_ref, y_ref, tail_ref, carry_ref):
    @pl.when(pl.program_id(1) == 0)
    def _():
        carry_ref[...] = jnp.zeros(carry_ref.shape, F32)

    tm = h_ref.shape[1]
    row = lax.broadcasted_iota(I32, (tm, 1), 0)
    c0 = carry_ref[SUBLANES - 2:SUBLANES - 1, :]
    c1 = carry_ref[SUBLANES - 1:SUBLANES, :]

    def shifted(g):
        gm1 = jnp.where(row == 0, c1, pltpu.roll(g, 1, 0))
        gm2 = jnp.where(row == 0, c0, jnp.where(row == 1, c1, pltpu.roll(g, 2, 0)))
        return gm1, gm2

    y, g = _ffn_core(h_ref[0], gn_ref[...], wg_ref, wu_ref, cw_ref, cb_ref, wd_ref, shifted)
    y_ref[0] = y
    tail = g[tm - SUBLANES:, :]
    carry_ref[...] = tail
    tail_ref[0] = tail


def _ffn_prompt(h, gn, wg, wu, cw, cb, wd, tm):
    b, s, d = h.shape
    f = wg.shape[1]
    blk = pl.BlockSpec((1, tm, d), lambda bi, i: (bi, i, 0))
    wspec = lambda shape: _const_spec(shape, single_buffer=True)
    return pl.pallas_call(
        _ffn_prompt_kernel,
        out_shape=[jax.ShapeDtypeStruct((b, s, d), F32), jax.ShapeDtypeStruct((b, SUBLANES, f), F32)],
        grid=(b, s // tm),
        in_specs=[blk, _const_spec(gn.shape), wspec(wg.shape), wspec(wu.shape), _const_spec(cw.shape),
                  _const_spec(cb.shape), wspec(wd.shape)],
        out_specs=[blk, pl.BlockSpec((1, SUBLANES, f), lambda bi, i: (bi, 0, 0))],
        scratch_shapes=[pltpu.VMEM((SUBLANES, f), F32)],
        compiler_params=_cparams(("arbitrary", "arbitrary")),
        name="ffn_prompt",
    )(h, gn, wg, wu, cw, cb, wd)


def _ffn_sample_kernel(h_ref, gn_ref, wg_ref, wu_ref, cw_ref, cb_ref, wd_ref, st1_ref, st2_ref, y_ref, g_ref):
    m = h_ref.shape[0]
    t = lax.broadcasted_iota(I32, (m, 1), 0) & (T_PAD - 1)

    def shifted(g):
        gm1 = jnp.where(t == 0, st1_ref[...], pltpu.roll(g, 1, 0))
        gm2 = jnp.where(t < 2, st2_ref[...], pltpu.roll(g, 2, 0))
        return gm1, gm2

    y, g = _ffn_core(h_ref[...], gn_ref[...], wg_ref, wu_ref, cw_ref, cb_ref, wd_ref, shifted)
    y_ref[...] = y
    g_ref[...] = g


def _ffn_sample(h2d, gn, wg, wu, cw, cb, wd, st1, st2):
    m, d = h2d.shape
    f = wg.shape[1]
    wspec = lambda shape: _const_spec(shape, single_buffer=True)
    return pl.pallas_call(
        _ffn_sample_kernel,
        out_shape=[jax.ShapeDtypeStruct((m, d), F32), jax.ShapeDtypeStruct((m, f), F32)],
        grid=(1,),
        in_specs=[_const_spec((m, d)), _const_spec(gn.shape), wspec(wg.shape), wspec(wu.shape),
                  _const_spec(cw.shape), _const_spec(cb.shape), wspec(wd.shape), _const_spec((m, f)),
                  _const_spec((m, f))],
        out_specs=[_const_spec((m, d)), _const_spec((m, f))],
        compiler_params=_cparams(("arbitrary",)),
        name="ffn_sample",
    )(h2d, gn, wg, wu, cw, cb, wd, st1, st2)


def _rel_bucket(dist):
    n = np.maximum(dist, 0)
    max_exact = NUM_BUCKETS // 2
    nf = np.maximum(n, 1).astype(np.float32)
    log_b = (np.log(nf / np.float32(max_exact)) / np.float32(math.log(MAX_DISTANCE / max_exact))
             * np.float32(NUM_BUCKETS - max_exact))
    large = np.minimum(max_exact + log_b.astype(np.int32), NUM_BUCKETS - 1)
    return np.where(n < max_exact, n, large)


def _bias_by_dist(dist, causal, bias):
    onehot = (_rel_bucket(dist)[..., None] == np.arange(NUM_BUCKETS)).astype(np.float32)
    vals = jnp.einsum("...k,kh->h...", onehot, bias, precision=lax.Precision.HIGHEST)
    return jnp.where(causal[None], vals, NEG).astype(F32)


def _prompt_bias_tables(bias):
    r = np.arange(TK)[:, None]
    c = np.arange(TK)[None, :]
    always = np.ones((TK, TK), bool)
    masked = _bias_by_dist(r - c, ~always, bias)
    t0 = _bias_by_dist(r - c, r >= c, bias)
    t1 = _bias_by_dist(r - c + TK, always, bias)
    t2 = _bias_by_dist(r - c + 2 * TK, always, bias)
    return jnp.swapaxes(jnp.stack([masked, t0, t1, t2], axis=1), -1, -2) * LOG2E


def _sample_bias_tables(bias, streams_per_head):
    t = np.arange(T_PAD)[:, None]
    c = np.arange(TK)[None, :]
    always = np.ones((T_PAD, TK), bool)
    far = _bias_by_dist(t - c + 2 * TK, always, bias)
    last = _bias_by_dist(t - c + TK, always, bias)
    new = _bias_by_dist(t - c, c <= t, bias)
    tabs = jnp.stack([far, last, new], axis=0)
    tabs = jnp.repeat(tabs[:, :, None], streams_per_head, axis=2)
    return tabs.reshape(3, -1, TK)


def _group_mean_matrix(width, group):
    idx = jnp.arange(width) // group
    return jnp.where(idx[:, None] == idx[None, :], 1.0 / group, 0.0).astype(BF16)


def _pad_rows(x, rows):
    return jnp.pad(x, ((0, 0), (0, rows - x.shape[1]), (0, 0)))


def _new_keys_t(x, rows):
    return jnp.swapaxes(_pad_rows(x, rows), 1, 2)


def kernel(x_prompt, x_sample, mem_prompt, cache_a_k, cache_a_v, cache_idx_k, cache_b_k, cache_b_v, cache_mem_k, cache_mem_v, state_ffn_conv, page_table, rel_bias, norm_mix, w_in, a_q_norm, a_k_norm, b_q_norm, b_k_norm, diff_lambda, diff_subln, w_out, norm_mem_x, norm_mem_src, w_mem_q, w_mem_kv, mem_q_norm, mem_k_norm, w_mem_o, norm_ffn, w_up, w_gate, ffn_conv_w, ffn_conv_b, w_down):
    depth = w_in.shape[0]
    assert depth == 1, "single-layer trunk"
    layer = 0
    lam_init = 0.8 - 0.6 * math.exp(-0.3 * layer)
    b, s, d = x_prompt.shape
    db, t_new, _ = x_sample.shape
    assert CONV_W - 1 <= t_new <= T_PAD and cache_a_k.shape[2] == TK
    m_tok = mem_prompt.shape[1]
    f = w_up.shape[-1]
    n_pool = cache_a_k.shape[1]
    tkb = min(TKB_MAX, s)
    assert s % tkb == 0 and tkb % TK == 0

    w_in_l = w_in[layer]
    n_front = A_Q + 2 * A_KV + IDX_Q + IDX_DIM + IDX_HEADS
    w_in_p = jnp.concatenate(
        [w_in_l[:, :n_front], jnp.zeros((d, C_QB - n_front), F32), w_in_l[:, n_front:]], axis=1).astype(BF16)
    ones = lambda n: jnp.ones((n,), F32)
    hgain = jnp.concatenate([
        jnp.tile(a_q_norm[layer], A_HEADS), jnp.tile(a_k_norm[layer], A_KV_HEADS), ones(C_QB - C_VA),
        jnp.tile(b_q_norm[layer], 2 * B_HEADS), jnp.tile(b_k_norm[layer], 2 * B_HEADS), ones(B_V)])[None, :]
    gmat64 = _group_mean_matrix(A_Q, A_HEAD_DIM)
    gmat128 = _group_mean_matrix(MEM_W, MEM_HEAD_DIM)
    proj_segs = ((C_QA, A_Q, True, A_Q, 1), (C_KA, A_KV, True, A_KV, 1), (C_VA, A_KV, False, A_KV, 1),
                 (C_QI, IDX_Q, False, IDX_Q, 1), (C_KIW, LANES, False, LANES, 1), (C_KIW, LANES, False, IDX_DIM, 1),
                 (C_QB, B_QK, True, B_QK, 1), (C_KB, B_QK, True, B_QK, 1), (C_VB, B_V, False, B_V, 1))
    prompt_segs = ((C_QA, A_Q, True, A_Q, 1), (C_KA, A_KV, True, A_KV, 1), (C_QI, IDX_Q, False, IDX_Q, 1),
                   (C_KIW, LANES, False, LANES, 1), (C_QB, B_QK, True, B_QK, 1), (C_KB, B_QK, True, B_QK, 1),
                   (C_VA, A_KV, False, A_KV, CHUNKED_T), (C_VB, B_V, False, B_V, CHUNKED_T),
                   (C_KA, A_KV, True, A_KV, FEATURE_MAJOR), (C_VA, A_KV, False, A_KV, FEATURE_MAJOR),
                   (C_KIW, LANES, False, IDX_DIM, FEATURE_MAJOR), (C_KB, B_QK, True, B_QK, FEATURE_MAJOR),
                   (C_VB, B_V, False, B_V, B_HEADS))
    g_mix = norm_mix[layer][None, :]
    w_out_b = w_out[layer].astype(BF16)
    w_q_b = w_mem_q[layer].astype(BF16)
    w_o_b = w_mem_o[layer].astype(BF16)
    w_kv_b = w_mem_kv[layer].astype(BF16)
    w_gate_b = w_gate[layer].astype(BF16)
    w_up_b = w_up[layer].astype(BF16)
    w_down_b = w_down[layer].astype(BF16)
    g_memx = norm_mem_x[layer][None, :]
    q_gain = jnp.tile(mem_q_norm[layer], MEM_HEADS)[None, :]
    kv_gain = jnp.concatenate([jnp.tile(mem_k_norm[layer], MEM_HEADS), ones(MEM_W)])[None, :]
    g_ffn = norm_ffn[layer][None, :]
    conv_w = ffn_conv_w[layer]
    conv_b = ffn_conv_b[layer][None, :]
    lam_p = diff_lambda[layer]
    subln = diff_subln[layer][None, :]
    bias_a = rel_bias[:, :A_HEADS]
    bias_b = rel_bias[:, A_HEADS:]

    assert tkb == TM_TOKENS, "the projection emits one value chunk per token tile"
    qa, ka, qi, kiw, qb, kb, va_ch, vb_ch, ka_t, va_t, ki_t, kb_t, vb_c = _norm_proj(
        x_prompt.reshape(b * s, d), g_mix, w_in_p, gmat64, hgain, prompt_segs, TM_TOKENS, "proj_prompt", seq=s)

    def from_feature_major(a_t, shape):
        nd = len(shape)
        return jnp.transpose(a_t.reshape((b,) + shape + (s,)), (0, nd + 1) + tuple(range(1, nd + 1)))[None]
    r3 = lambda a: a.reshape(b, s, a.shape[-1])
    mix_a = _dsa_prompt(r3(qi), r3(kiw), r3(qa), r3(ka), va_ch, _prompt_bias_tables(bias_a), tkb)
    mix_b = _diff_prompt(r3(qb), r3(kb), vb_ch, _prompt_bias_tables(bias_b), lam_p, subln, lam_init, tkb)
    mk, mv = _norm_proj(mem_prompt.reshape(b * m_tok, d), norm_mem_src[layer][None, :], w_kv_b, gmat128, kv_gain,
                        ((0, MEM_W, True, MEM_W, MEM_HEADS), (MEM_W, MEM_W, False, MEM_W, MEM_HEADS)), 256, "mem_kv")
    mem_rows = lambda a: a.reshape(-1, m_tok * MEM_HEADS, MEM_HEAD_DIM)
    h2 = _mid(x_prompt, mix_a, mix_b, mem_rows(mk), mem_rows(mv), w_out_b, g_memx, w_q_b, gmat128, q_gain, w_o_b,
              TM_TOKENS)
    yp, tail = _ffn_prompt(h2, g_ffn, w_gate_b, w_up_b, conv_w, conv_b, w_down_b, TM_TOKENS)
    conv_p = tail[:, SUBLANES - (CONV_W - 1):, :]

    xs = _pad_rows(x_sample, T_PAD)
    qa, ka_s, va_s, qi, kiw, ki_s, qb, kb_s, vb_s = _norm_proj(
        xs.reshape(db * T_PAD, d), g_mix, w_in_p, gmat64, hgain, proj_segs, db * T_PAD, "proj_sample")
    r3 = lambda a: a.reshape(db, T_PAD, a.shape[-1])
    idx_kt = jnp.transpose(cache_idx_k[layer], (0, 2, 1))
    a_kt = jnp.transpose(cache_a_k[layer], (0, 2, 3, 1)).reshape(n_pool, A_KV, TK)
    a_vt = jnp.transpose(cache_a_v[layer], (0, 2, 3, 1)).reshape(n_pool, A_KV, TK)
    b_kt = jnp.transpose(cache_b_k[layer], (0, 2, 3, 4, 1)).reshape(n_pool, B_QK, TK)
    b_v2 = cache_b_v[layer].reshape(n_pool, TK * B_HEADS, B_VH)
    n_pages = page_table.shape[1]
    qi_rows = r3(qi).reshape(db, T_PAD, IDX_HEADS, IDX_DIM).transpose(0, 2, 1, 3).reshape(db, IDX_HEADS * T_PAD, IDX_DIM)
    wi_rows = r3(kiw)[:, :, IDX_DIM:IDX_DIM + IDX_HEADS].transpose(0, 2, 1).reshape(db, IDX_HEADS * T_PAD, 1)
    selb = _dsa_sample_select(page_table, qi_rows, wi_rows, _new_keys_t(r3(ki_s), TK), idx_kt, t_new,
                              math.gcd(n_pages, SELECT_PAGES_PER_STEP))
    q_rows = r3(qa).reshape(db, T_PAD, A_HEADS, A_HEAD_DIM).transpose(0, 2, 1, 3)
    eye_g = jnp.repeat(jnp.eye(A_KV_HEADS, dtype=F32), A_REP, axis=0)
    qa_bd = (q_rows[:, :, :, None, :] * eye_g[None, :, None, :, None]).reshape(db, A_HEADS * T_PAD, A_KV)
    mix_a = _dsa_sample_attn(page_table, qa_bd, _new_keys_t(r3(ka_s), TK), _new_keys_t(r3(va_s), TK), selb,
                             _sample_bias_tables(bias_a, 1), a_kt, a_vt)
    n_str = 2 * B_HEADS
    q_rows = r3(qb).reshape(db, T_PAD, n_str, B_HEAD_DIM).transpose(0, 2, 1, 3)
    qb_bd = (q_rows[:, :, :, None, :] * jnp.eye(n_str, dtype=F32)[None, :, None, :, None]).reshape(
        db, n_str * T_PAD, B_QK)
    mix_b = _diff_sample(page_table, qb_bd, _new_keys_t(r3(kb_s), TK), _pad_rows(r3(vb_s), TK),
                         _sample_bias_tables(bias_b, 2), lam_p, subln, b_kt, b_v2, lam_init,
                         math.gcd(n_pages, DIFF_PAGES_PER_STEP))
    h2 = _mid(xs, mix_a, mix_b, mem_rows(cache_mem_k[layer]), mem_rows(cache_mem_v[layer]), w_out_b, g_memx, w_q_b,
              gmat128, q_gain, w_o_b, T_PAD)
    state = state_ffn_conv[layer]
    st1 = _pad_rows(state[:, 1:2, :], T_PAD).reshape(db * T_PAD, f)
    st2 = _pad_rows(state, T_PAD).reshape(db * T_PAD, f)
    ys, g_s = _ffn_sample(h2.reshape(db * T_PAD, d), g_ffn, w_gate_b, w_up_b, conv_w, conv_b, w_down_b, st1, st2)
    ys = ys.reshape(db, T_PAD, d)[:, :t_new]
    conv_s = g_s.reshape(db, T_PAD, f)[:, t_new - (CONV_W - 1):t_new]

    def new_rows(a, shape):
        return a.reshape(db, T_PAD, -1)[:, :t_new].reshape((1, db, t_new) + shape)

    return (yp, ys,
            from_feature_major(ka_t, (A_KV_HEADS, A_HEAD_DIM)), from_feature_major(va_t, (A_KV_HEADS, A_HEAD_DIM)),
            from_feature_major(ki_t, (IDX_DIM,)), from_feature_major(kb_t, (B_HEADS, 2, B_HEAD_DIM)),
            vb_c.reshape(1, b, s, B_HEADS, 2 * B_HEAD_DIM),
            mk.reshape(1, b, m_tok, MEM_HEADS, MEM_HEAD_DIM), mv.reshape(1, b, m_tok, MEM_HEADS, MEM_HEAD_DIM),
            conv_p[None],
            new_rows(ka_s, (A_KV_HEADS, A_HEAD_DIM)), new_rows(va_s, (A_KV_HEADS, A_HEAD_DIM)),
            new_rows(ki_s, (IDX_DIM,)), new_rows(kb_s, (B_HEADS, 2, B_HEAD_DIM)),
            new_rows(vb_s, (B_HEADS, 2 * B_HEAD_DIM)), conv_s[None])
```

```python
import functools
import math

import jax
import jax.numpy as jnp
import numpy as np
from jax import lax
from jax.experimental import pallas as pl
from jax.experimental.pallas import tpu as pltpu

F32 = jnp.float32
BF16 = jnp.bfloat16
I32 = jnp.int32

EPS = 1e-6
NEG = -1e30
LOG2E = math.log2(math.e)
INT_MIN = -(2 ** 31)

A_HEADS = 8
A_KV_HEADS = 2
A_HEAD_DIM = 64
IDX_HEADS = 4
IDX_DIM = 64
TOPK_MAX = 256
B_HEADS = 4
B_HEAD_DIM = 64
MEM_HEADS = 4
MEM_HEAD_DIM = 128
CONV_W = 3
NUM_BUCKETS = 32
MAX_DISTANCE = 128

A_Q = A_HEADS * A_HEAD_DIM
A_KV = A_KV_HEADS * A_HEAD_DIM
IDX_Q = IDX_HEADS * IDX_DIM
B_QK = B_HEADS * 2 * B_HEAD_DIM
B_V = B_HEADS * 2 * B_HEAD_DIM
B_VH = 2 * B_HEAD_DIM
MEM_W = MEM_HEADS * MEM_HEAD_DIM
A_REP = A_HEADS // A_KV_HEADS

LANES = 128
SUBLANES = 8
TQ_DSA = 512
TQ_DIFF = 512
TK = 128
TKB_MAX = 512
T_PAD = SUBLANES
TM_TOKENS = 512
FEATURE_MAJOR = 0
TKB_DIFF = 512
SELECT_PAGES_PER_STEP = 32
DIFF_PAGES_PER_STEP = 16
VMEM_LIMIT = 56 * 1024 * 1024

C_QA = 0
C_KA = C_QA + A_Q
C_VA = C_KA + A_KV
C_QI = C_VA + A_KV
C_KIW = C_QI + IDX_Q
C_QB = C_KIW + LANES
C_KB = C_QB + B_QK
C_VB = C_KB + B_QK
D_IN_PAD = C_VB + B_V


def _dot(a, b):
    return jnp.dot(a, b, preferred_element_type=F32)


def _dot_nt(a, b):
    return lax.dot_general(a, b, (((1,), (1,)), ((), ())), preferred_element_type=F32)


def _rms(x):
    return x * lax.rsqrt(jnp.mean(x * x, axis=-1, keepdims=True) + EPS)


KEY_NEG_INF = -(2 ** 31) + 0x7FFFFF


def _key_to_float(key):
    bits = key ^ ((key >> 31) & 0x7FFFFFFF)
    return jnp.where(key <= KEY_NEG_INF, -jnp.inf, lax.bitcast_convert_type(bits, F32))


def _kth_largest(count_ge, shape, topk):
    def bit_body(it, key):
        cand = key ^ lax.shift_left(jnp.int32(1), 31 - it)
        return jnp.where(count_ge(_key_to_float(cand)) >= topk, cand, key)

    return _key_to_float(lax.fori_loop(0, 32, bit_body, jnp.full(shape, INT_MIN, I32)))


def _cparams(sem, vmem=VMEM_LIMIT):
    return pltpu.CompilerParams(dimension_semantics=sem, vmem_limit_bytes=vmem)


def _const_spec(shape, single_buffer=False):
    nd = len(shape)
    if single_buffer:
        return pl.BlockSpec(shape, lambda *_: (0,) * nd, pipeline_mode=pl.Buffered(1))
    return pl.BlockSpec(shape, lambda *_: (0,) * nd)


def _norm_proj_kernel(x_ref, g_ref, w_ref, gm_ref, hg_ref, *out_refs, segs):
    xn = _rms(x_ref[...]) * g_ref[...]
    p = _dot(xn.astype(BF16), w_ref[...])
    tm = x_ref.shape[0]
    done = {}
    for (start, width, norm, out_width, split), o_ref in zip(segs, out_refs):
        if (start, width, norm) not in done:
            s = p[:, start:start + width]
            if norm:
                ms = _dot((s * s).astype(BF16), gm_ref[:width, :width])
                s = s * lax.rsqrt(ms + EPS) * hg_ref[:, start:start + width]
            done[(start, width, norm)] = s
        s = done[(start, width, norm)]
        if split == 1:
            o_ref[...] = s[:, :out_width]
        elif split == FEATURE_MAJOR:
            o_ref[0] = s.T[:out_width, :]
        elif split < 0:
            st = s.T
            for u in range(tm // -split):
                o_ref[0, u] = st[:out_width, u * -split:(u + 1) * -split]
        else:
            pw = out_width // split
            for j in range(split):
                o_ref[pl.ds(j, tm, stride=split), :] = s[:, j * pw:(j + 1) * pw]


def _norm_proj(x2d, gain, w, gmat, hgain, segs, tm, name, seq=None):
    m, d = x2d.shape
    n = w.shape[1]
    tiles = None if seq is None else seq // tm

    def shape_spec(ow, sp):
        if sp == FEATURE_MAJOR:
            return (jax.ShapeDtypeStruct((m // seq, ow, seq), F32),
                    pl.BlockSpec((1, ow, tm), lambda i: (i // tiles, 0, i % tiles)))
        if sp < 0:
            return (jax.ShapeDtypeStruct((m // seq, seq // -sp, ow, -sp), F32),
                    pl.BlockSpec((1, tm // -sp, ow, -sp), lambda i: (i // tiles, i % tiles, 0, 0)))
        return (jax.ShapeDtypeStruct((m * sp, ow // sp), F32), pl.BlockSpec((tm * sp, ow // sp), lambda i: (i, 0)))

    shapes, specs = zip(*[shape_spec(ow, sp) for (_, _, _, ow, sp) in segs])
    return pl.pallas_call(
        functools.partial(_norm_proj_kernel, segs=segs),
        out_shape=list(shapes),
        grid=(m // tm,),
        in_specs=[pl.BlockSpec((tm, d), lambda i: (i, 0)), _const_spec((1, d)), _const_spec((d, n)),
                  _const_spec(gmat.shape), _const_spec((1, n))],
        out_specs=list(specs),
        compiler_params=_cparams(("parallel",)),
        name=name,
    )(x2d, gain, w, gmat, hgain)


def _init_flash(m_sc, l_sc, acc_sc):
    m_sc[...] = jnp.full(m_sc.shape, -jnp.inf, F32)
    l_sc[...] = jnp.zeros(l_sc.shape, F32)
    acc_sc[...] = jnp.zeros(acc_sc.shape, F32)


def _flash_update(st, s, pv_fn, m_sc, l_sc, acc_sc):
    m_old = m_sc[st]
    m_new = jnp.maximum(m_old, jnp.max(s, axis=-1, keepdims=True))
    alpha = jnp.exp(m_old - m_new)
    p = jnp.exp(s - m_new)
    l_sc[st] = alpha * l_sc[st] + jnp.sum(p, axis=-1, keepdims=True)
    acc_sc[st] = alpha * acc_sc[st] + pv_fn(p.astype(BF16))
    m_sc[st] = m_new


def _flash_scratch(streams, rows, dv):
    return [pltpu.VMEM((streams, rows, 1), F32), pltpu.VMEM((streams, rows, 1), F32),
            pltpu.VMEM((streams, rows, dv), F32)]


def _select_mask(sk, thr, need, before):
    need = jnp.where(thr > -jnp.inf, need, 0.0)
    return jnp.where(sk > thr, 0.0, jnp.where(sk == thr, jnp.where(before < need, 0.0, NEG), NEG))


def _chunk_bias(dtab_ref, h, i, c, ksub, tq):
    qsub = tq // TK
    return jnp.concatenate(
        [jnp.concatenate([dtab_ref[h, jnp.clip(i * qsub + a - (c * ksub + j) + 1, 0, 3)] for a in range(qsub)], axis=1)
         for j in range(ksub)], axis=0)


def _flash_update_t(st, s, vt, m_sc, l_sc, acc_sc):
    m_old = m_sc[st]
    m_new = jnp.maximum(m_old, jnp.max(_col_reduce(s, jnp.max), axis=0, keepdims=True))
    alpha = jnp.exp2(m_old - m_new)
    p = jnp.exp2(s - m_new)
    l_sc[st] = alpha * l_sc[st] + jnp.sum(_col_reduce(p, jnp.sum), axis=0, keepdims=True)
    acc_sc[st] = alpha * acc_sc[st] + _dot(vt, p.astype(BF16))
    m_sc[st] = m_new


def _flash_scratch_t(streams, dv, cols):
    return [pltpu.VMEM((streams, 1, cols), F32), pltpu.VMEM((streams, 1, cols), F32),
            pltpu.VMEM((streams, dv, cols), F32)]


REDUCE_WAYS = 8


def _col_reduce(x, op):
    rows, cols = x.shape
    slabs = rows // SUBLANES
    if slabs % REDUCE_WAYS == 0 and slabs > REDUCE_WAYS:
        x = op(x.reshape(REDUCE_WAYS, slabs // REDUCE_WAYS, SUBLANES, cols), axis=1)
    else:
        x = x.reshape(slabs, SUBLANES, cols)
    return op(x, axis=0)


def _col_count(w):
    return _col_reduce(w, jnp.sum)


def _dsa_prompt_kernel(qi_ref, wit_ref, kiw_ref, qa_ref, ka_ref, vat_ref, dtab_ref, tril_ref, o_ref,
                       skey_ref, selb_ref, m_sc, l_sc, acc_sc, *, topk, tkb):
    tq = TQ_DSA
    i = pl.program_id(1)
    ksub = tkb // TK
    nbig = (i * tq + tq - 1) // tkb + 1
    qi = qi_ref[0].astype(BF16)
    wit = wit_ref[0] * (IDX_HEADS ** -0.5 * IDX_DIM ** -0.5)
    krow = lax.broadcasted_iota(I32, (tkb, tq), 0)
    qcol = lax.broadcasted_iota(I32, (tkb, tq), 1)

    def score_body(c, carry):
        off = pl.multiple_of(c * tkb, tkb)
        kc = kiw_ref[0, pl.ds(off, tkb), 0:IDX_DIM].astype(BF16)
        sc = jnp.zeros((tkb, tq), F32)
        for h in range(IDX_HEADS):
            d = _dot_nt(kc, qi[:, h * IDX_DIM:(h + 1) * IDX_DIM])
            sc = sc + jnp.maximum(d, 0.0) * wit[h:h + 1, :]
        causal = (c * tkb + krow) <= (i * tq + qcol)
        skey_ref[c] = jnp.where(causal, sc, -jnp.inf)
        return carry

    lax.fori_loop(0, nbig, score_body, 0)

    def count(pred_fn):
        def body(c, cnt):
            return cnt + _col_count(jnp.where(pred_fn(skey_ref[c]), 1.0, 0.0))
        cnt = lax.fori_loop(0, nbig, body, jnp.zeros((SUBLANES, tq), F32))
        return jnp.sum(cnt, axis=0, keepdims=True)

    thr = _kth_largest(lambda t: count(lambda sk: sk >= t), (1, tq), topk)
    need = float(topk) - count(lambda sk: sk > thr)
    tril = tril_ref[...]

    def sel_body(c, off):
        sk = skey_ref[c]
        eqf = jnp.where(sk == thr, 1.0, 0.0)
        before = _dot(tril, eqf.astype(BF16)) + off
        selb_ref[c] = _select_mask(sk, thr, need, before)
        return off + jnp.sum(_col_count(eqf), axis=0, keepdims=True)

    lax.fori_loop(0, nbig, sel_body, jnp.zeros((1, tq), F32))

    _init_flash(m_sc, l_sc, acc_sc)
    scale = A_HEAD_DIM ** -0.5 * LOG2E
    qg = [jnp.concatenate([qa_ref[0, :, h * A_HEAD_DIM:(h + 1) * A_HEAD_DIM] * scale
                           for h in range(g * A_REP, (g + 1) * A_REP)], axis=0).astype(BF16)
          for g in range(A_KV_HEADS)]

    def att_body(c, carry):
        off = pl.multiple_of(c * tkb, tkb)
        maskb = selb_ref[c]
        for g in range(A_KV_HEADS):
            g0 = g * A_HEAD_DIM
            kc = ka_ref[0, pl.ds(off, tkb), g0:g0 + A_HEAD_DIM].astype(BF16)
            vt = vat_ref[0, c, g0:g0 + A_HEAD_DIM, :].astype(BF16)
            sg = _dot_nt(kc, qg[g])
            for r in range(A_REP):
                h = g * A_REP + r
                s = sg[:, r * tq:(r + 1) * tq] + (_chunk_bias(dtab_ref, h, i, c, ksub, tq) + maskb)
                _flash_update_t(h, s, vt, m_sc, l_sc, acc_sc)
        return carry

    lax.fori_loop(0, nbig, att_body, 0)
    for h in range(0, A_HEADS, 2):
        ot = jnp.concatenate([acc_sc[h] / l_sc[h], acc_sc[h + 1] / l_sc[h + 1]], axis=0)
        o_ref[0, :, h * A_HEAD_DIM:(h + 2) * A_HEAD_DIM] = ot.T


def _dsa_prompt(qi, kiw, qa, ka, vat, dtab, tkb):
    b, s, _ = qa.shape
    tq = TQ_DSA
    nq = s // tq
    nc = s // tkb
    topk = min(TOPK_MAX, s // 4)
    tril = (jnp.arange(tkb)[None, :] < jnp.arange(tkb)[:, None]).astype(BF16)
    wit = jnp.swapaxes(kiw[:, :, IDX_DIM:IDX_DIM + SUBLANES], 1, 2)
    blk = lambda w: pl.BlockSpec((1, tq, w), lambda bi, i: (bi, i, 0))
    full = lambda w: pl.BlockSpec((1, s, w), lambda bi, i: (bi, 0, 0))
    return pl.pallas_call(
        functools.partial(_dsa_prompt_kernel, topk=topk, tkb=tkb),
        out_shape=jax.ShapeDtypeStruct((b, s, A_Q), F32),
        grid=(b, nq),
        in_specs=[blk(IDX_Q), pl.BlockSpec((1, SUBLANES, tq), lambda bi, i: (bi, 0, i)), full(LANES), blk(A_Q),
                  full(A_KV), pl.BlockSpec((1, nc, A_KV, tkb), lambda bi, i: (bi, 0, 0, 0)),
                  _const_spec(dtab.shape), _const_spec(tril.shape)],
        out_specs=blk(A_Q),
        scratch_shapes=[pltpu.VMEM((nc, tkb, tq), F32), pltpu.VMEM((nc, tkb, tq), F32)]
        + _flash_scratch_t(A_HEADS, A_HEAD_DIM, tq),
        compiler_params=_cparams(("parallel", "arbitrary")),
        name="dsa_prompt",
    )(qi, wit, kiw, qa, ka, vat, dtab, tril)


def _diff_lambda(lp_ref, lam_init):
    lp = lp_ref[...]
    s1 = jnp.sum(lp[0:1] * lp[1:2], axis=-1, keepdims=True)
    s2 = jnp.sum(lp[2:3] * lp[3:4], axis=-1, keepdims=True)
    return jnp.exp(s1) - jnp.exp(s2) + lam_init


def _diff_finish(o0, o1, lam, sg, lam_init):
    o = o0 - lam * o1
    return _rms(o) * sg * (1.0 - lam_init)


def _diff_prompt_kernel(q_ref, k_ref, vt_ref, dtab_ref, lp_ref, sg_ref, o_ref, m_sc, l_sc, acc_sc, *, lam_init, tkb):
    tq = TQ_DIFF
    i = pl.program_id(1)
    ksub = tkb // TK
    nbig = (i * tq + tq - 1) // tkb + 1
    scale = B_HEAD_DIM ** -0.5 * LOG2E
    _init_flash(m_sc, l_sc, acc_sc)

    def body(c, carry):
        off = pl.multiple_of(c * tkb, tkb)
        for h in range(B_HEADS):
            bias = _chunk_bias(dtab_ref, h, i, c, ksub, tq)
            vt = vt_ref[0, c, h * B_VH:(h + 1) * B_VH, :].astype(BF16)
            for comp in range(2):
                st = h * 2 + comp
                c0 = st * B_HEAD_DIM
                qh = (q_ref[0, :, c0:c0 + B_HEAD_DIM] * scale).astype(BF16)
                kc = k_ref[0, pl.ds(off, tkb), c0:c0 + B_HEAD_DIM].astype(BF16)
                _flash_update_t(st, _dot_nt(kc, qh) + bias, vt, m_sc, l_sc, acc_sc)
        return carry

    lax.fori_loop(0, nbig, body, 0)
    lam = _diff_lambda(lp_ref, lam_init)
    for h in range(B_HEADS):
        o0 = (acc_sc[2 * h] / l_sc[2 * h]).T
        o1 = (acc_sc[2 * h + 1] / l_sc[2 * h + 1]).T
        o_ref[0, :, h * B_VH:(h + 1) * B_VH] = _diff_finish(o0, o1, lam, sg_ref[...], lam_init)


def _diff_prompt(qb, kb, vbt, dtab, lam_p, subln, lam_init, tkb):
    b, s, _ = qb.shape
    tq = TQ_DIFF
    nq = s // tq
    nc = s // tkb
    blk = lambda w: pl.BlockSpec((1, tq, w), lambda bi, i: (bi, i, 0))
    full = lambda w: pl.BlockSpec((1, s, w), lambda bi, i: (bi, 0, 0))
    return pl.pallas_call(
        functools.partial(_diff_prompt_kernel, lam_init=lam_init, tkb=tkb),
        out_shape=jax.ShapeDtypeStruct((b, s, B_V), F32),
        grid=(b, nq),
        in_specs=[blk(B_QK), full(B_QK), pl.BlockSpec((1, nc, B_V, tkb), lambda bi, i: (bi, 0, 0, 0)),
                  _const_spec(dtab.shape), _const_spec(lam_p.shape), _const_spec(subln.shape)],
        out_specs=blk(B_V),
        scratch_shapes=_flash_scratch_t(2 * B_HEADS, B_VH, tq),
        compiler_params=_cparams(("parallel", "arbitrary")),
        name="diff_prompt",
    )(qb, kb, vbt, dtab, lam_p, subln)


def _page_specs(pps, rows, cols):
    return [pl.BlockSpec((1, rows, cols), functools.partial(lambda b, s, pt, j: (pt[b, s * pps + j], 0, 0), j=j))
            for j in range(pps)]


def _all_page_specs(npages, rows, cols):
    return [pl.BlockSpec((1, rows, cols), functools.partial(lambda b, pt, j: (pt[b, j], 0, 0), j=j))
            for j in range(npages)]


def _dsa_sample_select_kernel(pt_ref, q_ref, wi_ref, kinew_ref, *rest, pps, npages, topk, group):
    del pt_ref
    page_refs = rest[:pps]
    tri_ref, selb_ref, sc_ref = rest[pps:]
    b = pl.program_id(0)
    s = pl.program_id(1)
    np1 = npages + 1
    q = q_ref[0].astype(BF16)
    wi = wi_ref[0] * (IDX_HEADS ** -0.5 * IDX_DIM ** -0.5)

    def scores(kt):
        return jnp.sum((jnp.maximum(_dot(q, kt), 0.0) * wi).reshape(IDX_HEADS, T_PAD, TK), axis=0)

    base = b * np1
    for j in range(pps):
        sc_ref[base + s * pps + j] = scores(page_refs[j][0].astype(BF16))

    last_page_step = s == pl.num_programs(1) - 1

    @pl.when(last_page_step)
    def _():
        row = lax.broadcasted_iota(I32, (T_PAD, TK), 0)
        col = lax.broadcasted_iota(I32, (T_PAD, TK), 1)
        sc_ref[base + npages] = jnp.where(col <= row, scores(kinew_ref[0].astype(BF16)), -jnp.inf)

    @pl.when(last_page_step & (b == pl.num_programs(0) - 1))
    def _():
        def group_body(gi, carry):
            sk = sc_ref[pl.ds(gi * (group * np1), group * np1)].reshape(group, np1, T_PAD, TK)

            def count(pred):
                cnt = jnp.sum(jnp.where(pred, 1.0, 0.0), axis=1)
                return jnp.sum(cnt, axis=-1, keepdims=True)

            thr = _kth_largest(lambda t: count(sk >= t[:, None]), (group, T_PAD, 1), topk)
            need = float(topk) - count(sk > thr[:, None])
            eqf = jnp.where(sk == thr[:, None], 1.0, 0.0)
            before = _dot(eqf.reshape(group * np1 * T_PAD, TK).astype(BF16), tri_ref[...]).reshape(sk.shape)
            ties = jnp.sum(eqf, axis=-1, keepdims=True)
            off = jnp.zeros((group, T_PAD, 1), F32)
            for c in range(np1):
                selb_ref[pl.ds(gi * group, group), c] = _select_mask(sk[:, c], thr, need, before[:, c] + off)
                off = off + ties[:, c]
            return carry

        lax.fori_loop(0, pl.num_programs(0) // group, group_body, 0)


def _dsa_sample_select(page_table, q_rows, wi_rows, ki_new_t, cache_ikt, n_new, pps):
    db, npages = page_table.shape
    np1 = npages + 1
    topk = min(TOPK_MAX, (npages * TK + n_new) // 4)
    group = math.gcd(db, SUBLANES)
    tri = (jnp.arange(TK)[:, None] < jnp.arange(TK)[None, :]).astype(BF16)
    per_b = lambda shape: pl.BlockSpec((1,) + shape, lambda b, s, pt: (b,) + (0,) * len(shape))
    grid_spec = pltpu.PrefetchScalarGridSpec(
        num_scalar_prefetch=1,
        grid=(db, npages // pps),
        in_specs=[per_b(q_rows.shape[1:]), per_b(wi_rows.shape[1:]), per_b((IDX_DIM, TK))]
        + _page_specs(pps, IDX_DIM, TK) + [pl.BlockSpec(tri.shape, lambda b, s, pt: (0, 0))],
        out_specs=pl.BlockSpec((db, np1, T_PAD, TK), lambda b, s, pt: (0, 0, 0, 0)),
        scratch_shapes=[pltpu.VMEM((db * np1, T_PAD, TK), F32)],
    )
    return pl.pallas_call(
        functools.partial(_dsa_sample_select_kernel, pps=pps, npages=npages, topk=topk, group=group),
        out_shape=jax.ShapeDtypeStruct((db, np1, T_PAD, TK), F32),
        grid_spec=grid_spec,
        compiler_params=_cparams(("arbitrary", "arbitrary")),
        name="dsa_sample_select",
    )(page_table, q_rows, wi_rows, ki_new_t, *([cache_ikt] * pps), tri)


def _page_bias(btab_ref, c0, n, npages):
    return jnp.concatenate([btab_ref[jnp.where(c0 + j == npages - 1, 1, 0)] for j in range(n)], axis=1)


def _dsa_sample_attn_kernel(pt_ref, q_ref, knew_ref, vnew_ref, selb_ref, btab_ref, *rest, npages):
    del pt_ref
    o_ref = rest[2 * npages]
    kts = [r[0] for r in rest[:npages]] + [knew_ref[0]]
    vts = [r[0] for r in rest[npages:2 * npages]] + [vnew_ref[0]]
    n = npages + 1
    rows = A_HEADS * T_PAD
    scale = A_HEAD_DIM ** -0.5
    q = (q_ref[0] * scale).astype(BF16)
    sc = jnp.concatenate([_dot(q, kt.astype(BF16)) for kt in kts], axis=1)
    bias = jnp.concatenate([btab_ref[0]] * (npages - 1) + [btab_ref[1], btab_ref[2]], axis=1)
    mask = jnp.concatenate([selb_ref[0, c] for c in range(n)], axis=1)
    sc = ((sc + bias).reshape(A_HEADS, T_PAD, n * TK) + mask[None]).reshape(rows, n * TK)
    p = jnp.exp(sc - jnp.max(sc, axis=-1, keepdims=True))
    l = jnp.sum(p, axis=-1, keepdims=True)
    p = p.astype(BF16)
    acc = _dot_nt(p[:, 0:TK], vts[0].astype(BF16))
    for j in range(1, n):
        acc = acc + _dot_nt(p[:, j * TK:(j + 1) * TK], vts[j].astype(BF16))
    o = acc / l
    for h in range(A_HEADS):
        g0 = (h // A_REP) * A_HEAD_DIM
        o_ref[0, :, h * A_HEAD_DIM:(h + 1) * A_HEAD_DIM] = o[h * T_PAD:(h + 1) * T_PAD, g0:g0 + A_HEAD_DIM]


def _dsa_sample_attn(page_table, q_bd, k_new_t, v_new_t, selb, btab, cache_kt, cache_vt):
    db, npages = page_table.shape
    rows = A_HEADS * T_PAD
    per_b = lambda shape: pl.BlockSpec((1,) + shape, lambda b, pt: (b,) + (0,) * len(shape))
    grid_spec = pltpu.PrefetchScalarGridSpec(
        num_scalar_prefetch=1,
        grid=(db,),
        in_specs=[per_b((rows, A_KV)), per_b((A_KV, TK)), per_b((A_KV, TK)), per_b((npages + 1, T_PAD, TK)),
                  pl.BlockSpec(btab.shape, lambda b, pt: (0, 0, 0))]
        + _all_page_specs(npages, A_KV, TK) + _all_page_specs(npages, A_KV, TK),
        out_specs=per_b((T_PAD, A_Q)),
    )
    return pl.pallas_call(
        functools.partial(_dsa_sample_attn_kernel, npages=npages),
        out_shape=jax.ShapeDtypeStruct((db, T_PAD, A_Q), F32),
        grid_spec=grid_spec,
        compiler_params=_cparams(("parallel",)),
        name="dsa_sample_attn",
    )(page_table, q_bd, k_new_t, v_new_t, selb, btab, *([cache_kt] * npages), *([cache_vt] * npages))


def _diff_sample_kernel(pt_ref, q_ref, knew_ref, vnew_ref, btab_ref, lp_ref, sg_ref, *rest, pps, npages, lam_init):
    del pt_ref
    k_refs = rest[:pps]
    v_refs = rest[pps:2 * pps]
    o_ref, m_sc, l_sc, acc_sc = rest[2 * pps:]
    s = pl.program_id(1)

    @pl.when(s == 0)
    def _():
        _init_flash(m_sc, l_sc, acc_sc)

    scale = B_HEAD_DIM ** -0.5
    q = (q_ref[0] * scale).astype(BF16)
    hrows = 2 * T_PAD

    def step(kt_list, v_fn, bias):
        n = len(kt_list)
        sc = jnp.concatenate([_dot(q, kt.astype(BF16)) for kt in kt_list], axis=1) + bias

        def pv(p):
            outs = []
            for h in range(B_HEADS):
                ph = p[h * hrows:(h + 1) * hrows]
                out = _dot(ph[:, 0:TK], v_fn(0, h))
                for j in range(1, n):
                    out = out + _dot(ph[:, j * TK:(j + 1) * TK], v_fn(j, h))
                outs.append(out)
            return jnp.concatenate(outs, axis=0)

        _flash_update(0, sc, pv, m_sc, l_sc, acc_sc)

    def page_v(j, h):
        return v_refs[j][0, pl.ds(h, TK, stride=B_HEADS), :].astype(BF16)

    step([r[0] for r in k_refs], page_v, _page_bias(btab_ref, s * pps, pps, npages))

    @pl.when(s == pl.num_programs(1) - 1)
    def _():
        step([knew_ref[0]], lambda j, h: vnew_ref[0, :, h * B_VH:(h + 1) * B_VH].astype(BF16), btab_ref[2])
        o = acc_sc[0] / l_sc[0]
        lam = _diff_lambda(lp_ref, lam_init)
        for h in range(B_HEADS):
            r0 = h * hrows
            o_ref[0, :, h * B_VH:(h + 1) * B_VH] = _diff_finish(
                o[r0:r0 + T_PAD], o[r0 + T_PAD:r0 + hrows], lam, sg_ref[...], lam_init)


def _diff_sample(page_table, q_bd, k_new_t, v_new, btab, lam_p, subln, cache_kt, cache_v2, lam_init, pps):
    db, npages = page_table.shape
    rows = 2 * B_HEADS * T_PAD
    per_b = lambda shape: pl.BlockSpec((1,) + shape, lambda b, s, pt: (b,) + (0,) * len(shape))
    const = lambda shape: pl.BlockSpec(shape, lambda b, s, pt: (0,) * len(shape))
    grid_spec = pltpu.PrefetchScalarGridSpec(
        num_scalar_prefetch=1,
        grid=(db, npages // pps),
        in_specs=[per_b((rows, B_QK)), per_b((B_QK, TK)), per_b((TK, B_V)), const(btab.shape),
                  const(lam_p.shape), const(subln.shape)]
        + _page_specs(pps, B_QK, TK) + _page_specs(pps, TK * B_HEADS, B_VH),
        out_specs=per_b((T_PAD, B_V)),
        scratch_shapes=_flash_scratch(1, rows, B_VH),
    )
    return pl.pallas_call(
        functools.partial(_diff_sample_kernel, pps=pps, npages=npages, lam_init=lam_init),
        out_shape=jax.ShapeDtypeStruct((db, T_PAD, B_V), F32),
        grid_spec=grid_spec,
        compiler_params=_cparams(("parallel", "arbitrary")),
        name="diff_sample",
    )(page_table, q_bd, k_new_t, v_new, btab, lam_p, subln, *([cache_kt] * pps), *([cache_v2] * pps))


def _mid_kernel(x_ref, ma_ref, mb_ref, mk_ref, mv_ref, wout_ref, gx_ref, wq_ref, gm_ref, qg_ref, wo_ref, h2_ref):
    h = (x_ref[0] + _dot(ma_ref[0].astype(BF16), wout_ref[:A_Q, :])
         + _dot(mb_ref[0].astype(BF16), wout_ref[A_Q:, :]))
    hn = _rms(h) * gx_ref[...]
    q = _dot(hn.astype(BF16), wq_ref[...])
    ms = _dot((q * q).astype(BF16), gm_ref[...])
    q = (q * lax.rsqrt(ms + EPS) * qg_ref[...]).astype(BF16)
    m_tok = mk_ref.shape[1] // MEM_HEADS
    outs = []
    for hh in range(MEM_HEADS):
        sl = slice(hh * MEM_HEAD_DIM, (hh + 1) * MEM_HEAD_DIM)
        mk = mk_ref[0, pl.ds(hh, m_tok, stride=MEM_HEADS), :].astype(BF16)
        mv = mv_ref[0, pl.ds(hh, m_tok, stride=MEM_HEADS), :].astype(BF16)
        s = _dot_nt(q[:, sl], mk) * (MEM_HEAD_DIM ** -0.5)
        p = jnp.exp(s - jnp.max(s, axis=-1, keepdims=True))
        l = jnp.sum(p, axis=-1, keepdims=True)
        outs.append(_dot(p.astype(BF16), mv) / l)
    o = jnp.concatenate(outs, axis=-1)
    h2_ref[0] = h + _dot(o.astype(BF16), wo_ref[...])


def _mid(x, mix_a, mix_b, mk, mv, w_out, g_x, w_q, gmat, q_gain, w_o, tm):
    b, s, d = x.shape
    blk = lambda w: pl.BlockSpec((1, tm, w), lambda bi, i: (bi, i, 0))
    per_b = lambda w: pl.BlockSpec((1, mk.shape[1], w), lambda bi, i: (bi, 0, 0))
    return pl.pallas_call(
        _mid_kernel,
        out_shape=jax.ShapeDtypeStruct((b, s, d), F32),
        grid=(b, s // tm),
        in_specs=[blk(d), blk(A_Q), blk(B_V), per_b(MEM_HEAD_DIM), per_b(MEM_HEAD_DIM), _const_spec(w_out.shape),
                  _const_spec(g_x.shape), _const_spec(w_q.shape), _const_spec(gmat.shape),
                  _const_spec(q_gain.shape), _const_spec(w_o.shape)],
        out_specs=blk(d),
        compiler_params=_cparams(("parallel", "arbitrary")),
        name="mid",
    )(x, mix_a, mix_b, mk, mv, w_out, g_x, w_q, gmat, q_gain, w_o)


def _ffn_core(h, gn, wg_ref, wu_ref, cw_ref, cb_ref, wd_ref, shifted):
    xb = (_rms(h) * gn).astype(BF16)
    g = _dot(xb, wg_ref[...])
    u = _dot(xb, wu_ref[...])
    gm1, gm2 = shifted(g)
    gc = cb_ref[...] + cw_ref[0:1, :] * gm2 + cw_ref[1:2, :] * gm1 + cw_ref[2:3, :] * g
    a = gc / (1.0 + jnp.exp(-gc)) * u
    return h + _dot(a.astype(BF16), wd_ref[...]), g


def _ffn_prompt_kernel(h_ref, gn_ref, wg_ref, wu_ref, cw_ref, cb_ref, wd_ref, y_ref, tail_ref, carry_ref):
    @pl.when(pl.program_id(1) == 0)
    def _():
        carry_ref[...] = jnp.zeros(carry_ref.shape, F32)

    tm = h_ref.shape[1]
    row = lax.broadcasted_iota(I32, (tm, 1), 0)
    c0 = carry_ref[SUBLANES - 2:SUBLANES - 1, :]
    c1 = carry_ref[SUBLANES - 1:SUBLANES, :]

    def shifted(g):
        gm1 = jnp.where(row == 0, c1, pltpu.roll(g, 1, 0))
        gm2 = jnp.where(row == 0, c0, jnp.where(row == 1, c1, pltpu.roll(g, 2, 0)))
        return gm1, gm2

    y, g = _ffn_core(h_ref[0], gn_ref[...], wg_ref, wu_ref, cw_ref, cb_ref, wd_ref, shifted)
    y_ref[0] = y
    tail = g[tm - SUBLANES:, :]
    carry_ref[...] = tail
    tail_ref[0] = tail


def _ffn_prompt(h, gn, wg, wu, cw, cb, wd, tm):
    b, s, d = h.shape
    f = wg.shape[1]
    blk = pl.BlockSpec((1, tm, d), lambda bi, i: (bi, i, 0))
    wspec = lambda shape: _const_spec(shape, single_buffer=True)
    return pl.pallas_call(
        _ffn_prompt_kernel,
        out_shape=[jax.ShapeDtypeStruct((b, s, d), F32), jax.ShapeDtypeStruct((b, SUBLANES, f), F32)],
        grid=(b, s // tm),
        in_specs=[blk, _const_spec(gn.shape), wspec(wg.shape), wspec(wu.shape), _const_spec(cw.shape),
                  _const_spec(cb.shape), wspec(wd.shape)],
        out_specs=[blk, pl.BlockSpec((1, SUBLANES, f), lambda bi, i: (bi, 0, 0))],
        scratch_shapes=[pltpu.VMEM((SUBLANES, f), F32)],
        compiler_params=_cparams(("arbitrary", "arbitrary")),
        name="ffn_prompt",
    )(h, gn, wg, wu, cw, cb, wd)


def _ffn_sample_kernel(h_ref, gn_ref, wg_ref, wu_ref, cw_ref, cb_ref, wd_ref, st1_ref, st2_ref, y_ref, g_ref):
    m = h_ref.shape[0]
    t = lax.broadcasted_iota(I32, (m, 1), 0) & (T_PAD - 1)

    def shifted(g):
        gm1 = jnp.where(t == 0, st1_ref[...], pltpu.roll(g, 1, 0))
        gm2 = jnp.where(t < 2, st2_ref[...], pltpu.roll(g, 2, 0))
        return gm1, gm2

    y, g = _ffn_core(h_ref[...], gn_ref[...], wg_ref, wu_ref, cw_ref, cb_ref, wd_ref, shifted)
    y_ref[...] = y
    g_ref[...] = g


def _ffn_sample(h2d, gn, wg, wu, cw, cb, wd, st1, st2):
    m, d = h2d.shape
    f = wg.shape[1]
    wspec = lambda shape: _const_spec(shape, single_buffer=True)
    return pl.pallas_call(
        _ffn_sample_kernel,
        out_shape=[jax.ShapeDtypeStruct((m, d), F32), jax.ShapeDtypeStruct((m, f), F32)],
        grid=(1,),
        in_specs=[_const_spec((m, d)), _const_spec(gn.shape), wspec(wg.shape), wspec(wu.shape),
                  _const_spec(cw.shape), _const_spec(cb.shape), wspec(wd.shape), _const_spec((m, f)),
                  _const_spec((m, f))],
        out_specs=[_const_spec((m, d)), _const_spec((m, f))],
        compiler_params=_cparams(("arbitrary",)),
        name="ffn_sample",
    )(h2d, gn, wg, wu, cw, cb, wd, st1, st2)


def _rel_bucket(dist):
    n = np.maximum(dist, 0)
    max_exact = NUM_BUCKETS // 2
    nf = np.maximum(n, 1).astype(np.float32)
    log_b = (np.log(nf / np.float32(max_exact)) / np.float32(math.log(MAX_DISTANCE / max_exact))
             * np.float32(NUM_BUCKETS - max_exact))
    large = np.minimum(max_exact + log_b.astype(np.int32), NUM_BUCKETS - 1)
    return np.where(n < max_exact, n, large)


def _bias_by_dist(dist, causal, bias):
    onehot = (_rel_bucket(dist)[..., None] == np.arange(NUM_BUCKETS)).astype(np.float32)
    vals = jnp.einsum("...k,kh->h...", onehot, bias, precision=lax.Precision.HIGHEST)
    return jnp.where(causal[None], vals, NEG).astype(F32)


def _prompt_bias_tables(bias):
    r = np.arange(TK)[:, None]
    c = np.arange(TK)[None, :]
    always = np.ones((TK, TK), bool)
    masked = _bias_by_dist(r - c, ~always, bias)
    t0 = _bias_by_dist(r - c, r >= c, bias)
    t1 = _bias_by_dist(r - c + TK, always, bias)
    t2 = _bias_by_dist(r - c + 2 * TK, always, bias)
    return jnp.swapaxes(jnp.stack([masked, t0, t1, t2], axis=1), -1, -2) * LOG2E


def _sample_bias_tables(bias, streams_per_head):
    t = np.arange(T_PAD)[:, None]
    c = np.arange(TK)[None, :]
    always = np.ones((T_PAD, TK), bool)
    far = _bias_by_dist(t - c + 2 * TK, always, bias)
    last = _bias_by_dist(t - c + TK, always, bias)
    new = _bias_by_dist(t - c, c <= t, bias)
    tabs = jnp.stack([far, last, new], axis=0)
    tabs = jnp.repeat(tabs[:, :, None], streams_per_head, axis=2)
    return tabs.reshape(3, -1, TK)


def _group_mean_matrix(width, group):
    idx = jnp.arange(width) // group
    return jnp.where(idx[:, None] == idx[None, :], 1.0 / group, 0.0).astype(BF16)


def _pad_rows(x, rows):
    return jnp.pad(x, ((0, 0), (0, rows - x.shape[1]), (0, 0)))


def _new_keys_t(x, rows):
    return jnp.swapaxes(_pad_rows(x, rows), 1, 2)


def kernel(x_prompt, x_sample, mem_prompt, cache_a_k, cache_a_v, cache_idx_k, cache_b_k, cache_b_v, cache_mem_k, cache_mem_v, state_ffn_conv, page_table, rel_bias, norm_mix, w_in, a_q_norm, a_k_norm, b_q_norm, b_k_norm, diff_lambda, diff_subln, w_out, norm_mem_x, norm_mem_src, w_mem_q, w_mem_kv, mem_q_norm, mem_k_norm, w_mem_o, norm_ffn, w_up, w_gate, ffn_conv_w, ffn_conv_b, w_down):
    depth = w_in.shape[0]
    assert depth == 1, "single-layer trunk"
    layer = 0
    lam_init = 0.8 - 0.6 * math.exp(-0.3 * layer)
    b, s, d = x_prompt.shape
    db, t_new, _ = x_sample.shape
    assert CONV_W - 1 <= t_new <= T_PAD and cache_a_k.shape[2] == TK
    m_tok = mem_prompt.shape[1]
    f = w_up.shape[-1]
    n_pool = cache_a_k.shape[1]
    tkb = min(TKB_MAX, s)
    tkb_diff = min(TKB_DIFF, s)
    assert s % tkb == 0 and tkb % TK == 0 and s % tkb_diff == 0 and tkb_diff % TK == 0

    w_in_l = w_in[layer]
    n_front = A_Q + 2 * A_KV + IDX_Q + IDX_DIM + IDX_HEADS
    w_in_p = jnp.concatenate(
        [w_in_l[:, :n_front], jnp.zeros((d, C_QB - n_front), F32), w_in_l[:, n_front:]], axis=1).astype(BF16)
    ones = lambda n: jnp.ones((n,), F32)
    hgain = jnp.concatenate([
        jnp.tile(a_q_norm[layer], A_HEADS), jnp.tile(a_k_norm[layer], A_KV_HEADS), ones(C_QB - C_VA),
        jnp.tile(b_q_norm[layer], 2 * B_HEADS), jnp.tile(b_k_norm[layer], 2 * B_HEADS), ones(B_V)])[None, :]
    gmat64 = _group_mean_matrix(A_Q, A_HEAD_DIM)
    gmat128 = _group_mean_matrix(MEM_W, MEM_HEAD_DIM)
    proj_segs = ((C_QA, A_Q, True, A_Q, 1), (C_KA, A_KV, True, A_KV, 1), (C_VA, A_KV, False, A_KV, 1),
                 (C_QI, IDX_Q, False, IDX_Q, 1), (C_KIW, LANES, False, LANES, 1), (C_KIW, LANES, False, IDX_DIM, 1),
                 (C_QB, B_QK, True, B_QK, 1), (C_KB, B_QK, True, B_QK, 1), (C_VB, B_V, False, B_V, 1))
    prompt_segs = ((C_QA, A_Q, True, A_Q, 1), (C_KA, A_KV, True, A_KV, 1), (C_QI, IDX_Q, False, IDX_Q, 1),
                   (C_KIW, LANES, False, LANES, 1), (C_QB, B_QK, True, B_QK, 1), (C_KB, B_QK, True, B_QK, 1),
                   (C_VA, A_KV, False, A_KV, -tkb), (C_VB, B_V, False, B_V, -tkb_diff),
                   (C_KA, A_KV, True, A_KV, FEATURE_MAJOR), (C_VA, A_KV, False, A_KV, FEATURE_MAJOR),
                   (C_KIW, LANES, False, IDX_DIM, FEATURE_MAJOR), (C_KB, B_QK, True, B_QK, FEATURE_MAJOR),
                   (C_VB, B_V, False, B_V, B_HEADS))
    g_mix = norm_mix[layer][None, :]
    w_out_b = w_out[layer].astype(BF16)
    w_q_b = w_mem_q[layer].astype(BF16)
    w_o_b = w_mem_o[layer].astype(BF16)
    w_kv_b = w_mem_kv[layer].astype(BF16)
    w_gate_b = w_gate[layer].astype(BF16)
    w_up_b = w_up[layer].astype(BF16)
    w_down_b = w_down[layer].astype(BF16)
    g_memx = norm_mem_x[layer][None, :]
    q_gain = jnp.tile(mem_q_norm[layer], MEM_HEADS)[None, :]
    kv_gain = jnp.concatenate([jnp.tile(mem_k_norm[layer], MEM_HEADS), ones(MEM_W)])[None, :]
    g_ffn = norm_ffn[layer][None, :]
    conv_w = ffn_conv_w[layer]
    conv_b = ffn_conv_b[layer][None, :]
    lam_p = diff_lambda[layer]
    subln = diff_subln[layer][None, :]
    bias_a = rel_bias[:, :A_HEADS]
    bias_b = rel_bias[:, A_HEADS:]

    assert TM_TOKENS % tkb == 0 and TM_TOKENS % tkb_diff == 0, "value chunks must tile the projection's token tile"
    qa, ka, qi, kiw, qb, kb, va_ch, vb_ch, ka_t, va_t, ki_t, kb_t, vb_c = _norm_proj(
        x_prompt.reshape(b * s, d), g_mix, w_in_p, gmat64, hgain, prompt_segs, TM_TOKENS, "proj_prompt", seq=s)

    def from_feature_major(a_t, shape):
        nd = len(shape)
        return jnp.transpose(a_t.reshape((b,) + shape + (s,)), (0, nd + 1) + tuple(range(1, nd + 1)))[None]
    r3 = lambda a: a.reshape(b, s, a.shape[-1])
    mix_a = _dsa_prompt(r3(qi), r3(kiw), r3(qa), r3(ka), va_ch, _prompt_bias_tables(bias_a), tkb)
    mix_b = _diff_prompt(r3(qb), r3(kb), vb_ch, _prompt_bias_tables(bias_b), lam_p, subln, lam_init, tkb_diff)
    mk, mv = _norm_proj(mem_prompt.reshape(b * m_tok, d), norm_mem_src[layer][None, :], w_kv_b, gmat128, kv_gain,
                        ((0, MEM_W, True, MEM_W, MEM_HEADS), (MEM_W, MEM_W, False, MEM_W, MEM_HEADS)), 256, "mem_kv")
    mem_rows = lambda a: a.reshape(-1, m_tok * MEM_HEADS, MEM_HEAD_DIM)
    h2 = _mid(x_prompt, mix_a, mix_b, mem_rows(mk), mem_rows(mv), w_out_b, g_memx, w_q_b, gmat128, q_gain, w_o_b,
              TM_TOKENS)
    yp, tail = _ffn_prompt(h2, g_ffn, w_gate_b, w_up_b, conv_w, conv_b, w_down_b, TM_TOKENS)
    conv_p = tail[:, SUBLANES - (CONV_W - 1):, :]

    xs = _pad_rows(x_sample, T_PAD)
    qa, ka_s, va_s, qi, kiw, ki_s, qb, kb_s, vb_s = _norm_proj(
        xs.reshape(db * T_PAD, d), g_mix, w_in_p, gmat64, hgain, proj_segs, db * T_PAD, "proj_sample")
    r3 = lambda a: a.reshape(db, T_PAD, a.shape[-1])
    idx_kt = jnp.transpose(cache_idx_k[layer], (0, 2, 1))
    a_kt = jnp.transpose(cache_a_k[layer], (0, 2, 3, 1)).reshape(n_pool, A_KV, TK)
    a_vt = jnp.transpose(cache_a_v[layer], (0, 2, 3, 1)).reshape(n_pool, A_KV, TK)
    b_kt = jnp.transpose(cache_b_k[layer], (0, 2, 3, 4, 1)).reshape(n_pool, B_QK, TK)
    b_v2 = cache_b_v[layer].reshape(n_pool, TK * B_HEADS, B_VH)
    n_pages = page_table.shape[1]
    qi_rows = r3(qi).reshape(db, T_PAD, IDX_HEADS, IDX_DIM).transpose(0, 2, 1, 3).reshape(db, IDX_HEADS * T_PAD, IDX_DIM)
    wi_rows = r3(kiw)[:, :, IDX_DIM:IDX_DIM + IDX_HEADS].transpose(0, 2, 1).reshape(db, IDX_HEADS * T_PAD, 1)
    selb = _dsa_sample_select(page_table, qi_rows, wi_rows, _new_keys_t(r3(ki_s), TK), idx_kt, t_new,
                              math.gcd(n_pages, SELECT_PAGES_PER_STEP))
    q_rows = r3(qa).reshape(db, T_PAD, A_HEADS, A_HEAD_DIM).transpose(0, 2, 1, 3)
    eye_g = jnp.repeat(jnp.eye(A_KV_HEADS, dtype=F32), A_REP, axis=0)
    qa_bd = (q_rows[:, :, :, None, :] * eye_g[None, :, None, :, None]).reshape(db, A_HEADS * T_PAD, A_KV)
    mix_a = _dsa_sample_attn(page_table, qa_bd, _new_keys_t(r3(ka_s), TK), _new_keys_t(r3(va_s), TK), selb,
                             _sample_bias_tables(bias_a, 1), a_kt, a_vt)
    n_str = 2 * B_HEADS
    q_rows = r3(qb).reshape(db, T_PAD, n_str, B_HEAD_DIM).transpose(0, 2, 1, 3)
    qb_bd = (q_rows[:, :, :, None, :] * jnp.eye(n_str, dtype=F32)[None, :, None, :, None]).reshape(
        db, n_str * T_PAD, B_QK)
    mix_b = _diff_sample(page_table, qb_bd, _new_keys_t(r3(kb_s), TK), _pad_rows(r3(vb_s), TK),
                         _sample_bias_tables(bias_b, 2), lam_p, subln, b_kt, b_v2, lam_init,
                         math.gcd(n_pages, DIFF_PAGES_PER_STEP))
    h2 = _mid(xs, mix_a, mix_b, mem_rows(cache_mem_k[layer]), mem_rows(cache_mem_v[layer]), w_out_b, g_memx, w_q_b,
              gmat128, q_gain, w_o_b, T_PAD)
    state = state_ffn_conv[layer]
    st1 = _pad_rows(state[:, 1:2, :], T_PAD).reshape(db * T_PAD, f)
    st2 = _pad_rows(state, T_PAD).reshape(db * T_PAD, f)
    ys, g_s = _ffn_sample(h2.reshape(db * T_PAD, d), g_ffn, w_gate_b, w_up_b, conv_w, conv_b, w_down_b, st1, st2)
    ys = ys.reshape(db, T_PAD, d)[:, :t_new]
    conv_s = g_s.reshape(db, T_PAD, f)[:, t_new - (CONV_W - 1):t_new]

    def new_rows(a, shape):
        return a.reshape(db, T_PAD, -1)[:, :t_new].reshape((1, db, t_new) + shape)

    return (yp, ys,
            from_feature_major(ka_t, (A_KV_HEADS, A_HEAD_DIM)), from_feature_major(va_t, (A_KV_HEADS, A_HEAD_DIM)),
            from_feature_major(ki_t, (IDX_DIM,)), from_feature_major(kb_t, (B_HEADS, 2, B_HEAD_DIM)),
            vb_c.reshape(1, b, s, B_HEADS, 2 * B_HEAD_DIM),
            mk.reshape(1, b, m_tok, MEM_HEADS, MEM_HEAD_DIM), mv.reshape(1, b, m_tok, MEM_HEADS, MEM_HEAD_DIM),
            conv_p[None],
            new_rows(ka_s, (A_KV_HEADS, A_HEAD_DIM)), new_rows(va_s, (A_KV_HEADS, A_HEAD_DIM)),
            new_rows(ki_s, (IDX_DIM,)), new_rows(kb_s, (B_HEADS, 2, B_HEAD_DIM)),
            new_rows(vb_s, (B_HEADS, 2 * B_HEAD_DIM)), conv_s[None])
```

```python
import functools
import math

import jax
import jax.numpy as jnp
import numpy as np
from jax import lax
from jax.experimental import pallas as pl
from jax.experimental.pallas import tpu as pltpu

F32 = jnp.float32
BF16 = jnp.bfloat16
I32 = jnp.int32

EPS = 1e-6
NEG = -1e30
LOG2E = math.log2(math.e)
INT_MIN = -(2 ** 31)

A_HEADS = 8
A_KV_HEADS = 2
A_HEAD_DIM = 64
IDX_HEADS = 4
IDX_DIM = 64
TOPK_MAX = 256
B_HEADS = 4
B_HEAD_DIM = 64
MEM_HEADS = 4
MEM_HEAD_DIM = 128
CONV_W = 3
NUM_BUCKETS = 32
MAX_DISTANCE = 128

A_Q = A_HEADS * A_HEAD_DIM
A_KV = A_KV_HEADS * A_HEAD_DIM
IDX_Q = IDX_HEADS * IDX_DIM
B_QK = B_HEADS * 2 * B_HEAD_DIM
B_V = B_HEADS * 2 * B_HEAD_DIM
B_VH = 2 * B_HEAD_DIM
MEM_W = MEM_HEADS * MEM_HEAD_DIM
A_REP = A_HEADS // A_KV_HEADS

LANES = 128
SUBLANES = 8
TQ_DSA = 512
TQ_DIFF = 512
TK = 128
TKB_MAX = 512
T_PAD = SUBLANES
TM_TOKENS = 512
FEATURE_MAJOR = 0
TKB_DIFF = 512
MID_SEQS_PER_STEP = 8
SELECT_PAGES_PER_STEP = 64
DIFF_PAGES_PER_STEP = 32
VMEM_LIMIT = 56 * 1024 * 1024

C_QA = 0
C_KA = C_QA + A_Q
C_VA = C_KA + A_KV
C_QI = C_VA + A_KV
C_KIW = C_QI + IDX_Q
C_QB = C_KIW + LANES
C_KB = C_QB + B_QK
C_VB = C_KB + B_QK
D_IN_PAD = C_VB + B_V


def _dot(a, b):
    return jnp.dot(a, b, preferred_element_type=F32)


def _dot_nt(a, b):
    return lax.dot_general(a, b, (((1,), (1,)), ((), ())), preferred_element_type=F32)


def _rms(x):
    return x * lax.rsqrt(jnp.mean(x * x, axis=-1, keepdims=True) + EPS)


KEY_NEG_INF = -(2 ** 31) + 0x7FFFFF


def _key_to_float(key):
    bits = key ^ ((key >> 31) & 0x7FFFFFFF)
    return jnp.where(key <= KEY_NEG_INF, -jnp.inf, lax.bitcast_convert_type(bits, F32))


def _kth_largest(count_ge, shape, topk):
    def bit_body(it, key):
        cand = key ^ lax.shift_left(jnp.int32(1), 31 - it)
        return jnp.where(count_ge(_key_to_float(cand)) >= topk, cand, key)

    return _key_to_float(lax.fori_loop(0, 32, bit_body, jnp.full(shape, INT_MIN, I32)))


def _cparams(sem, vmem=VMEM_LIMIT):
    return pltpu.CompilerParams(dimension_semantics=sem, vmem_limit_bytes=vmem)


def _const_spec(shape, single_buffer=False):
    nd = len(shape)
    if single_buffer:
        return pl.BlockSpec(shape, lambda *_: (0,) * nd, pipeline_mode=pl.Buffered(1))
    return pl.BlockSpec(shape, lambda *_: (0,) * nd)


def _norm_proj_kernel(x_ref, g_ref, w_ref, gm_ref, hg_ref, *out_refs, segs):
    xn = _rms(x_ref[...]) * g_ref[...]
    p = _dot(xn.astype(BF16), w_ref[...])
    tm = x_ref.shape[0]
    done = {}
    for (start, width, norm, out_width, split), o_ref in zip(segs, out_refs):
        if (start, width, norm) not in done:
            s = p[:, start:start + width]
            if norm:
                ms = _dot((s * s).astype(BF16), gm_ref[:width, :width])
                s = s * lax.rsqrt(ms + EPS) * hg_ref[:, start:start + width]
            done[(start, width, norm)] = s
        s = done[(start, width, norm)]
        if split == 1:
            o_ref[...] = s[:, :out_width]
        elif split == FEATURE_MAJOR:
            o_ref[0] = s.T[:out_width, :]
        elif split < 0:
            st = s.T
            for u in range(tm // -split):
                o_ref[0, u] = st[:out_width, u * -split:(u + 1) * -split]
        else:
            pw = out_width // split
            for j in range(split):
                o_ref[pl.ds(j, tm, stride=split), :] = s[:, j * pw:(j + 1) * pw]


def _norm_proj(x2d, gain, w, gmat, hgain, segs, tm, name, seq=None):
    m, d = x2d.shape
    n = w.shape[1]
    tiles = None if seq is None else seq // tm

    def shape_spec(ow, sp):
        if sp == FEATURE_MAJOR:
            return (jax.ShapeDtypeStruct((m // seq, ow, seq), F32),
                    pl.BlockSpec((1, ow, tm), lambda i: (i // tiles, 0, i % tiles)))
        if sp < 0:
            return (jax.ShapeDtypeStruct((m // seq, seq // -sp, ow, -sp), F32),
                    pl.BlockSpec((1, tm // -sp, ow, -sp), lambda i: (i // tiles, i % tiles, 0, 0)))
        return (jax.ShapeDtypeStruct((m * sp, ow // sp), F32), pl.BlockSpec((tm * sp, ow // sp), lambda i: (i, 0)))

    shapes, specs = zip(*[shape_spec(ow, sp) for (_, _, _, ow, sp) in segs])
    return pl.pallas_call(
        functools.partial(_norm_proj_kernel, segs=segs),
        out_shape=list(shapes),
        grid=(m // tm,),
        in_specs=[pl.BlockSpec((tm, d), lambda i: (i, 0)), _const_spec((1, d)), _const_spec((d, n)),
                  _const_spec(gmat.shape), _const_spec((1, n))],
        out_specs=list(specs),
        compiler_params=_cparams(("parallel",)),
        name=name,
    )(x2d, gain, w, gmat, hgain)


def _init_flash(m_sc, l_sc, acc_sc):
    m_sc[...] = jnp.full(m_sc.shape, -jnp.inf, F32)
    l_sc[...] = jnp.zeros(l_sc.shape, F32)
    acc_sc[...] = jnp.zeros(acc_sc.shape, F32)


def _flash_update(st, s, pv_fn, m_sc, l_sc, acc_sc):
    m_old = m_sc[st]
    m_new = jnp.maximum(m_old, jnp.max(s, axis=-1, keepdims=True))
    alpha = jnp.exp(m_old - m_new)
    p = jnp.exp(s - m_new)
    l_sc[st] = alpha * l_sc[st] + jnp.sum(p, axis=-1, keepdims=True)
    acc_sc[st] = alpha * acc_sc[st] + pv_fn(p.astype(BF16))
    m_sc[st] = m_new


def _flash_scratch(streams, rows, dv):
    return [pltpu.VMEM((streams, rows, 1), F32), pltpu.VMEM((streams, rows, 1), F32),
            pltpu.VMEM((streams, rows, dv), F32)]


def _select_mask(sk, thr, need, before):
    need = jnp.where(thr > -jnp.inf, need, 0.0)
    return jnp.where(sk > thr, 0.0, jnp.where(sk == thr, jnp.where(before < need, 0.0, NEG), NEG))


def _chunk_bias(dtab_ref, h, i, c, ksub, tq):
    qsub = tq // TK
    return jnp.concatenate(
        [jnp.concatenate([dtab_ref[h, jnp.clip(i * qsub + a - (c * ksub + j) + 1, 0, 3)] for a in range(qsub)], axis=1)
         for j in range(ksub)], axis=0)


def _flash_update_t(st, s, vt, m_sc, l_sc, acc_sc):
    m_old = m_sc[st]
    m_new = jnp.maximum(m_old, jnp.max(_col_reduce(s, jnp.max), axis=0, keepdims=True))
    alpha = jnp.exp2(m_old - m_new)
    p = jnp.exp2(s - m_new)
    l_sc[st] = alpha * l_sc[st] + jnp.sum(_col_reduce(p, jnp.sum), axis=0, keepdims=True)
    acc_sc[st] = alpha * acc_sc[st] + _dot(vt, p.astype(BF16))
    m_sc[st] = m_new


def _flash_scratch_t(streams, dv, cols):
    return [pltpu.VMEM((streams, 1, cols), F32), pltpu.VMEM((streams, 1, cols), F32),
            pltpu.VMEM((streams, dv, cols), F32)]


REDUCE_WAYS = 8


def _col_reduce(x, op):
    rows, cols = x.shape
    slabs = rows // SUBLANES
    if slabs % REDUCE_WAYS == 0 and slabs > REDUCE_WAYS:
        x = op(x.reshape(REDUCE_WAYS, slabs // REDUCE_WAYS, SUBLANES, cols), axis=1)
    else:
        x = x.reshape(slabs, SUBLANES, cols)
    return op(x, axis=0)


def _col_count(w):
    return _col_reduce(w, jnp.sum)


def _dsa_prompt_kernel(qi_ref, wit_ref, kiw_ref, qa_ref, ka_ref, vat_ref, dtab_ref, tril_ref, o_ref,
                       skey_ref, selb_ref, m_sc, l_sc, acc_sc, *, topk, tkb):
    tq = TQ_DSA
    i = pl.program_id(1)
    ksub = tkb // TK
    nbig = (i * tq + tq - 1) // tkb + 1
    qi = qi_ref[0].astype(BF16)
    wit = wit_ref[0] * (IDX_HEADS ** -0.5 * IDX_DIM ** -0.5)
    krow = lax.broadcasted_iota(I32, (tkb, tq), 0)
    qcol = lax.broadcasted_iota(I32, (tkb, tq), 1)

    def score_body(c, carry):
        off = pl.multiple_of(c * tkb, tkb)
        kc = kiw_ref[0, pl.ds(off, tkb), 0:IDX_DIM].astype(BF16)
        sc = jnp.zeros((tkb, tq), F32)
        for h in range(IDX_HEADS):
            d = _dot_nt(kc, qi[:, h * IDX_DIM:(h + 1) * IDX_DIM])
            sc = sc + jnp.maximum(d, 0.0) * wit[h:h + 1, :]
        causal = (c * tkb + krow) <= (i * tq + qcol)
        skey_ref[c] = jnp.where(causal, sc, -jnp.inf)
        return carry

    lax.fori_loop(0, nbig, score_body, 0)

    def count(pred_fn):
        def body(c, cnt):
            return cnt + _col_count(jnp.where(pred_fn(skey_ref[c]), 1.0, 0.0))
        cnt = lax.fori_loop(0, nbig, body, jnp.zeros((SUBLANES, tq), F32))
        return jnp.sum(cnt, axis=0, keepdims=True)

    thr = _kth_largest(lambda t: count(lambda sk: sk >= t), (1, tq), topk)
    need = float(topk) - count(lambda sk: sk > thr)
    tril = tril_ref[...]

    def sel_body(c, off):
        sk = skey_ref[c]
        eqf = jnp.where(sk == thr, 1.0, 0.0)
        before = _dot(tril, eqf.astype(BF16)) + off
        selb_ref[c] = _select_mask(sk, thr, need, before)
        return off + jnp.sum(_col_count(eqf), axis=0, keepdims=True)

    lax.fori_loop(0, nbig, sel_body, jnp.zeros((1, tq), F32))

    _init_flash(m_sc, l_sc, acc_sc)
    scale = A_HEAD_DIM ** -0.5 * LOG2E
    qg = [jnp.concatenate([qa_ref[0, :, h * A_HEAD_DIM:(h + 1) * A_HEAD_DIM] * scale
                           for h in range(g * A_REP, (g + 1) * A_REP)], axis=0).astype(BF16)
          for g in range(A_KV_HEADS)]

    def att_body(c, carry):
        off = pl.multiple_of(c * tkb, tkb)
        maskb = selb_ref[c]
        for g in range(A_KV_HEADS):
            g0 = g * A_HEAD_DIM
            kc = ka_ref[0, pl.ds(off, tkb), g0:g0 + A_HEAD_DIM].astype(BF16)
            vt = vat_ref[0, c, g0:g0 + A_HEAD_DIM, :].astype(BF16)
            sg = _dot_nt(kc, qg[g])
            for r in range(A_REP):
                h = g * A_REP + r
                s = sg[:, r * tq:(r + 1) * tq] + (_chunk_bias(dtab_ref, h, i, c, ksub, tq) + maskb)
                _flash_update_t(h, s, vt, m_sc, l_sc, acc_sc)
        return carry

    lax.fori_loop(0, nbig, att_body, 0)
    for h in range(0, A_HEADS, 2):
        ot = jnp.concatenate([acc_sc[h] / l_sc[h], acc_sc[h + 1] / l_sc[h + 1]], axis=0)
        o_ref[0, :, h * A_HEAD_DIM:(h + 2) * A_HEAD_DIM] = ot.T


def _dsa_prompt(qi, kiw, qa, ka, vat, dtab, tkb):
    b, s, _ = qa.shape
    tq = TQ_DSA
    nq = s // tq
    nc = s // tkb
    topk = min(TOPK_MAX, s // 4)
    tril = (jnp.arange(tkb)[None, :] < jnp.arange(tkb)[:, None]).astype(BF16)
    wit = jnp.swapaxes(kiw[:, :, IDX_DIM:IDX_DIM + SUBLANES], 1, 2)
    blk = lambda w: pl.BlockSpec((1, tq, w), lambda bi, i: (bi, i, 0))
    full = lambda w: pl.BlockSpec((1, s, w), lambda bi, i: (bi, 0, 0))
    return pl.pallas_call(
        functools.partial(_dsa_prompt_kernel, topk=topk, tkb=tkb),
        out_shape=jax.ShapeDtypeStruct((b, s, A_Q), F32),
        grid=(b, nq),
        in_specs=[blk(IDX_Q), pl.BlockSpec((1, SUBLANES, tq), lambda bi, i: (bi, 0, i)), full(LANES), blk(A_Q),
                  full(A_KV), pl.BlockSpec((1, nc, A_KV, tkb), lambda bi, i: (bi, 0, 0, 0)),
                  _const_spec(dtab.shape), _const_spec(tril.shape)],
        out_specs=blk(A_Q),
        scratch_shapes=[pltpu.VMEM((nc, tkb, tq), F32), pltpu.VMEM((nc, tkb, tq), F32)]
        + _flash_scratch_t(A_HEADS, A_HEAD_DIM, tq),
        compiler_params=_cparams(("parallel", "arbitrary")),
        name="dsa_prompt",
    )(qi, wit, kiw, qa, ka, vat, dtab, tril)


def _diff_lambda(lp_ref, lam_init):
    lp = lp_ref[...]
    s1 = jnp.sum(lp[0:1] * lp[1:2], axis=-1, keepdims=True)
    s2 = jnp.sum(lp[2:3] * lp[3:4], axis=-1, keepdims=True)
    return jnp.exp(s1) - jnp.exp(s2) + lam_init


def _diff_finish(o0, o1, lam, sg, lam_init):
    o = o0 - lam * o1
    return _rms(o) * sg * (1.0 - lam_init)


def _diff_prompt_kernel(q_ref, k_ref, vt_ref, dtab_ref, lp_ref, sg_ref, o_ref, m_sc, l_sc, acc_sc, *, lam_init, tkb):
    tq = TQ_DIFF
    i = pl.program_id(1)
    ksub = tkb // TK
    nbig = (i * tq + tq - 1) // tkb + 1
    scale = B_HEAD_DIM ** -0.5 * LOG2E
    _init_flash(m_sc, l_sc, acc_sc)

    def body(c, carry):
        off = pl.multiple_of(c * tkb, tkb)
        for h in range(B_HEADS):
            bias = _chunk_bias(dtab_ref, h, i, c, ksub, tq)
            vt = vt_ref[0, c, h * B_VH:(h + 1) * B_VH, :].astype(BF16)
            for comp in range(2):
                st = h * 2 + comp
                c0 = st * B_HEAD_DIM
                qh = (q_ref[0, :, c0:c0 + B_HEAD_DIM] * scale).astype(BF16)
                kc = k_ref[0, pl.ds(off, tkb), c0:c0 + B_HEAD_DIM].astype(BF16)
                _flash_update_t(st, _dot_nt(kc, qh) + bias, vt, m_sc, l_sc, acc_sc)
        return carry

    lax.fori_loop(0, nbig, body, 0)
    lam = _diff_lambda(lp_ref, lam_init)
    for h in range(B_HEADS):
        o0 = (acc_sc[2 * h] / l_sc[2 * h]).T
        o1 = (acc_sc[2 * h + 1] / l_sc[2 * h + 1]).T
        o_ref[0, :, h * B_VH:(h + 1) * B_VH] = _diff_finish(o0, o1, lam, sg_ref[...], lam_init)


def _diff_prompt(qb, kb, vbt, dtab, lam_p, subln, lam_init, tkb):
    b, s, _ = qb.shape
    tq = TQ_DIFF
    nq = s // tq
    nc = s // tkb
    blk = lambda w: pl.BlockSpec((1, tq, w), lambda bi, i: (bi, i, 0))
    full = lambda w: pl.BlockSpec((1, s, w), lambda bi, i: (bi, 0, 0))
    return pl.pallas_call(
        functools.partial(_diff_prompt_kernel, lam_init=lam_init, tkb=tkb),
        out_shape=jax.ShapeDtypeStruct((b, s, B_V), F32),
        grid=(b, nq),
        in_specs=[blk(B_QK), full(B_QK), pl.BlockSpec((1, nc, B_V, tkb), lambda bi, i: (bi, 0, 0, 0)),
                  _const_spec(dtab.shape), _const_spec(lam_p.shape), _const_spec(subln.shape)],
        out_specs=blk(B_V),
        scratch_shapes=_flash_scratch_t(2 * B_HEADS, B_VH, tq),
        compiler_params=_cparams(("parallel", "arbitrary")),
        name="diff_prompt",
    )(qb, kb, vbt, dtab, lam_p, subln)


def _page_specs(pps, rows, cols):
    return [pl.BlockSpec((1, rows, cols), functools.partial(lambda b, s, pt, j: (pt[b, s * pps + j], 0, 0), j=j))
            for j in range(pps)]


def _all_page_specs(npages, rows, cols):
    return [pl.BlockSpec((1, rows, cols), functools.partial(lambda b, pt, j: (pt[b, j], 0, 0), j=j))
            for j in range(npages)]


def _dsa_sample_select_kernel(pt_ref, q_ref, wi_ref, kinew_ref, *rest, pps, npages, topk, group):
    del pt_ref
    page_refs = rest[:pps]
    tri_ref, selb_ref, sc_ref = rest[pps:]
    b = pl.program_id(0)
    s = pl.program_id(1)
    np1 = npages + 1
    q = q_ref[0].astype(BF16)
    wi = wi_ref[0] * (IDX_HEADS ** -0.5 * IDX_DIM ** -0.5)

    def scores(kt):
        return jnp.sum((jnp.maximum(_dot(q, kt), 0.0) * wi).reshape(IDX_HEADS, T_PAD, TK), axis=0)

    base = b * np1
    for j in range(pps):
        sc_ref[base + s * pps + j] = scores(page_refs[j][0].astype(BF16))

    last_page_step = s == pl.num_programs(1) - 1

    @pl.when(last_page_step)
    def _():
        row = lax.broadcasted_iota(I32, (T_PAD, TK), 0)
        col = lax.broadcasted_iota(I32, (T_PAD, TK), 1)
        sc_ref[base + npages] = jnp.where(col <= row, scores(kinew_ref[0].astype(BF16)), -jnp.inf)

    @pl.when(last_page_step & (b == pl.num_programs(0) - 1))
    def _():
        def group_body(gi, carry):
            sk = sc_ref[pl.ds(gi * (group * np1), group * np1)].reshape(group, np1, T_PAD, TK)

            def count(pred):
                cnt = jnp.sum(jnp.where(pred, 1.0, 0.0), axis=1)
                return jnp.sum(cnt, axis=-1, keepdims=True)

            thr = _kth_largest(lambda t: count(sk >= t[:, None]), (group, T_PAD, 1), topk)
            need = float(topk) - count(sk > thr[:, None])
            eqf = jnp.where(sk == thr[:, None], 1.0, 0.0)
            before = _dot(eqf.reshape(group * np1 * T_PAD, TK).astype(BF16), tri_ref[...]).reshape(sk.shape)
            ties = jnp.sum(eqf, axis=-1, keepdims=True)
            off = jnp.zeros((group, T_PAD, 1), F32)
            for c in range(np1):
                selb_ref[pl.ds(gi * group, group), c] = _select_mask(sk[:, c], thr, need, before[:, c] + off)
                off = off + ties[:, c]
            return carry

        lax.fori_loop(0, pl.num_programs(0) // group, group_body, 0)


def _dsa_sample_select(page_table, q_rows, wi_rows, ki_new_t, cache_ikt, n_new, pps):
    db, npages = page_table.shape
    np1 = npages + 1
    topk = min(TOPK_MAX, (npages * TK + n_new) // 4)
    group = math.gcd(db, SUBLANES)
    tri = (jnp.arange(TK)[:, None] < jnp.arange(TK)[None, :]).astype(BF16)
    per_b = lambda shape: pl.BlockSpec((1,) + shape, lambda b, s, pt: (b,) + (0,) * len(shape))
    grid_spec = pltpu.PrefetchScalarGridSpec(
        num_scalar_prefetch=1,
        grid=(db, npages // pps),
        in_specs=[per_b(q_rows.shape[1:]), per_b(wi_rows.shape[1:]), per_b((IDX_DIM, TK))]
        + _page_specs(pps, IDX_DIM, TK) + [pl.BlockSpec(tri.shape, lambda b, s, pt: (0, 0))],
        out_specs=pl.BlockSpec((db, np1, T_PAD, TK), lambda b, s, pt: (0, 0, 0, 0)),
        scratch_shapes=[pltpu.VMEM((db * np1, T_PAD, TK), F32)],
    )
    return pl.pallas_call(
        functools.partial(_dsa_sample_select_kernel, pps=pps, npages=npages, topk=topk, group=group),
        out_shape=jax.ShapeDtypeStruct((db, np1, T_PAD, TK), F32),
        grid_spec=grid_spec,
        compiler_params=_cparams(("arbitrary", "arbitrary")),
        name="dsa_sample_select",
    )(page_table, q_rows, wi_rows, ki_new_t, *([cache_ikt] * pps), tri)


def _page_bias(btab_ref, c0, n, npages):
    return jnp.concatenate([btab_ref[jnp.where(c0 + j == npages - 1, 1, 0)] for j in range(n)], axis=1)


def _dsa_sample_attn_kernel(pt_ref, q_ref, knew_ref, vnew_ref, selb_ref, btab_ref, *rest, npages):
    del pt_ref
    o_ref = rest[2 * npages]
    kts = [r[0] for r in rest[:npages]] + [knew_ref[0]]
    vts = [r[0] for r in rest[npages:2 * npages]] + [vnew_ref[0]]
    n = npages + 1
    rows = A_HEADS * T_PAD
    scale = A_HEAD_DIM ** -0.5
    q = (q_ref[0] * scale).astype(BF16)
    sc = jnp.concatenate([_dot(q, kt.astype(BF16)) for kt in kts], axis=1)
    bias = jnp.concatenate([btab_ref[0]] * (npages - 1) + [btab_ref[1], btab_ref[2]], axis=1)
    mask = jnp.concatenate([selb_ref[0, c] for c in range(n)], axis=1)
    sc = ((sc + bias).reshape(A_HEADS, T_PAD, n * TK) + mask[None]).reshape(rows, n * TK)
    p = jnp.exp(sc - jnp.max(sc, axis=-1, keepdims=True))
    l = jnp.sum(p, axis=-1, keepdims=True)
    p = p.astype(BF16)
    acc = _dot_nt(p[:, 0:TK], vts[0].astype(BF16))
    for j in range(1, n):
        acc = acc + _dot_nt(p[:, j * TK:(j + 1) * TK], vts[j].astype(BF16))
    o = acc / l
    for h in range(A_HEADS):
        g0 = (h // A_REP) * A_HEAD_DIM
        o_ref[0, :, h * A_HEAD_DIM:(h + 1) * A_HEAD_DIM] = o[h * T_PAD:(h + 1) * T_PAD, g0:g0 + A_HEAD_DIM]


def _dsa_sample_attn(page_table, q_bd, k_new_t, v_new_t, selb, btab, cache_kt, cache_vt):
    db, npages = page_table.shape
    rows = A_HEADS * T_PAD
    per_b = lambda shape: pl.BlockSpec((1,) + shape, lambda b, pt: (b,) + (0,) * len(shape))
    grid_spec = pltpu.PrefetchScalarGridSpec(
        num_scalar_prefetch=1,
        grid=(db,),
        in_specs=[per_b((rows, A_KV)), per_b((A_KV, TK)), per_b((A_KV, TK)), per_b((npages + 1, T_PAD, TK)),
                  pl.BlockSpec(btab.shape, lambda b, pt: (0, 0, 0))]
        + _all_page_specs(npages, A_KV, TK) + _all_page_specs(npages, A_KV, TK),
        out_specs=per_b((T_PAD, A_Q)),
    )
    return pl.pallas_call(
        functools.partial(_dsa_sample_attn_kernel, npages=npages),
        out_shape=jax.ShapeDtypeStruct((db, T_PAD, A_Q), F32),
        grid_spec=grid_spec,
        compiler_params=_cparams(("parallel",)),
        name="dsa_sample_attn",
    )(page_table, q_bd, k_new_t, v_new_t, selb, btab, *([cache_kt] * npages), *([cache_vt] * npages))


def _diff_sample_kernel(pt_ref, q_ref, knew_ref, vnew_ref, btab_ref, lp_ref, sg_ref, *rest, pps, npages, lam_init):
    del pt_ref
    k_refs = rest[:pps]
    v_refs = rest[pps:2 * pps]
    o_ref, m_sc, l_sc, acc_sc = rest[2 * pps:]
    s = pl.program_id(1)

    @pl.when(s == 0)
    def _():
        _init_flash(m_sc, l_sc, acc_sc)

    scale = B_HEAD_DIM ** -0.5
    q = (q_ref[0] * scale).astype(BF16)
    hrows = 2 * T_PAD

    def step(kt_list, v_fn, bias):
        n = len(kt_list)
        sc = jnp.concatenate([_dot(q, kt.astype(BF16)) for kt in kt_list], axis=1) + bias

        def pv(p):
            outs = []
            for h in range(B_HEADS):
                ph = p[h * hrows:(h + 1) * hrows]
                out = _dot(ph[:, 0:TK], v_fn(0, h))
                for j in range(1, n):
                    out = out + _dot(ph[:, j * TK:(j + 1) * TK], v_fn(j, h))
                outs.append(out)
            return jnp.concatenate(outs, axis=0)

        _flash_update(0, sc, pv, m_sc, l_sc, acc_sc)

    def page_v(j, h):
        return v_refs[j][0, pl.ds(h, TK, stride=B_HEADS), :].astype(BF16)

    step([r[0] for r in k_refs], page_v, _page_bias(btab_ref, s * pps, pps, npages))

    @pl.when(s == pl.num_programs(1) - 1)
    def _():
        step([knew_ref[0]], lambda j, h: vnew_ref[0, :, h * B_VH:(h + 1) * B_VH].astype(BF16), btab_ref[2])
        o = acc_sc[0] / l_sc[0]
        lam = _diff_lambda(lp_ref, lam_init)
        for h in range(B_HEADS):
            r0 = h * hrows
            o_ref[0, :, h * B_VH:(h + 1) * B_VH] = _diff_finish(
                o[r0:r0 + T_PAD], o[r0 + T_PAD:r0 + hrows], lam, sg_ref[...], lam_init)


def _diff_sample(page_table, q_bd, k_new_t, v_new, btab, lam_p, subln, cache_kt, cache_v2, lam_init, pps):
    db, npages = page_table.shape
    rows = 2 * B_HEADS * T_PAD
    per_b = lambda shape: pl.BlockSpec((1,) + shape, lambda b, s, pt: (b,) + (0,) * len(shape))
    const = lambda shape: pl.BlockSpec(shape, lambda b, s, pt: (0,) * len(shape))
    grid_spec = pltpu.PrefetchScalarGridSpec(
        num_scalar_prefetch=1,
        grid=(db, npages // pps),
        in_specs=[per_b((rows, B_QK)), per_b((B_QK, TK)), per_b((TK, B_V)), const(btab.shape),
                  const(lam_p.shape), const(subln.shape)]
        + _page_specs(pps, B_QK, TK) + _page_specs(pps, TK * B_HEADS, B_VH),
        out_specs=per_b((T_PAD, B_V)),
        scratch_shapes=_flash_scratch(1, rows, B_VH),
    )
    return pl.pallas_call(
        functools.partial(_diff_sample_kernel, pps=pps, npages=npages, lam_init=lam_init),
        out_shape=jax.ShapeDtypeStruct((db, T_PAD, B_V), F32),
        grid_spec=grid_spec,
        compiler_params=_cparams(("parallel", "arbitrary")),
        name="diff_sample",
    )(page_table, q_bd, k_new_t, v_new, btab, lam_p, subln, *([cache_kt] * pps), *([cache_v2] * pps))


def _mid_kernel(x_ref, ma_ref, mb_ref, mk_ref, mv_ref, wout_ref, gx_ref, wq_ref, gm_ref, qg_ref, wo_ref, h2_ref):
    group, tm, d = x_ref.shape
    rows = lambda ref: ref[...].reshape(group * tm, ref.shape[-1])
    h = (rows(x_ref) + _dot(rows(ma_ref).astype(BF16), wout_ref[:A_Q, :])
         + _dot(rows(mb_ref).astype(BF16), wout_ref[A_Q:, :]))
    hn = _rms(h) * gx_ref[...]
    q = _dot(hn.astype(BF16), wq_ref[...])
    ms = _dot((q * q).astype(BF16), gm_ref[...])
    q = (q * lax.rsqrt(ms + EPS) * qg_ref[...]).astype(BF16)
    m_tok = mk_ref.shape[1] // MEM_HEADS
    o_rows = []
    for g in range(group):
        outs = []
        for hh in range(MEM_HEADS):
            sl = slice(hh * MEM_HEAD_DIM, (hh + 1) * MEM_HEAD_DIM)
            mk = mk_ref[g, pl.ds(hh, m_tok, stride=MEM_HEADS), :].astype(BF16)
            mv = mv_ref[g, pl.ds(hh, m_tok, stride=MEM_HEADS), :].astype(BF16)
            s = _dot_nt(q[g * tm:(g + 1) * tm, sl], mk) * (MEM_HEAD_DIM ** -0.5)
            p = jnp.exp(s - jnp.max(s, axis=-1, keepdims=True))
            l = jnp.sum(p, axis=-1, keepdims=True)
            outs.append(_dot(p.astype(BF16), mv) / l)
        o_rows.append(jnp.concatenate(outs, axis=-1))
    o = jnp.concatenate(o_rows, axis=0)
    h2_ref[...] = (h + _dot(o.astype(BF16), wo_ref[...])).reshape(group, tm, d)


def _mid(x, mix_a, mix_b, mk, mv, w_out, g_x, w_q, gmat, q_gain, w_o, tm, group):
    b, s, d = x.shape
    blk = lambda w: pl.BlockSpec((group, tm, w), lambda bi, i: (bi, i, 0))
    per_b = lambda w: pl.BlockSpec((group, mk.shape[1], w), lambda bi, i: (bi, 0, 0))
    return pl.pallas_call(
        _mid_kernel,
        out_shape=jax.ShapeDtypeStruct((b, s, d), F32),
        grid=(b // group, s // tm),
        in_specs=[blk(d), blk(A_Q), blk(B_V), per_b(MEM_HEAD_DIM), per_b(MEM_HEAD_DIM), _const_spec(w_out.shape),
                  _const_spec(g_x.shape), _const_spec(w_q.shape), _const_spec(gmat.shape),
                  _const_spec(q_gain.shape), _const_spec(w_o.shape)],
        out_specs=blk(d),
        compiler_params=_cparams(("parallel", "arbitrary")),
        name="mid",
    )(x, mix_a, mix_b, mk, mv, w_out, g_x, w_q, gmat, q_gain, w_o)


def _ffn_core(h, gn, wg_ref, wu_ref, cw_ref, cb_ref, wd_ref, shifted):
    xb = (_rms(h) * gn).astype(BF16)
    g = _dot(xb, wg_ref[...])
    u = _dot(xb, wu_ref[...])
    gm1, gm2 = shifted(g)
    gc = cb_ref[...] + cw_ref[0:1, :] * gm2 + cw_ref[1:2, :] * gm1 + cw_ref[2:3, :] * g
    a = gc / (1.0 + jnp.exp(-gc)) * u
    return h + _dot(a.astype(BF16), wd_ref[...]), g


def _ffn_prompt_kernel(h_ref, gn_ref, wg_ref, wu_ref, cw_ref, cb_ref, wd_ref, y_ref, tail_ref, carry_ref):
    @pl.when(pl.program_id(1) == 0)
    def _():
        carry_ref[...] = jnp.zeros(carry_ref.shape, F32)

    tm = h_ref.shape[1]
    row = lax.broadcasted_iota(I32, (tm, 1), 0)
    c0 = carry_ref[SUBLANES - 2:SUBLANES - 1, :]
    c1 = carry_ref[SUBLANES - 1:SUBLANES, :]

    def shifted(g):
        gm1 = jnp.where(row == 0, c1, pltpu.roll(g, 1, 0))
        gm2 = jnp.where(row == 0, c0, jnp.where(row == 1, c1, pltpu.roll(g, 2, 0)))
        return gm1, gm2

    y, g = _ffn_core(h_ref[0], gn_ref[...], wg_ref, wu_ref, cw_ref, cb_ref, wd_ref, shifted)
    y_ref[0] = y
    tail = g[tm - SUBLANES:, :]
    carry_ref[...] = tail
    tail_ref[0] = tail


def _ffn_prompt(h, gn, wg, wu, cw, cb, wd, tm):
    b, s, d = h.shape
    f = wg.shape[1]
    blk = pl.BlockSpec((1, tm, d), lambda bi, i: (bi, i, 0))
    wspec = lambda shape: _const_spec(shape, single_buffer=True)
    return pl.pallas_call(
        _ffn_prompt_kernel,
        out_shape=[jax.ShapeDtypeStruct((b, s, d), F32), jax.ShapeDtypeStruct((b, SUBLANES, f), F32)],
        grid=(b, s // tm),
        in_specs=[blk, _const_spec(gn.shape), wspec(wg.shape), wspec(wu.shape), _const_spec(cw.shape),
                  _const_spec(cb.shape), wspec(wd.shape)],
        out_specs=[blk, pl.BlockSpec((1, SUBLANES, f), lambda bi, i: (bi, 0, 0))],
        scratch_shapes=[pltpu.VMEM((SUBLANES, f), F32)],
        compiler_params=_cparams(("arbitrary", "arbitrary")),
        name="ffn_prompt",
    )(h, gn, wg, wu, cw, cb, wd)


def _ffn_sample_kernel(h_ref, gn_ref, wg_ref, wu_ref, cw_ref, cb_ref, wd_ref, st1_ref, st2_ref, y_ref, g_ref):
    m = h_ref.shape[0]
    t = lax.broadcasted_iota(I32, (m, 1), 0) & (T_PAD - 1)

    def shifted(g):
        gm1 = jnp.where(t == 0, st1_ref[...], pltpu.roll(g, 1, 0))
        gm2 = jnp.where(t < 2, st2_ref[...], pltpu.roll(g, 2, 0))
        return gm1, gm2

    y, g = _ffn_core(h_ref[...], gn_ref[...], wg_ref, wu_ref, cw_ref, cb_ref, wd_ref, shifted)
    y_ref[...] = y
    g_ref[...] = g


def _ffn_sample(h2d, gn, wg, wu, cw, cb, wd, st1, st2):
    m, d = h2d.shape
    f = wg.shape[1]
    wspec = lambda shape: _const_spec(shape, single_buffer=True)
    return pl.pallas_call(
        _ffn_sample_kernel,
        out_shape=[jax.ShapeDtypeStruct((m, d), F32), jax.ShapeDtypeStruct((m, f), F32)],
        grid=(1,),
        in_specs=[_const_spec((m, d)), _const_spec(gn.shape), wspec(wg.shape), wspec(wu.shape),
                  _const_spec(cw.shape), _const_spec(cb.shape), wspec(wd.shape), _const_spec((m, f)),
                  _const_spec((m, f))],
        out_specs=[_const_spec((m, d)), _const_spec((m, f))],
        compiler_params=_cparams(("arbitrary",)),
        name="ffn_sample",
    )(h2d, gn, wg, wu, cw, cb, wd, st1, st2)


def _rel_bucket(dist):
    n = np.maximum(dist, 0)
    max_exact = NUM_BUCKETS // 2
    nf = np.maximum(n, 1).astype(np.float32)
    log_b = (np.log(nf / np.float32(max_exact)) / np.float32(math.log(MAX_DISTANCE / max_exact))
             * np.float32(NUM_BUCKETS - max_exact))
    large = np.minimum(max_exact + log_b.astype(np.int32), NUM_BUCKETS - 1)
    return np.where(n < max_exact, n, large)


def _bias_by_dist(dist, causal, bias):
    onehot = (_rel_bucket(dist)[..., None] == np.arange(NUM_BUCKETS)).astype(np.float32)
    vals = jnp.einsum("...k,kh->h...", onehot, bias, precision=lax.Precision.HIGHEST)
    return jnp.where(causal[None], vals, NEG).astype(F32)


def _prompt_bias_tables(bias):
    r = np.arange(TK)[:, None]
    c = np.arange(TK)[None, :]
    always = np.ones((TK, TK), bool)
    masked = _bias_by_dist(r - c, ~always, bias)
    t0 = _bias_by_dist(r - c, r >= c, bias)
    t1 = _bias_by_dist(r - c + TK, always, bias)
    t2 = _bias_by_dist(r - c + 2 * TK, always, bias)
    return jnp.swapaxes(jnp.stack([masked, t0, t1, t2], axis=1), -1, -2) * LOG2E


def _sample_bias_tables(bias, streams_per_head):
    t = np.arange(T_PAD)[:, None]
    c = np.arange(TK)[None, :]
    always = np.ones((T_PAD, TK), bool)
    far = _bias_by_dist(t - c + 2 * TK, always, bias)
    last = _bias_by_dist(t - c + TK, always, bias)
    new = _bias_by_dist(t - c, c <= t, bias)
    tabs = jnp.stack([far, last, new], axis=0)
    tabs = jnp.repeat(tabs[:, :, None], streams_per_head, axis=2)
    return tabs.reshape(3, -1, TK)


def _group_mean_matrix(width, group):
    idx = jnp.arange(width) // group
    return jnp.where(idx[:, None] == idx[None, :], 1.0 / group, 0.0).astype(BF16)


def _pad_rows(x, rows):
    return jnp.pad(x, ((0, 0), (0, rows - x.shape[1]), (0, 0)))


def _new_keys_t(x, rows):
    return jnp.swapaxes(_pad_rows(x, rows), 1, 2)


def kernel(x_prompt, x_sample, mem_prompt, cache_a_k, cache_a_v, cache_idx_k, cache_b_k, cache_b_v, cache_mem_k, cache_mem_v, state_ffn_conv, page_table, rel_bias, norm_mix, w_in, a_q_norm, a_k_norm, b_q_norm, b_k_norm, diff_lambda, diff_subln, w_out, norm_mem_x, norm_mem_src, w_mem_q, w_mem_kv, mem_q_norm, mem_k_norm, w_mem_o, norm_ffn, w_up, w_gate, ffn_conv_w, ffn_conv_b, w_down):
    depth = w_in.shape[0]
    assert depth == 1, "single-layer trunk"
    layer = 0
    lam_init = 0.8 - 0.6 * math.exp(-0.3 * layer)
    b, s, d = x_prompt.shape
    db, t_new, _ = x_sample.shape
    assert CONV_W - 1 <= t_new <= T_PAD and cache_a_k.shape[2] == TK
    m_tok = mem_prompt.shape[1]
    f = w_up.shape[-1]
    n_pool = cache_a_k.shape[1]
    tkb = min(TKB_MAX, s)
    tkb_diff = min(TKB_DIFF, s)
    assert s % tkb == 0 and tkb % TK == 0 and s % tkb_diff == 0 and tkb_diff % TK == 0

    w_in_l = w_in[layer]
    n_front = A_Q + 2 * A_KV + IDX_Q + IDX_DIM + IDX_HEADS
    w_in_p = jnp.concatenate(
        [w_in_l[:, :n_front], jnp.zeros((d, C_QB - n_front), F32), w_in_l[:, n_front:]], axis=1).astype(BF16)
    ones = lambda n: jnp.ones((n,), F32)
    hgain = jnp.concatenate([
        jnp.tile(a_q_norm[layer], A_HEADS), jnp.tile(a_k_norm[layer], A_KV_HEADS), ones(C_QB - C_VA),
        jnp.tile(b_q_norm[layer], 2 * B_HEADS), jnp.tile(b_k_norm[layer], 2 * B_HEADS), ones(B_V)])[None, :]
    gmat64 = _group_mean_matrix(A_Q, A_HEAD_DIM)
    gmat128 = _group_mean_matrix(MEM_W, MEM_HEAD_DIM)
    proj_segs = ((C_QA, A_Q, True, A_Q, 1), (C_KA, A_KV, True, A_KV, 1), (C_VA, A_KV, False, A_KV, 1),
                 (C_QI, IDX_Q, False, IDX_Q, 1), (C_KIW, LANES, False, LANES, 1), (C_KIW, LANES, False, IDX_DIM, 1),
                 (C_QB, B_QK, True, B_QK, 1), (C_KB, B_QK, True, B_QK, 1), (C_VB, B_V, False, B_V, 1))
    prompt_segs = ((C_QA, A_Q, True, A_Q, 1), (C_KA, A_KV, True, A_KV, 1), (C_QI, IDX_Q, False, IDX_Q, 1),
                   (C_KIW, LANES, False, LANES, 1), (C_QB, B_QK, True, B_QK, 1), (C_KB, B_QK, True, B_QK, 1),
                   (C_VA, A_KV, False, A_KV, -tkb), (C_VB, B_V, False, B_V, -tkb_diff),
                   (C_KA, A_KV, True, A_KV, FEATURE_MAJOR), (C_VA, A_KV, False, A_KV, FEATURE_MAJOR),
                   (C_KIW, LANES, False, IDX_DIM, FEATURE_MAJOR), (C_KB, B_QK, True, B_QK, FEATURE_MAJOR),
                   (C_VB, B_V, False, B_V, B_HEADS))
    g_mix = norm_mix[layer][None, :]
    w_out_b = w_out[layer].astype(BF16)
    w_q_b = w_mem_q[layer].astype(BF16)
    w_o_b = w_mem_o[layer].astype(BF16)
    w_kv_b = w_mem_kv[layer].astype(BF16)
    w_gate_b = w_gate[layer].astype(BF16)
    w_up_b = w_up[layer].astype(BF16)
    w_down_b = w_down[layer].astype(BF16)
    g_memx = norm_mem_x[layer][None, :]
    q_gain = jnp.tile(mem_q_norm[layer], MEM_HEADS)[None, :]
    kv_gain = jnp.concatenate([jnp.tile(mem_k_norm[layer], MEM_HEADS), ones(MEM_W)])[None, :]
    g_ffn = norm_ffn[layer][None, :]
    conv_w = ffn_conv_w[layer]
    conv_b = ffn_conv_b[layer][None, :]
    lam_p = diff_lambda[layer]
    subln = diff_subln[layer][None, :]
    bias_a = rel_bias[:, :A_HEADS]
    bias_b = rel_bias[:, A_HEADS:]

    assert TM_TOKENS % tkb == 0 and TM_TOKENS % tkb_diff == 0, "value chunks must tile the projection's token tile"
    qa, ka, qi, kiw, qb, kb, va_ch, vb_ch, ka_t, va_t, ki_t, kb_t, vb_c = _norm_proj(
        x_prompt.reshape(b * s, d), g_mix, w_in_p, gmat64, hgain, prompt_segs, TM_TOKENS, "proj_prompt", seq=s)

    def from_feature_major(a_t, shape):
        nd = len(shape)
        return jnp.transpose(a_t.reshape((b,) + shape + (s,)), (0, nd + 1) + tuple(range(1, nd + 1)))[None]
    r3 = lambda a: a.reshape(b, s, a.shape[-1])
    mix_a = _dsa_prompt(r3(qi), r3(kiw), r3(qa), r3(ka), va_ch, _prompt_bias_tables(bias_a), tkb)
    mix_b = _diff_prompt(r3(qb), r3(kb), vb_ch, _prompt_bias_tables(bias_b), lam_p, subln, lam_init, tkb_diff)
    mk, mv = _norm_proj(mem_prompt.reshape(b * m_tok, d), norm_mem_src[layer][None, :], w_kv_b, gmat128, kv_gain,
                        ((0, MEM_W, True, MEM_W, MEM_HEADS), (MEM_W, MEM_W, False, MEM_W, MEM_HEADS)), 256, "mem_kv")
    mem_rows = lambda a: a.reshape(-1, m_tok * MEM_HEADS, MEM_HEAD_DIM)
    h2 = _mid(x_prompt, mix_a, mix_b, mem_rows(mk), mem_rows(mv), w_out_b, g_memx, w_q_b, gmat128, q_gain, w_o_b,
              TM_TOKENS, 1)
    yp, tail = _ffn_prompt(h2, g_ffn, w_gate_b, w_up_b, conv_w, conv_b, w_down_b, TM_TOKENS)
    conv_p = tail[:, SUBLANES - (CONV_W - 1):, :]

    xs = _pad_rows(x_sample, T_PAD)
    qa, ka_s, va_s, qi, kiw, ki_s, qb, kb_s, vb_s = _norm_proj(
        xs.reshape(db * T_PAD, d), g_mix, w_in_p, gmat64, hgain, proj_segs, db * T_PAD, "proj_sample")
    r3 = lambda a: a.reshape(db, T_PAD, a.shape[-1])
    idx_kt = jnp.transpose(cache_idx_k[layer], (0, 2, 1))
    a_kt = jnp.transpose(cache_a_k[layer], (0, 2, 3, 1)).reshape(n_pool, A_KV, TK)
    a_vt = jnp.transpose(cache_a_v[layer], (0, 2, 3, 1)).reshape(n_pool, A_KV, TK)
    b_kt = jnp.transpose(cache_b_k[layer], (0, 2, 3, 4, 1)).reshape(n_pool, B_QK, TK)
    b_v2 = cache_b_v[layer].reshape(n_pool, TK * B_HEADS, B_VH)
    n_pages = page_table.shape[1]
    qi_rows = r3(qi).reshape(db, T_PAD, IDX_HEADS, IDX_DIM).transpose(0, 2, 1, 3).reshape(db, IDX_HEADS * T_PAD, IDX_DIM)
    wi_rows = r3(kiw)[:, :, IDX_DIM:IDX_DIM + IDX_HEADS].transpose(0, 2, 1).reshape(db, IDX_HEADS * T_PAD, 1)
    selb = _dsa_sample_select(page_table, qi_rows, wi_rows, _new_keys_t(r3(ki_s), TK), idx_kt, t_new,
                              math.gcd(n_pages, SELECT_PAGES_PER_STEP))
    q_rows = r3(qa).reshape(db, T_PAD, A_HEADS, A_HEAD_DIM).transpose(0, 2, 1, 3)
    eye_g = jnp.repeat(jnp.eye(A_KV_HEADS, dtype=F32), A_REP, axis=0)
    qa_bd = (q_rows[:, :, :, None, :] * eye_g[None, :, None, :, None]).reshape(db, A_HEADS * T_PAD, A_KV)
    mix_a = _dsa_sample_attn(page_table, qa_bd, _new_keys_t(r3(ka_s), TK), _new_keys_t(r3(va_s), TK), selb,
                             _sample_bias_tables(bias_a, 1), a_kt, a_vt)
    n_str = 2 * B_HEADS
    q_rows = r3(qb).reshape(db, T_PAD, n_str, B_HEAD_DIM).transpose(0, 2, 1, 3)
    qb_bd = (q_rows[:, :, :, None, :] * jnp.eye(n_str, dtype=F32)[None, :, None, :, None]).reshape(
        db, n_str * T_PAD, B_QK)
    mix_b = _diff_sample(page_table, qb_bd, _new_keys_t(r3(kb_s), TK), _pad_rows(r3(vb_s), TK),
                         _sample_bias_tables(bias_b, 2), lam_p, subln, b_kt, b_v2, lam_init,
                         math.gcd(n_pages, DIFF_PAGES_PER_STEP))
    h2 = _mid(xs, mix_a, mix_b, mem_rows(cache_mem_k[layer]), mem_rows(cache_mem_v[layer]), w_out_b, g_memx, w_q_b,
              gmat128, q_gain, w_o_b, T_PAD, math.gcd(db, MID_SEQS_PER_STEP))
    state = state_ffn_conv[layer]
    st1 = _pad_rows(state[:, 1:2, :], T_PAD).reshape(db * T_PAD, f)
    st2 = _pad_rows(state, T_PAD).reshape(db * T_PAD, f)
    ys, g_s = _ffn_sample(h2.reshape(db * T_PAD, d), g_ffn, w_gate_b, w_up_b, conv_w, conv_b, w_down_b, st1, st2)
    ys = ys.reshape(db, T_PAD, d)[:, :t_new]
    conv_s = g_s.reshape(db, T_PAD, f)[:, t_new - (CONV_W - 1):t_new]

    def new_rows(a, shape):
        return a.reshape(db, T_PAD, -1)[:, :t_new].reshape((1, db, t_new) + shape)

    return (yp, ys,
            from_feature_major(ka_t, (A_KV_HEADS, A_HEAD_DIM)), from_feature_major(va_t, (A_KV_HEADS, A_HEAD_DIM)),
            from_feature_major(ki_t, (IDX_DIM,)), from_feature_major(kb_t, (B_HEADS, 2, B_HEAD_DIM)),
            vb_c.reshape(1, b, s, B_HEADS, 2 * B_HEAD_DIM),
            mk.reshape(1, b, m_tok, MEM_HEADS, MEM_HEAD_DIM), mv.reshape(1, b, m_tok, MEM_HEADS, MEM_HEAD_DIM),
            conv_p[None],
            new_rows(ka_s, (A_KV_HEADS, A_HEAD_DIM)), new_rows(va_s, (A_KV_HEADS, A_HEAD_DIM)),
            new_rows(ki_s, (IDX_DIM,)), new_rows(kb_s, (B_HEADS, 2, B_HEAD_DIM)),
            new_rows(vb_s, (B_HEADS, 2 * B_HEAD_DIM)), conv_s[None])
```

```python
import functools
import math

import jax
import jax.numpy as jnp
import numpy as np
from jax import lax
from jax.experimental import pallas as pl
from jax.experimental.pallas import tpu as pltpu

F32 = jnp.float32
BF16 = jnp.bfloat16
I32 = jnp.int32

EPS = 1e-6
NEG = -1e30
LOG2E = math.log2(math.e)
INT_MIN = -(2 ** 31)

A_HEADS = 8
A_KV_HEADS = 2
A_HEAD_DIM = 64
IDX_HEADS = 4
IDX_DIM = 64
TOPK_MAX = 256
B_HEADS = 4
B_HEAD_DIM = 64
MEM_HEADS = 4
MEM_HEAD_DIM = 128
CONV_W = 3
NUM_BUCKETS = 32
MAX_DISTANCE = 128

A_Q = A_HEADS * A_HEAD_DIM
A_KV = A_KV_HEADS * A_HEAD_DIM
IDX_Q = IDX_HEADS * IDX_DIM
B_QK = B_HEADS * 2 * B_HEAD_DIM
B_V = B_HEADS * 2 * B_HEAD_DIM
B_VH = 2 * B_HEAD_DIM
MEM_W = MEM_HEADS * MEM_HEAD_DIM
A_REP = A_HEADS // A_KV_HEADS

LANES = 128
SUBLANES = 8
TQ_DSA = 512
TQ_DIFF = 512
TK = 128
TKB_MAX = 512
T_PAD = SUBLANES
TM_TOKENS = 512
FEATURE_MAJOR = 0
TKB_DIFF = 512
MID_SEQS_PER_STEP = 8
SELECT_PAGES_PER_STEP = 64
DIFF_PAGES_PER_STEP = 32
VMEM_LIMIT = 56 * 1024 * 1024

C_QA = 0
C_KA = C_QA + A_Q
C_VA = C_KA + A_KV
C_QI = C_VA + A_KV
C_KIW = C_QI + IDX_Q
C_QB = C_KIW + LANES
C_KB = C_QB + B_QK
C_VB = C_KB + B_QK
D_IN_PAD = C_VB + B_V


def _dot(a, b):
    return jnp.dot(a, b, preferred_element_type=F32)


def _dot_nt(a, b):
    return lax.dot_general(a, b, (((1,), (1,)), ((), ())), preferred_element_type=F32)


def _rms(x):
    return x * lax.rsqrt(jnp.mean(x * x, axis=-1, keepdims=True) + EPS)


def _group_mean_sq(x, gm_ref):
    sq = (x * x).astype(BF16)
    gm = gm_ref[:LANES, :LANES]
    return jnp.concatenate([_dot(sq[:, j:j + LANES], gm) for j in range(0, x.shape[1], LANES)], axis=1)


KEY_NEG_INF = -(2 ** 31) + 0x7FFFFF


def _key_to_float(key):
    bits = key ^ ((key >> 31) & 0x7FFFFFFF)
    return jnp.where(key <= KEY_NEG_INF, -jnp.inf, lax.bitcast_convert_type(bits, F32))


def _kth_largest(count_ge, shape, topk):
    def bit_body(it, key):
        cand = key ^ lax.shift_left(jnp.int32(1), 31 - it)
        return jnp.where(count_ge(_key_to_float(cand)) >= topk, cand, key)

    return _key_to_float(lax.fori_loop(0, 32, bit_body, jnp.full(shape, INT_MIN, I32)))


def _cparams(sem, vmem=VMEM_LIMIT):
    return pltpu.CompilerParams(dimension_semantics=sem, vmem_limit_bytes=vmem)


def _const_spec(shape, single_buffer=False):
    nd = len(shape)
    if single_buffer:
        return pl.BlockSpec(shape, lambda *_: (0,) * nd, pipeline_mode=pl.Buffered(1))
    return pl.BlockSpec(shape, lambda *_: (0,) * nd)


def _norm_proj_kernel(x_ref, g_ref, w_ref, gm_ref, hg_ref, *out_refs, segs):
    xn = _rms(x_ref[...]) * g_ref[...]
    p = _dot(xn.astype(BF16), w_ref[...])
    tm = x_ref.shape[0]
    done = {}
    for (start, width, norm, out_width, split), o_ref in zip(segs, out_refs):
        if (start, width, norm) not in done:
            s = p[:, start:start + width]
            if norm:
                s = s * lax.rsqrt(_group_mean_sq(s, gm_ref) + EPS) * hg_ref[:, start:start + width]
            done[(start, width, norm)] = s
        s = done[(start, width, norm)]
        if split == 1:
            o_ref[...] = s[:, :out_width]
        elif split == FEATURE_MAJOR:
            o_ref[0] = s.T[:out_width, :]
        elif split < 0:
            st = s.T
            for u in range(tm // -split):
                o_ref[0, u] = st[:out_width, u * -split:(u + 1) * -split]
        else:
            pw = out_width // split
            for j in range(split):
                o_ref[pl.ds(j, tm, stride=split), :] = s[:, j * pw:(j + 1) * pw]


def _norm_proj(x2d, gain, w, gmat, hgain, segs, tm, name, seq=None):
    m, d = x2d.shape
    n = w.shape[1]
    tiles = None if seq is None else seq // tm

    def shape_spec(ow, sp):
        if sp == FEATURE_MAJOR:
            return (jax.ShapeDtypeStruct((m // seq, ow, seq), F32),
                    pl.BlockSpec((1, ow, tm), lambda i: (i // tiles, 0, i % tiles)))
        if sp < 0:
            return (jax.ShapeDtypeStruct((m // seq, seq // -sp, ow, -sp), F32),
                    pl.BlockSpec((1, tm // -sp, ow, -sp), lambda i: (i // tiles, i % tiles, 0, 0)))
        return (jax.ShapeDtypeStruct((m * sp, ow // sp), F32), pl.BlockSpec((tm * sp, ow // sp), lambda i: (i, 0)))

    shapes, specs = zip(*[shape_spec(ow, sp) for (_, _, _, ow, sp) in segs])
    return pl.pallas_call(
        functools.partial(_norm_proj_kernel, segs=segs),
        out_shape=list(shapes),
        grid=(m // tm,),
        in_specs=[pl.BlockSpec((tm, d), lambda i: (i, 0)), _const_spec((1, d)), _const_spec((d, n)),
                  _const_spec(gmat.shape), _const_spec((1, n))],
        out_specs=list(specs),
        compiler_params=_cparams(("parallel",)),
        name=name,
    )(x2d, gain, w, gmat, hgain)


def _init_flash(m_sc, l_sc, acc_sc):
    m_sc[...] = jnp.full(m_sc.shape, -jnp.inf, F32)
    l_sc[...] = jnp.zeros(l_sc.shape, F32)
    acc_sc[...] = jnp.zeros(acc_sc.shape, F32)


def _flash_update(st, s, pv_fn, m_sc, l_sc, acc_sc):
    m_old = m_sc[st]
    m_new = jnp.maximum(m_old, jnp.max(s, axis=-1, keepdims=True))
    alpha = jnp.exp(m_old - m_new)
    p = jnp.exp(s - m_new)
    l_sc[st] = alpha * l_sc[st] + jnp.sum(p, axis=-1, keepdims=True)
    acc_sc[st] = alpha * acc_sc[st] + pv_fn(p.astype(BF16))
    m_sc[st] = m_new


def _flash_scratch(streams, rows, dv):
    return [pltpu.VMEM((streams, rows, 1), F32), pltpu.VMEM((streams, rows, 1), F32),
            pltpu.VMEM((streams, rows, dv), F32)]


def _select_mask(sk, thr, need, before):
    need = jnp.where(thr > -jnp.inf, need, 0.0)
    return jnp.where(sk > thr, 0.0, jnp.where(sk == thr, jnp.where(before < need, 0.0, NEG), NEG))


def _chunk_bias(dtab_ref, h, i, c, ksub, tq):
    qsub = tq // TK
    return jnp.concatenate(
        [jnp.concatenate([dtab_ref[h, jnp.clip(i * qsub + a - (c * ksub + j) + 1, 0, 3)] for a in range(qsub)], axis=1)
         for j in range(ksub)], axis=0)


def _flash_update_t(st, s, vt, m_sc, l_sc, acc_sc):
    m_old = m_sc[st]
    m_new = jnp.maximum(m_old, jnp.max(_col_reduce(s, jnp.max), axis=0, keepdims=True))
    alpha = jnp.exp2(m_old - m_new)
    p = jnp.exp2(s - m_new)
    l_sc[st] = alpha * l_sc[st] + jnp.sum(_col_reduce(p, jnp.sum), axis=0, keepdims=True)
    acc_sc[st] = alpha * acc_sc[st] + _dot(vt, p.astype(BF16))
    m_sc[st] = m_new


def _flash_scratch_t(streams, dv, cols):
    return [pltpu.VMEM((streams, 1, cols), F32), pltpu.VMEM((streams, 1, cols), F32),
            pltpu.VMEM((streams, dv, cols), F32)]


REDUCE_WAYS = 8


def _col_reduce(x, op):
    rows, cols = x.shape
    slabs = rows // SUBLANES
    if slabs % REDUCE_WAYS == 0 and slabs > REDUCE_WAYS:
        x = op(x.reshape(REDUCE_WAYS, slabs // REDUCE_WAYS, SUBLANES, cols), axis=1)
    else:
        x = x.reshape(slabs, SUBLANES, cols)
    return op(x, axis=0)


def _col_count(w):
    return _col_reduce(w, jnp.sum)


def _dsa_prompt_kernel(qi_ref, wit_ref, kiw_ref, qa_ref, ka_ref, vat_ref, dtab_ref, tril_ref, o_ref,
                       skey_ref, selb_ref, m_sc, l_sc, acc_sc, *, topk, tkb):
    tq = TQ_DSA
    i = pl.program_id(1)
    ksub = tkb // TK
    nbig = (i * tq + tq - 1) // tkb + 1
    qi = qi_ref[0].astype(BF16)
    wit = wit_ref[0] * (IDX_HEADS ** -0.5 * IDX_DIM ** -0.5)
    krow = lax.broadcasted_iota(I32, (tkb, tq), 0)
    qcol = lax.broadcasted_iota(I32, (tkb, tq), 1)

    def score_body(c, carry):
        off = pl.multiple_of(c * tkb, tkb)
        kc = kiw_ref[0, pl.ds(off, tkb), 0:IDX_DIM].astype(BF16)
        sc = jnp.zeros((tkb, tq), F32)
        for h in range(IDX_HEADS):
            d = _dot_nt(kc, qi[:, h * IDX_DIM:(h + 1) * IDX_DIM])
            sc = sc + jnp.maximum(d, 0.0) * wit[h:h + 1, :]
        causal = (c * tkb + krow) <= (i * tq + qcol)
        skey_ref[c] = jnp.where(causal, sc, -jnp.inf)
        return carry

    lax.fori_loop(0, nbig, score_body, 0)

    def count(pred_fn):
        def body(c, cnt):
            return cnt + _col_count(jnp.where(pred_fn(skey_ref[c]), 1.0, 0.0))
        cnt = lax.fori_loop(0, nbig, body, jnp.zeros((SUBLANES, tq), F32))
        return jnp.sum(cnt, axis=0, keepdims=True)

    thr = _kth_largest(lambda t: count(lambda sk: sk >= t), (1, tq), topk)
    need = float(topk) - count(lambda sk: sk > thr)
    tril = tril_ref[...]

    def sel_body(c, off):
        sk = skey_ref[c]
        eqf = jnp.where(sk == thr, 1.0, 0.0)
        before = _dot(tril, eqf.astype(BF16)) + off
        selb_ref[c] = _select_mask(sk, thr, need, before)
        return off + jnp.sum(_col_count(eqf), axis=0, keepdims=True)

    lax.fori_loop(0, nbig, sel_body, jnp.zeros((1, tq), F32))

    _init_flash(m_sc, l_sc, acc_sc)
    scale = A_HEAD_DIM ** -0.5 * LOG2E
    qg = [jnp.concatenate([qa_ref[0, :, h * A_HEAD_DIM:(h + 1) * A_HEAD_DIM] * scale
                           for h in range(g * A_REP, (g + 1) * A_REP)], axis=0).astype(BF16)
          for g in range(A_KV_HEADS)]

    def att_body(c, carry):
        off = pl.multiple_of(c * tkb, tkb)
        maskb = selb_ref[c]
        for g in range(A_KV_HEADS):
            g0 = g * A_HEAD_DIM
            kc = ka_ref[0, pl.ds(off, tkb), g0:g0 + A_HEAD_DIM].astype(BF16)
            vt = vat_ref[0, c, g0:g0 + A_HEAD_DIM, :].astype(BF16)
            sg = _dot_nt(kc, qg[g])
            for r in range(A_REP):
                h = g * A_REP + r
                s = sg[:, r * tq:(r + 1) * tq] + (_chunk_bias(dtab_ref, h, i, c, ksub, tq) + maskb)
                _flash_update_t(h, s, vt, m_sc, l_sc, acc_sc)
        return carry

    lax.fori_loop(0, nbig, att_body, 0)
    for h in range(0, A_HEADS, 2):
        ot = jnp.concatenate([acc_sc[h] / l_sc[h], acc_sc[h + 1] / l_sc[h + 1]], axis=0)
        o_ref[0, :, h * A_HEAD_DIM:(h + 2) * A_HEAD_DIM] = ot.T


def _dsa_prompt(qi, kiw, qa, ka, vat, dtab, tkb):
    b, s, _ = qa.shape
    tq = TQ_DSA
    nq = s // tq
    nc = s // tkb
    topk = min(TOPK_MAX, s // 4)
    tril = (jnp.arange(tkb)[None, :] < jnp.arange(tkb)[:, None]).astype(BF16)
    wit = jnp.swapaxes(kiw[:, :, IDX_DIM:IDX_DIM + SUBLANES], 1, 2)
    blk = lambda w: pl.BlockSpec((1, tq, w), lambda bi, i: (bi, i, 0))
    full = lambda w: pl.BlockSpec((1, s, w), lambda bi, i: (bi, 0, 0))
    return pl.pallas_call(
        functools.partial(_dsa_prompt_kernel, topk=topk, tkb=tkb),
        out_shape=jax.ShapeDtypeStruct((b, s, A_Q), F32),
        grid=(b, nq),
        in_specs=[blk(IDX_Q), pl.BlockSpec((1, SUBLANES, tq), lambda bi, i: (bi, 0, i)), full(LANES), blk(A_Q),
                  full(A_KV), pl.BlockSpec((1, nc, A_KV, tkb), lambda bi, i: (bi, 0, 0, 0)),
                  _const_spec(dtab.shape), _const_spec(tril.shape)],
        out_specs=blk(A_Q),
        scratch_shapes=[pltpu.VMEM((nc, tkb, tq), F32), pltpu.VMEM((nc, tkb, tq), F32)]
        + _flash_scratch_t(A_HEADS, A_HEAD_DIM, tq),
        compiler_params=_cparams(("parallel", "arbitrary")),
        name="dsa_prompt",
    )(qi, wit, kiw, qa, ka, vat, dtab, tril)


def _diff_lambda(lp_ref, lam_init):
    lp = lp_ref[...]
    s1 = jnp.sum(lp[0:1] * lp[1:2], axis=-1, keepdims=True)
    s2 = jnp.sum(lp[2:3] * lp[3:4], axis=-1, keepdims=True)
    return jnp.exp(s1) - jnp.exp(s2) + lam_init


def _diff_finish(o0, o1, lam, sg, lam_init):
    o = o0 - lam * o1
    return _rms(o) * sg * (1.0 - lam_init)


def _diff_prompt_kernel(q_ref, k_ref, vt_ref, dtab_ref, lp_ref, sg_ref, o_ref, m_sc, l_sc, acc_sc, *, lam_init, tkb):
    tq = TQ_DIFF
    i = pl.program_id(1)
    ksub = tkb // TK
    nbig = (i * tq + tq - 1) // tkb + 1
    scale = B_HEAD_DIM ** -0.5 * LOG2E
    _init_flash(m_sc, l_sc, acc_sc)

    def body(c, carry):
        off = pl.multiple_of(c * tkb, tkb)
        for h in range(B_HEADS):
            bias = _chunk_bias(dtab_ref, h, i, c, ksub, tq)
            vt = vt_ref[0, c, h * B_VH:(h + 1) * B_VH, :].astype(BF16)
            for comp in range(2):
                st = h * 2 + comp
                c0 = st * B_HEAD_DIM
                qh = (q_ref[0, :, c0:c0 + B_HEAD_DIM] * scale).astype(BF16)
                kc = k_ref[0, pl.ds(off, tkb), c0:c0 + B_HEAD_DIM].astype(BF16)
                _flash_update_t(st, _dot_nt(kc, qh) + bias, vt, m_sc, l_sc, acc_sc)
        return carry

    lax.fori_loop(0, nbig, body, 0)
    lam = _diff_lambda(lp_ref, lam_init)
    for h in range(B_HEADS):
        o0 = (acc_sc[2 * h] / l_sc[2 * h]).T
        o1 = (acc_sc[2 * h + 1] / l_sc[2 * h + 1]).T
        o_ref[0, :, h * B_VH:(h + 1) * B_VH] = _diff_finish(o0, o1, lam, sg_ref[...], lam_init)


def _diff_prompt(qb, kb, vbt, dtab, lam_p, subln, lam_init, tkb):
    b, s, _ = qb.shape
    tq = TQ_DIFF
    nq = s // tq
    nc = s // tkb
    blk = lambda w: pl.BlockSpec((1, tq, w), lambda bi, i: (bi, i, 0))
    full = lambda w: pl.BlockSpec((1, s, w), lambda bi, i: (bi, 0, 0))
    return pl.pallas_call(
        functools.partial(_diff_prompt_kernel, lam_init=lam_init, tkb=tkb),
        out_shape=jax.ShapeDtypeStruct((b, s, B_V), F32),
        grid=(b, nq),
        in_specs=[blk(B_QK), full(B_QK), pl.BlockSpec((1, nc, B_V, tkb), lambda bi, i: (bi, 0, 0, 0)),
                  _const_spec(dtab.shape), _const_spec(lam_p.shape), _const_spec(subln.shape)],
        out_specs=blk(B_V),
        scratch_shapes=_flash_scratch_t(2 * B_HEADS, B_VH, tq),
        compiler_params=_cparams(("parallel", "arbitrary")),
        name="diff_prompt",
    )(qb, kb, vbt, dtab, lam_p, subln)


def _page_specs(pps, rows, cols):
    return [pl.BlockSpec((1, rows, cols), functools.partial(lambda b, s, pt, j: (pt[b, s * pps + j], 0, 0), j=j))
            for j in range(pps)]


def _all_page_specs(npages, rows, cols):
    return [pl.BlockSpec((1, rows, cols), functools.partial(lambda b, pt, j: (pt[b, j], 0, 0), j=j))
            for j in range(npages)]


def _dsa_sample_select_kernel(pt_ref, q_ref, wi_ref, kinew_ref, *rest, pps, npages, topk, group):
    del pt_ref
    page_refs = rest[:pps]
    tri_ref, selb_ref, sc_ref = rest[pps:]
    b = pl.program_id(0)
    s = pl.program_id(1)
    np1 = npages + 1
    q = q_ref[0].astype(BF16)
    wi = wi_ref[0] * (IDX_HEADS ** -0.5 * IDX_DIM ** -0.5)

    def scores(kt):
        return jnp.sum((jnp.maximum(_dot(q, kt), 0.0) * wi).reshape(IDX_HEADS, T_PAD, TK), axis=0)

    base = b * np1
    for j in range(pps):
        sc_ref[base + s * pps + j] = scores(page_refs[j][0].astype(BF16))

    last_page_step = s == pl.num_programs(1) - 1

    @pl.when(last_page_step)
    def _():
        row = lax.broadcasted_iota(I32, (T_PAD, TK), 0)
        col = lax.broadcasted_iota(I32, (T_PAD, TK), 1)
        sc_ref[base + npages] = jnp.where(col <= row, scores(kinew_ref[0].astype(BF16)), -jnp.inf)

    @pl.when(last_page_step & (b == pl.num_programs(0) - 1))
    def _():
        def group_body(gi, carry):
            sk = sc_ref[pl.ds(gi * (group * np1), group * np1)].reshape(group, np1, T_PAD, TK)

            def count(pred):
                cnt = jnp.sum(jnp.where(pred, 1.0, 0.0), axis=1)
                return jnp.sum(cnt, axis=-1, keepdims=True)

            thr = _kth_largest(lambda t: count(sk >= t[:, None]), (group, T_PAD, 1), topk)
            need = float(topk) - count(sk > thr[:, None])
            eqf = jnp.where(sk == thr[:, None], 1.0, 0.0)
            before = _dot(eqf.reshape(group * np1 * T_PAD, TK).astype(BF16), tri_ref[...]).reshape(sk.shape)
            ties = jnp.sum(eqf, axis=-1, keepdims=True)
            off = jnp.zeros((group, T_PAD, 1), F32)
            for c in range(np1):
                selb_ref[pl.ds(gi * group, group), c] = _select_mask(sk[:, c], thr, need, before[:, c] + off)
                off = off + ties[:, c]
            return carry

        lax.fori_loop(0, pl.num_programs(0) // group, group_body, 0)


def _dsa_sample_select(page_table, q_rows, wi_rows, ki_new_t, cache_ikt, n_new, pps):
    db, npages = page_table.shape
    np1 = npages + 1
    topk = min(TOPK_MAX, (npages * TK + n_new) // 4)
    group = math.gcd(db, SUBLANES)
    tri = (jnp.arange(TK)[:, None] < jnp.arange(TK)[None, :]).astype(BF16)
    per_b = lambda shape: pl.BlockSpec((1,) + shape, lambda b, s, pt: (b,) + (0,) * len(shape))
    grid_spec = pltpu.PrefetchScalarGridSpec(
        num_scalar_prefetch=1,
        grid=(db, npages // pps),
        in_specs=[per_b(q_rows.shape[1:]), per_b(wi_rows.shape[1:]), per_b((IDX_DIM, TK))]
        + _page_specs(pps, IDX_DIM, TK) + [pl.BlockSpec(tri.shape, lambda b, s, pt: (0, 0))],
        out_specs=pl.BlockSpec((db, np1, T_PAD, TK), lambda b, s, pt: (0, 0, 0, 0)),
        scratch_shapes=[pltpu.VMEM((db * np1, T_PAD, TK), F32)],
    )
    return pl.pallas_call(
        functools.partial(_dsa_sample_select_kernel, pps=pps, npages=npages, topk=topk, group=group),
        out_shape=jax.ShapeDtypeStruct((db, np1, T_PAD, TK), F32),
        grid_spec=grid_spec,
        compiler_params=_cparams(("arbitrary", "arbitrary")),
        name="dsa_sample_select",
    )(page_table, q_rows, wi_rows, ki_new_t, *([cache_ikt] * pps), tri)


def _page_bias(btab_ref, c0, n, npages):
    return jnp.concatenate([btab_ref[jnp.where(c0 + j == npages - 1, 1, 0)] for j in range(n)], axis=1)


def _dsa_sample_attn_kernel(pt_ref, q_ref, knew_ref, vnew_ref, selb_ref, btab_ref, *rest, npages):
    del pt_ref
    o_ref = rest[2 * npages]
    kts = [r[0] for r in rest[:npages]] + [knew_ref[0]]
    vts = [r[0] for r in rest[npages:2 * npages]] + [vnew_ref[0]]
    n = npages + 1
    rows = A_HEADS * T_PAD
    scale = A_HEAD_DIM ** -0.5
    q = (q_ref[0] * scale).astype(BF16)
    sc = jnp.concatenate([_dot(q, kt.astype(BF16)) for kt in kts], axis=1)
    bias = jnp.concatenate([btab_ref[0]] * (npages - 1) + [btab_ref[1], btab_ref[2]], axis=1)
    mask = jnp.concatenate([selb_ref[0, c] for c in range(n)], axis=1)
    sc = ((sc + bias).reshape(A_HEADS, T_PAD, n * TK) + mask[None]).reshape(rows, n * TK)
    p = jnp.exp(sc - jnp.max(sc, axis=-1, keepdims=True))
    l = jnp.sum(p, axis=-1, keepdims=True)
    p = p.astype(BF16)
    acc = _dot_nt(p[:, 0:TK], vts[0].astype(BF16))
    for j in range(1, n):
        acc = acc + _dot_nt(p[:, j * TK:(j + 1) * TK], vts[j].astype(BF16))
    o = acc / l
    for h in range(A_HEADS):
        g0 = (h // A_REP) * A_HEAD_DIM
        o_ref[0, :, h * A_HEAD_DIM:(h + 1) * A_HEAD_DIM] = o[h * T_PAD:(h + 1) * T_PAD, g0:g0 + A_HEAD_DIM]


def _dsa_sample_attn(page_table, q_bd, k_new_t, v_new_t, selb, btab, cache_kt, cache_vt):
    db, npages = page_table.shape
    rows = A_HEADS * T_PAD
    per_b = lambda shape: pl.BlockSpec((1,) + shape, lambda b, pt: (b,) + (0,) * len(shape))
    grid_spec = pltpu.PrefetchScalarGridSpec(
        num_scalar_prefetch=1,
        grid=(db,),
        in_specs=[per_b((rows, A_KV)), per_b((A_KV, TK)), per_b((A_KV, TK)), per_b((npages + 1, T_PAD, TK)),
                  pl.BlockSpec(btab.shape, lambda b, pt: (0, 0, 0))]
        + _all_page_specs(npages, A_KV, TK) + _all_page_specs(npages, A_KV, TK),
        out_specs=per_b((T_PAD, A_Q)),
    )
    return pl.pallas_call(
        functools.partial(_dsa_sample_attn_kernel, npages=npages),
        out_shape=jax.ShapeDtypeStruct((db, T_PAD, A_Q), F32),
        grid_spec=grid_spec,
        compiler_params=_cparams(("parallel",)),
        name="dsa_sample_attn",
    )(page_table, q_bd, k_new_t, v_new_t, selb, btab, *([cache_kt] * npages), *([cache_vt] * npages))


def _diff_sample_kernel(pt_ref, q_ref, knew_ref, vnew_ref, btab_ref, lp_ref, sg_ref, *rest, pps, npages, lam_init):
    del pt_ref
    k_refs = rest[:pps]
    v_refs = rest[pps:2 * pps]
    o_ref, m_sc, l_sc, acc_sc = rest[2 * pps:]
    s = pl.program_id(1)

    @pl.when(s == 0)
    def _():
        _init_flash(m_sc, l_sc, acc_sc)

    scale = B_HEAD_DIM ** -0.5
    q = (q_ref[0] * scale).astype(BF16)
    hrows = 2 * T_PAD

    def step(kt_list, v_fn, bias):
        n = len(kt_list)
        sc = jnp.concatenate([_dot(q, kt.astype(BF16)) for kt in kt_list], axis=1) + bias

        def pv(p):
            outs = []
            for h in range(B_HEADS):
                ph = p[h * hrows:(h + 1) * hrows]
                out = _dot(ph[:, 0:TK], v_fn(0, h))
                for j in range(1, n):
                    out = out + _dot(ph[:, j * TK:(j + 1) * TK], v_fn(j, h))
                outs.append(out)
            return jnp.concatenate(outs, axis=0)

        _flash_update(0, sc, pv, m_sc, l_sc, acc_sc)

    def page_v(j, h):
        return v_refs[j][0, pl.ds(h, TK, stride=B_HEADS), :].astype(BF16)

    step([r[0] for r in k_refs], page_v, _page_bias(btab_ref, s * pps, pps, npages))

    @pl.when(s == pl.num_programs(1) - 1)
    def _():
        step([knew_ref[0]], lambda j, h: vnew_ref[0, :, h * B_VH:(h + 1) * B_VH].astype(BF16), btab_ref[2])
        o = acc_sc[0] / l_sc[0]
        lam = _diff_lambda(lp_ref, lam_init)
        for h in range(B_HEADS):
            r0 = h * hrows
            o_ref[0, :, h * B_VH:(h + 1) * B_VH] = _diff_finish(
                o[r0:r0 + T_PAD], o[r0 + T_PAD:r0 + hrows], lam, sg_ref[...], lam_init)


def _diff_sample(page_table, q_bd, k_new_t, v_new, btab, lam_p, subln, cache_kt, cache_v2, lam_init, pps):
    db, npages = page_table.shape
    rows = 2 * B_HEADS * T_PAD
    per_b = lambda shape: pl.BlockSpec((1,) + shape, lambda b, s, pt: (b,) + (0,) * len(shape))
    const = lambda shape: pl.BlockSpec(shape, lambda b, s, pt: (0,) * len(shape))
    grid_spec = pltpu.PrefetchScalarGridSpec(
        num_scalar_prefetch=1,
        grid=(db, npages // pps),
        in_specs=[per_b((rows, B_QK)), per_b((B_QK, TK)), per_b((TK, B_V)), const(btab.shape),
                  const(lam_p.shape), const(subln.shape)]
        + _page_specs(pps, B_QK, TK) + _page_specs(pps, TK * B_HEADS, B_VH),
        out_specs=per_b((T_PAD, B_V)),
        scratch_shapes=_flash_scratch(1, rows, B_VH),
    )
    return pl.pallas_call(
        functools.partial(_diff_sample_kernel, pps=pps, npages=npages, lam_init=lam_init),
        out_shape=jax.ShapeDtypeStruct((db, T_PAD, B_V), F32),
        grid_spec=grid_spec,
        compiler_params=_cparams(("parallel", "arbitrary")),
        name="diff_sample",
    )(page_table, q_bd, k_new_t, v_new, btab, lam_p, subln, *([cache_kt] * pps), *([cache_v2] * pps))


def _mid_kernel(x_ref, ma_ref, mb_ref, mk_ref, mv_ref, wout_ref, gx_ref, wq_ref, gm_ref, qg_ref, wo_ref, h2_ref):
    group, tm, d = x_ref.shape
    rows = lambda ref: ref[...].reshape(group * tm, ref.shape[-1])
    h = (rows(x_ref) + _dot(rows(ma_ref).astype(BF16), wout_ref[:A_Q, :])
         + _dot(rows(mb_ref).astype(BF16), wout_ref[A_Q:, :]))
    hn = _rms(h) * gx_ref[...]
    q = _dot(hn.astype(BF16), wq_ref[...])
    q = (q * lax.rsqrt(_group_mean_sq(q, gm_ref) + EPS) * qg_ref[...]).astype(BF16)
    m_tok = mk_ref.shape[1] // MEM_HEADS
    o_rows = []
    for g in range(group):
        outs = []
        for hh in range(MEM_HEADS):
            sl = slice(hh * MEM_HEAD_DIM, (hh + 1) * MEM_HEAD_DIM)
            mk = mk_ref[g, pl.ds(hh, m_tok, stride=MEM_HEADS), :].astype(BF16)
            mv = mv_ref[g, pl.ds(hh, m_tok, stride=MEM_HEADS), :].astype(BF16)
            s = _dot_nt(q[g * tm:(g + 1) * tm, sl], mk) * (MEM_HEAD_DIM ** -0.5)
            p = jnp.exp(s - jnp.max(s, axis=-1, keepdims=True))
            l = jnp.sum(p, axis=-1, keepdims=True)
            outs.append(_dot(p.astype(BF16), mv) / l)
        o_rows.append(jnp.concatenate(outs, axis=-1))
    o = jnp.concatenate(o_rows, axis=0)
    h2_ref[...] = (h + _dot(o.astype(BF16), wo_ref[...])).reshape(group, tm, d)


def _mid(x, mix_a, mix_b, mk, mv, w_out, g_x, w_q, gmat, q_gain, w_o, tm, group):
    b, s, d = x.shape
    blk = lambda w: pl.BlockSpec((group, tm, w), lambda bi, i: (bi, i, 0))
    per_b = lambda w: pl.BlockSpec((group, mk.shape[1], w), lambda bi, i: (bi, 0, 0))
    return pl.pallas_call(
        _mid_kernel,
        out_shape=jax.ShapeDtypeStruct((b, s, d), F32),
        grid=(b // group, s // tm),
        in_specs=[blk(d), blk(A_Q), blk(B_V), per_b(MEM_HEAD_DIM), per_b(MEM_HEAD_DIM), _const_spec(w_out.shape),
                  _const_spec(g_x.shape), _const_spec(w_q.shape), _const_spec(gmat.shape),
                  _const_spec(q_gain.shape), _const_spec(w_o.shape)],
        out_specs=blk(d),
        compiler_params=_cparams(("parallel", "arbitrary")),
        name="mid",
    )(x, mix_a, mix_b, mk, mv, w_out, g_x, w_q, gmat, q_gain, w_o)


def _ffn_core(h, gn, wg_ref, wu_ref, cw_ref, cb_ref, wd_ref, shifted):
    xb = (_rms(h) * gn).astype(BF16)
    g = _dot(xb, wg_ref[...])
    u = _dot(xb, wu_ref[...])
    gm1, gm2 = shifted(g)
    gc = cb_ref[...] + cw_ref[0:1, :] * gm2 + cw_ref[1:2, :] * gm1 + cw_ref[2:3, :] * g
    a = gc / (1.0 + jnp.exp(-gc)) * u
    return h + _dot(a.astype(BF16), wd_ref[...]), g


def _ffn_prompt_kernel(h_ref, gn_ref, wg_ref, wu_ref, cw_ref, cb_ref, wd_ref, y_ref, tail_ref, carry_ref):
    @pl.when(pl.program_id(1) == 0)
    def _():
        carry_ref[...] = jnp.zeros(carry_ref.shape, F32)

    tm = h_ref.shape[1]
    row = lax.broadcasted_iota(I32, (tm, 1), 0)
    c0 = carry_ref[SUBLANES - 2:SUBLANES - 1, :]
    c1 = carry_ref[SUBLANES - 1:SUBLANES, :]

    def shifted(g):
        gm1 = jnp.where(row == 0, c1, pltpu.roll(g, 1, 0))
        gm2 = jnp.where(row == 0, c0, jnp.where(row == 1, c1, pltpu.roll(g, 2, 0)))
        return gm1, gm2

    y, g = _ffn_core(h_ref[0], gn_ref[...], wg_ref, wu_ref, cw_ref, cb_ref, wd_ref, shifted)
    y_ref[0] = y
    tail = g[tm - SUBLANES:, :]
    carry_ref[...] = tail
    tail_ref[0] = tail


def _ffn_prompt(h, gn, wg, wu, cw, cb, wd, tm):
    b, s, d = h.shape
    f = wg.shape[1]
    blk = pl.BlockSpec((1, tm, d), lambda bi, i: (bi, i, 0))
    wspec = lambda shape: _const_spec(shape, single_buffer=True)
    return pl.pallas_call(
        _ffn_prompt_kernel,
        out_shape=[jax.ShapeDtypeStruct((b, s, d), F32), jax.ShapeDtypeStruct((b, SUBLANES, f), F32)],
        grid=(b, s // tm),
        in_specs=[blk, _const_spec(gn.shape), wspec(wg.shape), wspec(wu.shape), _const_spec(cw.shape),
                  _const_spec(cb.shape), wspec(wd.shape)],
        out_specs=[blk, pl.BlockSpec((1, SUBLANES, f), lambda bi, i: (bi, 0, 0))],
        scratch_shapes=[pltpu.VMEM((SUBLANES, f), F32)],
        compiler_params=_cparams(("arbitrary", "arbitrary")),
        name="ffn_prompt",
    )(h, gn, wg, wu, cw, cb, wd)


def _ffn_sample_kernel(h_ref, gn_ref, wg_ref, wu_ref, cw_ref, cb_ref, wd_ref, st1_ref, st2_ref, y_ref, g_ref):
    m = h_ref.shape[0]
    t = lax.broadcasted_iota(I32, (m, 1), 0) & (T_PAD - 1)

    def shifted(g):
        gm1 = jnp.where(t == 0, st1_ref[...], pltpu.roll(g, 1, 0))
        gm2 = jnp.where(t < 2, st2_ref[...], pltpu.roll(g, 2, 0))
        return gm1, gm2

    y, g = _ffn_core(h_ref[...], gn_ref[...], wg_ref, wu_ref, cw_ref, cb_ref, wd_ref, shifted)
    y_ref[...] = y
    g_ref[...] = g


def _ffn_sample(h2d, gn, wg, wu, cw, cb, wd, st1, st2):
    m, d = h2d.shape
    f = wg.shape[1]
    wspec = lambda shape: _const_spec(shape, single_buffer=True)
    return pl.pallas_call(
        _ffn_sample_kernel,
        out_shape=[jax.ShapeDtypeStruct((m, d), F32), jax.ShapeDtypeStruct((m, f), F32)],
        grid=(1,),
        in_specs=[_const_spec((m, d)), _const_spec(gn.shape), wspec(wg.shape), wspec(wu.shape),
                  _const_spec(cw.shape), _const_spec(cb.shape), wspec(wd.shape), _const_spec((m, f)),
                  _const_spec((m, f))],
        out_specs=[_const_spec((m, d)), _const_spec((m, f))],
        compiler_params=_cparams(("arbitrary",)),
        name="ffn_sample",
    )(h2d, gn, wg, wu, cw, cb, wd, st1, st2)


def _rel_bucket(dist):
    n = np.maximum(dist, 0)
    max_exact = NUM_BUCKETS // 2
    nf = np.maximum(n, 1).astype(np.float32)
    log_b = (np.log(nf / np.float32(max_exact)) / np.float32(math.log(MAX_DISTANCE / max_exact))
             * np.float32(NUM_BUCKETS - max_exact))
    large = np.minimum(max_exact + log_b.astype(np.int32), NUM_BUCKETS - 1)
    return np.where(n < max_exact, n, large)


def _bias_by_dist(dist, causal, bias):
    onehot = (_rel_bucket(dist)[..., None] == np.arange(NUM_BUCKETS)).astype(np.float32)
    vals = jnp.einsum("...k,kh->h...", onehot, bias, precision=lax.Precision.HIGHEST)
    return jnp.where(causal[None], vals, NEG).astype(F32)


def _prompt_bias_tables(bias):
    r = np.arange(TK)[:, None]
    c = np.arange(TK)[None, :]
    always = np.ones((TK, TK), bool)
    masked = _bias_by_dist(r - c, ~always, bias)
    t0 = _bias_by_dist(r - c, r >= c, bias)
    t1 = _bias_by_dist(r - c + TK, always, bias)
    t2 = _bias_by_dist(r - c + 2 * TK, always, bias)
    return jnp.swapaxes(jnp.stack([masked, t0, t1, t2], axis=1), -1, -2) * LOG2E


def _sample_bias_tables(bias, streams_per_head):
    t = np.arange(T_PAD)[:, None]
    c = np.arange(TK)[None, :]
    always = np.ones((T_PAD, TK), bool)
    far = _bias_by_dist(t - c + 2 * TK, always, bias)
    last = _bias_by_dist(t - c + TK, always, bias)
    new = _bias_by_dist(t - c, c <= t, bias)
    tabs = jnp.stack([far, last, new], axis=0)
    tabs = jnp.repeat(tabs[:, :, None], streams_per_head, axis=2)
    return tabs.reshape(3, -1, TK)


def _group_mean_matrix(width, group):
    idx = jnp.arange(width) // group
    return jnp.where(idx[:, None] == idx[None, :], 1.0 / group, 0.0).astype(BF16)


def _pad_rows(x, rows):
    return jnp.pad(x, ((0, 0), (0, rows - x.shape[1]), (0, 0)))


def _new_keys_t(x, rows):
    return jnp.swapaxes(_pad_rows(x, rows), 1, 2)


def kernel(x_prompt, x_sample, mem_prompt, cache_a_k, cache_a_v, cache_idx_k, cache_b_k, cache_b_v, cache_mem_k, cache_mem_v, state_ffn_conv, page_table, rel_bias, norm_mix, w_in, a_q_norm, a_k_norm, b_q_norm, b_k_norm, diff_lambda, diff_subln, w_out, norm_mem_x, norm_mem_src, w_mem_q, w_mem_kv, mem_q_norm, mem_k_norm, w_mem_o, norm_ffn, w_up, w_gate, ffn_conv_w, ffn_conv_b, w_down):
    depth = w_in.shape[0]
    assert depth == 1, "single-layer trunk"
    layer = 0
    lam_init = 0.8 - 0.6 * math.exp(-0.3 * layer)
    b, s, d = x_prompt.shape
    db, t_new, _ = x_sample.shape
    assert CONV_W - 1 <= t_new <= T_PAD and cache_a_k.shape[2] == TK
    m_tok = mem_prompt.shape[1]
    f = w_up.shape[-1]
    n_pool = cache_a_k.shape[1]
    tkb = min(TKB_MAX, s)
    tkb_diff = min(TKB_DIFF, s)
    assert s % tkb == 0 and tkb % TK == 0 and s % tkb_diff == 0 and tkb_diff % TK == 0

    w_in_l = w_in[layer]
    n_front = A_Q + 2 * A_KV + IDX_Q + IDX_DIM + IDX_HEADS
    w_in_p = jnp.concatenate(
        [w_in_l[:, :n_front], jnp.zeros((d, C_QB - n_front), F32), w_in_l[:, n_front:]], axis=1).astype(BF16)
    ones = lambda n: jnp.ones((n,), F32)
    hgain = jnp.concatenate([
        jnp.tile(a_q_norm[layer], A_HEADS), jnp.tile(a_k_norm[layer], A_KV_HEADS), ones(C_QB - C_VA),
        jnp.tile(b_q_norm[layer], 2 * B_HEADS), jnp.tile(b_k_norm[layer], 2 * B_HEADS), ones(B_V)])[None, :]
    gmat64 = _group_mean_matrix(A_Q, A_HEAD_DIM)
    gmat128 = _group_mean_matrix(MEM_W, MEM_HEAD_DIM)
    proj_segs = ((C_QA, A_Q, True, A_Q, 1), (C_KA, A_KV, True, A_KV, 1), (C_VA, A_KV, False, A_KV, 1),
                 (C_QI, IDX_Q, False, IDX_Q, 1), (C_KIW, LANES, False, LANES, 1), (C_KIW, LANES, False, IDX_DIM, 1),
                 (C_QB, B_QK, True, B_QK, 1), (C_KB, B_QK, True, B_QK, 1), (C_VB, B_V, False, B_V, 1))
    prompt_segs = ((C_QA, A_Q, True, A_Q, 1), (C_KA, A_KV, True, A_KV, 1), (C_QI, IDX_Q, False, IDX_Q, 1),
                   (C_KIW, LANES, False, LANES, 1), (C_QB, B_QK, True, B_QK, 1), (C_KB, B_QK, True, B_QK, 1),
                   (C_VA, A_KV, False, A_KV, -tkb), (C_VB, B_V, False, B_V, -tkb_diff),
                   (C_KA, A_KV, True, A_KV, FEATURE_MAJOR), (C_VA, A_KV, False, A_KV, FEATURE_MAJOR),
                   (C_KIW, LANES, False, IDX_DIM, FEATURE_MAJOR), (C_KB, B_QK, True, B_QK, FEATURE_MAJOR),
                   (C_VB, B_V, False, B_V, B_HEADS))
    g_mix = norm_mix[layer][None, :]
    w_out_b = w_out[layer].astype(BF16)
    w_q_b = w_mem_q[layer].astype(BF16)
    w_o_b = w_mem_o[layer].astype(BF16)
    w_kv_b = w_mem_kv[layer].astype(BF16)
    w_gate_b = w_gate[layer].astype(BF16)
    w_up_b = w_up[layer].astype(BF16)
    w_down_b = w_down[layer].astype(BF16)
    g_memx = norm_mem_x[layer][None, :]
    q_gain = jnp.tile(mem_q_norm[layer], MEM_HEADS)[None, :]
    kv_gain = jnp.concatenate([jnp.tile(mem_k_norm[layer], MEM_HEADS), ones(MEM_W)])[None, :]
    g_ffn = norm_ffn[layer][None, :]
    conv_w = ffn_conv_w[layer]
    conv_b = ffn_conv_b[layer][None, :]
    lam_p = diff_lambda[layer]
    subln = diff_subln[layer][None, :]
    bias_a = rel_bias[:, :A_HEADS]
    bias_b = rel_bias[:, A_HEADS:]

    assert TM_TOKENS % tkb == 0 and TM_TOKENS % tkb_diff == 0, "value chunks must tile the projection's token tile"
    qa, ka, qi, kiw, qb, kb, va_ch, vb_ch, ka_t, va_t, ki_t, kb_t, vb_c = _norm_proj(
        x_prompt.reshape(b * s, d), g_mix, w_in_p, gmat64, hgain, prompt_segs, TM_TOKENS, "proj_prompt", seq=s)

    def from_feature_major(a_t, shape):
        nd = len(shape)
        return jnp.transpose(a_t.reshape((b,) + shape + (s,)), (0, nd + 1) + tuple(range(1, nd + 1)))[None]
    r3 = lambda a: a.reshape(b, s, a.shape[-1])
    mix_a = _dsa_prompt(r3(qi), r3(kiw), r3(qa), r3(ka), va_ch, _prompt_bias_tables(bias_a), tkb)
    mix_b = _diff_prompt(r3(qb), r3(kb), vb_ch, _prompt_bias_tables(bias_b), lam_p, subln, lam_init, tkb_diff)
    mk, mv = _norm_proj(mem_prompt.reshape(b * m_tok, d), norm_mem_src[layer][None, :], w_kv_b, gmat128, kv_gain,
                        ((0, MEM_W, True, MEM_W, MEM_HEADS), (MEM_W, MEM_W, False, MEM_W, MEM_HEADS)), 256, "mem_kv")
    mem_rows = lambda a: a.reshape(-1, m_tok * MEM_HEADS, MEM_HEAD_DIM)
    h2 = _mid(x_prompt, mix_a, mix_b, mem_rows(mk), mem_rows(mv), w_out_b, g_memx, w_q_b, gmat128, q_gain, w_o_b,
              TM_TOKENS, 1)
    yp, tail = _ffn_prompt(h2, g_ffn, w_gate_b, w_up_b, conv_w, conv_b, w_down_b, TM_TOKENS)
    conv_p = tail[:, SUBLANES - (CONV_W - 1):, :]

    xs = _pad_rows(x_sample, T_PAD)
    qa, ka_s, va_s, qi, kiw, ki_s, qb, kb_s, vb_s = _norm_proj(
        xs.reshape(db * T_PAD, d), g_mix, w_in_p, gmat64, hgain, proj_segs, db * T_PAD, "proj_sample")
    r3 = lambda a: a.reshape(db, T_PAD, a.shape[-1])
    idx_kt = jnp.transpose(cache_idx_k[layer], (0, 2, 1))
    a_kt = jnp.transpose(cache_a_k[layer], (0, 2, 3, 1)).reshape(n_pool, A_KV, TK)
    a_vt = jnp.transpose(cache_a_v[layer], (0, 2, 3, 1)).reshape(n_pool, A_KV, TK)
    b_kt = jnp.transpose(cache_b_k[layer], (0, 2, 3, 4, 1)).reshape(n_pool, B_QK, TK)
    b_v2 = cache_b_v[layer].reshape(n_pool, TK * B_HEADS, B_VH)
    n_pages = page_table.shape[1]
    qi_rows = r3(qi).reshape(db, T_PAD, IDX_HEADS, IDX_DIM).transpose(0, 2, 1, 3).reshape(db, IDX_HEADS * T_PAD, IDX_DIM)
    wi_rows = r3(kiw)[:, :, IDX_DIM:IDX_DIM + IDX_HEADS].transpose(0, 2, 1).reshape(db, IDX_HEADS * T_PAD, 1)
    selb = _dsa_sample_select(page_table, qi_rows, wi_rows, _new_keys_t(r3(ki_s), TK), idx_kt, t_new,
                              math.gcd(n_pages, SELECT_PAGES_PER_STEP))
    q_rows = r3(qa).reshape(db, T_PAD, A_HEADS, A_HEAD_DIM).transpose(0, 2, 1, 3)
    eye_g = jnp.repeat(jnp.eye(A_KV_HEADS, dtype=F32), A_REP, axis=0)
    qa_bd = (q_rows[:, :, :, None, :] * eye_g[None, :, None, :, None]).reshape(db, A_HEADS * T_PAD, A_KV)
    mix_a = _dsa_sample_attn(page_table, qa_bd, _new_keys_t(r3(ka_s), TK), _new_keys_t(r3(va_s), TK), selb,
                             _sample_bias_tables(bias_a, 1), a_kt, a_vt)
    n_str = 2 * B_HEADS
    q_rows = r3(qb).reshape(db, T_PAD, n_str, B_HEAD_DIM).transpose(0, 2, 1, 3)
    qb_bd = (q_rows[:, :, :, None, :] * jnp.eye(n_str, dtype=F32)[None, :, None, :, None]).reshape(
        db, n_str * T_PAD, B_QK)
    mix_b = _diff_sample(page_table, qb_bd, _new_keys_t(r3(kb_s), TK), _pad_rows(r3(vb_s), TK),
                         _sample_bias_tables(bias_b, 2), lam_p, subln, b_kt, b_v2, lam_init,
                         math.gcd(n_pages, DIFF_PAGES_PER_STEP))
    h2 = _mid(xs, mix_a, mix_b, mem_rows(cache_mem_k[layer]), mem_rows(cache_mem_v[layer]), w_out_b, g_memx, w_q_b,
              gmat128, q_gain, w_o_b, T_PAD, math.gcd(db, MID_SEQS_PER_STEP))
    state = state_ffn_conv[layer]
    st1 = _pad_rows(state[:, 1:2, :], T_PAD).reshape(db * T_PAD, f)
    st2 = _pad_rows(state, T_PAD).reshape(db * T_PAD, f)
    ys, g_s = _ffn_sample(h2.reshape(db * T_PAD, d), g_ffn, w_gate_b, w_up_b, conv_w, conv_b, w_down_b, st1, st2)
    ys = ys.reshape(db, T_PAD, d)[:, :t_new]
    conv_s = g_s.reshape(db, T_PAD, f)[:, t_new - (CONV_W - 1):t_new]

    def new_rows(a, shape):
        return a.reshape(db, T_PAD, -1)[:, :t_new].reshape((1, db, t_new) + shape)

    return (yp, ys,
            from_feature_major(ka_t, (A_KV_HEADS, A_HEAD_DIM)), from_feature_major(va_t, (A_KV_HEADS, A_HEAD_DIM)),
            from_feature_major(ki_t, (IDX_DIM,)), from_feature_major(kb_t, (B_HEADS, 2, B_HEAD_DIM)),
            vb_c.reshape(1, b, s, B_HEADS, 2 * B_HEAD_DIM),
            mk.reshape(1, b, m_tok, MEM_HEADS, MEM_HEAD_DIM), mv.reshape(1, b, m_tok, MEM_HEADS, MEM_HEAD_DIM),
            conv_p[None],
            new_rows(ka_s, (A_KV_HEADS, A_HEAD_DIM)), new_rows(va_s, (A_KV_HEADS, A_HEAD_DIM)),
            new_rows(ki_s, (IDX_DIM,)), new_rows(kb_s, (B_HEADS, 2, B_HEAD_DIM)),
            new_rows(vb_s, (B_HEADS, 2 * B_HEAD_DIM)), conv_s[None])
```

```python
import functools
import math

import jax
import jax.numpy as jnp
import numpy as np
from jax import lax
from jax.experimental import pallas as pl
from jax.experimental.pallas import tpu as pltpu

F32 = jnp.float32
BF16 = jnp.bfloat16
I32 = jnp.int32

EPS = 1e-6
NEG = -1e30
LOG2E = math.log2(math.e)
INT_MIN = -(2 ** 31)

A_HEADS = 8
A_KV_HEADS = 2
A_HEAD_DIM = 64
IDX_HEADS = 4
IDX_DIM = 64
TOPK_MAX = 256
B_HEADS = 4
B_HEAD_DIM = 64
MEM_HEADS = 4
MEM_HEAD_DIM = 128
CONV_W = 3
NUM_BUCKETS = 32
MAX_DISTANCE = 128

A_Q = A_HEADS * A_HEAD_DIM
A_KV = A_KV_HEADS * A_HEAD_DIM
IDX_Q = IDX_HEADS * IDX_DIM
B_QK = B_HEADS * 2 * B_HEAD_DIM
B_V = B_HEADS * 2 * B_HEAD_DIM
B_VH = 2 * B_HEAD_DIM
MEM_W = MEM_HEADS * MEM_HEAD_DIM
A_REP = A_HEADS // A_KV_HEADS

LANES = 128
SUBLANES = 8
TQ_DSA = 512
TQ_DIFF = 512
TK = 128
TKB_MAX = 512
T_PAD = SUBLANES
TM_TOKENS = 512
FEATURE_MAJOR = 0
TKB_DIFF = 512
MID_SEQS_PER_STEP = 8
SELECT_PAGES_PER_STEP = 64
DIFF_PAGES_PER_STEP = 32
VMEM_LIMIT = 56 * 1024 * 1024

C_QA = 0
C_KA = C_QA + A_Q
C_VA = C_KA + A_KV
C_QI = C_VA + A_KV
C_KIW = C_QI + IDX_Q
C_QB = C_KIW + LANES
C_KB = C_QB + B_QK
C_VB = C_KB + B_QK
D_IN_PAD = C_VB + B_V


def _dot(a, b):
    return jnp.dot(a, b, preferred_element_type=F32)


def _dot_nt(a, b):
    return lax.dot_general(a, b, (((1,), (1,)), ((), ())), preferred_element_type=F32)


def _rms(x):
    return x * lax.rsqrt(jnp.mean(x * x, axis=-1, keepdims=True) + EPS)


def _group_mean_sq(x, gm_ref):
    sq = (x * x).astype(BF16)
    gm = gm_ref[:LANES, :LANES]
    return jnp.concatenate([_dot(sq[:, j:j + LANES], gm) for j in range(0, x.shape[1], LANES)], axis=1)


KEY_NEG_INF = -(2 ** 31) + 0x7FFFFF


def _key_to_float(key):
    bits = key ^ ((key >> 31) & 0x7FFFFFFF)
    return jnp.where(key <= KEY_NEG_INF, -jnp.inf, lax.bitcast_convert_type(bits, F32))


def _kth_largest(count_ge, shape, topk):
    def bit_body(it, key):
        cand = key ^ lax.shift_left(jnp.int32(1), 31 - it)
        return jnp.where(count_ge(_key_to_float(cand)) >= topk, cand, key)

    return _key_to_float(lax.fori_loop(0, 32, bit_body, jnp.full(shape, INT_MIN, I32)))


def _cparams(sem, vmem=VMEM_LIMIT):
    return pltpu.CompilerParams(dimension_semantics=sem, vmem_limit_bytes=vmem)


def _const_spec(shape, single_buffer=False):
    nd = len(shape)
    if single_buffer:
        return pl.BlockSpec(shape, lambda *_: (0,) * nd, pipeline_mode=pl.Buffered(1))
    return pl.BlockSpec(shape, lambda *_: (0,) * nd)


def _norm_proj_kernel(x_ref, g_ref, w_ref, gm_ref, hg_ref, *out_refs, segs):
    xn = _rms(x_ref[...]) * g_ref[...]
    p = _dot(xn.astype(BF16), w_ref[...])
    tm = x_ref.shape[0]
    done = {}
    for (start, width, norm, out_width, split), o_ref in zip(segs, out_refs):
        if (start, width, norm) not in done:
            s = p[:, start:start + width]
            if norm:
                s = s * lax.rsqrt(_group_mean_sq(s, gm_ref) + EPS) * hg_ref[:, start:start + width]
            done[(start, width, norm)] = s
        s = done[(start, width, norm)]
        if split == 1:
            o_ref[...] = s[:, :out_width]
        elif split == FEATURE_MAJOR:
            o_ref[0] = s.T[:out_width, :]
        elif split < 0:
            st = s.T
            for u in range(tm // -split):
                o_ref[0, u] = st[:out_width, u * -split:(u + 1) * -split]
        else:
            pw = out_width // split
            for j in range(split):
                o_ref[pl.ds(j, tm, stride=split), :] = s[:, j * pw:(j + 1) * pw]


def _norm_proj(x2d, gain, w, gmat, hgain, segs, tm, name, seq=None):
    m, d = x2d.shape
    n = w.shape[1]
    tiles = None if seq is None else seq // tm

    def shape_spec(ow, sp):
        if sp == FEATURE_MAJOR:
            return (jax.ShapeDtypeStruct((m // seq, ow, seq), F32),
                    pl.BlockSpec((1, ow, tm), lambda i: (i // tiles, 0, i % tiles)))
        if sp < 0:
            return (jax.ShapeDtypeStruct((m // seq, seq // -sp, ow, -sp), F32),
                    pl.BlockSpec((1, tm // -sp, ow, -sp), lambda i: (i // tiles, i % tiles, 0, 0)))
        return (jax.ShapeDtypeStruct((m * sp, ow // sp), F32), pl.BlockSpec((tm * sp, ow // sp), lambda i: (i, 0)))

    shapes, specs = zip(*[shape_spec(ow, sp) for (_, _, _, ow, sp) in segs])
    return pl.pallas_call(
        functools.partial(_norm_proj_kernel, segs=segs),
        out_shape=list(shapes),
        grid=(m // tm,),
        in_specs=[pl.BlockSpec((tm, d), lambda i: (i, 0)), _const_spec((1, d)), _const_spec((d, n)),
                  _const_spec(gmat.shape), _const_spec((1, n))],
        out_specs=list(specs),
        compiler_params=_cparams(("parallel",)),
        name=name,
    )(x2d, gain, w, gmat, hgain)


def _init_flash(m_sc, l_sc, acc_sc):
    m_sc[...] = jnp.full(m_sc.shape, -jnp.inf, F32)
    l_sc[...] = jnp.zeros(l_sc.shape, F32)
    acc_sc[...] = jnp.zeros(acc_sc.shape, F32)


def _flash_update(st, s, pv_fn, m_sc, l_sc, acc_sc):
    m_old = m_sc[st]
    m_new = jnp.maximum(m_old, jnp.max(s, axis=-1, keepdims=True))
    alpha = jnp.exp(m_old - m_new)
    p = jnp.exp(s - m_new)
    l_sc[st] = alpha * l_sc[st] + jnp.sum(p, axis=-1, keepdims=True)
    acc_sc[st] = alpha * acc_sc[st] + pv_fn(p.astype(BF16))
    m_sc[st] = m_new


def _flash_scratch(streams, rows, dv):
    return [pltpu.VMEM((streams, rows, 1), F32), pltpu.VMEM((streams, rows, 1), F32),
            pltpu.VMEM((streams, rows, dv), F32)]


def _select_mask(sk, thr, need, before):
    need = jnp.where(thr > -jnp.inf, need, 0.0)
    return jnp.where(sk > thr, 0.0, jnp.where(sk == thr, jnp.where(before < need, 0.0, NEG), NEG))


def _chunk_bias(dtab_ref, h, i, c, ksub, tq):
    qsub = tq // TK
    return jnp.concatenate(
        [jnp.concatenate([dtab_ref[h, jnp.clip(i * qsub + a - (c * ksub + j) + 1, 0, 3)] for a in range(qsub)], axis=1)
         for j in range(ksub)], axis=0)


def _flash_update_t(st, s, vt, m_sc, l_sc, acc_sc):
    m_old = m_sc[st]
    m_new = jnp.maximum(m_old, jnp.max(_col_reduce(s, jnp.max), axis=0, keepdims=True))
    alpha = jnp.exp2(m_old - m_new)
    p = jnp.exp2(s - m_new)
    l_sc[st] = alpha * l_sc[st] + jnp.sum(_col_reduce(p, jnp.sum), axis=0, keepdims=True)
    acc_sc[st] = alpha * acc_sc[st] + _dot(vt, p.astype(BF16))
    m_sc[st] = m_new


def _flash_scratch_t(streams, dv, cols):
    return [pltpu.VMEM((streams, 1, cols), F32), pltpu.VMEM((streams, 1, cols), F32),
            pltpu.VMEM((streams, dv, cols), F32)]


REDUCE_WAYS = 8


def _col_reduce(x, op):
    rows, cols = x.shape
    slabs = rows // SUBLANES
    if slabs % REDUCE_WAYS == 0 and slabs > REDUCE_WAYS:
        x = op(x.reshape(REDUCE_WAYS, slabs // REDUCE_WAYS, SUBLANES, cols), axis=1)
    else:
        x = x.reshape(slabs, SUBLANES, cols)
    return op(x, axis=0)


def _col_count(w):
    return _col_reduce(w, jnp.sum)


def _dsa_prompt_kernel(qi_ref, wit_ref, kiw_ref, qa_ref, ka_ref, vat_ref, dtab_ref, tril_ref, o_ref,
                       skey_ref, selb_ref, m_sc, l_sc, acc_sc, *, topk, tkb):
    tq = TQ_DSA
    i = pl.program_id(1)
    ksub = tkb // TK
    nbig = (i * tq + tq - 1) // tkb + 1
    qi = qi_ref[0].astype(BF16)
    wit = wit_ref[0] * (IDX_HEADS ** -0.5 * IDX_DIM ** -0.5)
    krow = lax.broadcasted_iota(I32, (tkb, tq), 0)
    qcol = lax.broadcasted_iota(I32, (tkb, tq), 1)

    def score_body(c, carry):
        off = pl.multiple_of(c * tkb, tkb)
        kc = kiw_ref[0, pl.ds(off, tkb), 0:IDX_DIM].astype(BF16)
        sc = jnp.zeros((tkb, tq), F32)
        for h in range(IDX_HEADS):
            d = _dot_nt(kc, qi[:, h * IDX_DIM:(h + 1) * IDX_DIM])
            sc = sc + jnp.maximum(d, 0.0) * wit[h:h + 1, :]
        causal = (c * tkb + krow) <= (i * tq + qcol)
        skey_ref[c] = jnp.where(causal, sc, -jnp.inf)
        return carry

    lax.fori_loop(0, nbig, score_body, 0)

    diagonal_last = tq == tkb

    def count(cmp, t):
        def body(c, cnt):
            return cnt + _col_count(jnp.where(cmp(skey_ref[c], t), 1.0, 0.0))
        full = nbig - 1 if diagonal_last else nbig
        cnt = lax.fori_loop(0, full, body, jnp.zeros((SUBLANES, tq), F32))
        if diagonal_last:
            parts = []
            for a in range(tq // TK):
                blk = skey_ref[nbig - 1, 0:(a + 1) * TK, a * TK:(a + 1) * TK]
                parts.append(_col_count(jnp.where(cmp(blk, t[:, a * TK:(a + 1) * TK]), 1.0, 0.0)))
            cnt = cnt + jnp.concatenate(parts, axis=1)
        return jnp.sum(cnt, axis=0, keepdims=True)

    thr = _kth_largest(lambda t: count(lambda sk, tt: sk >= tt, t), (1, tq), topk)
    need = float(topk) - count(lambda sk, tt: sk > tt, thr)
    tril = tril_ref[...]

    def sel_body(c, off):
        sk = skey_ref[c]
        eqf = jnp.where(sk == thr, 1.0, 0.0)
        before = _dot(tril, eqf.astype(BF16)) + off
        selb_ref[c] = _select_mask(sk, thr, need, before)
        return off + jnp.sum(_col_count(eqf), axis=0, keepdims=True)

    lax.fori_loop(0, nbig, sel_body, jnp.zeros((1, tq), F32))

    _init_flash(m_sc, l_sc, acc_sc)
    scale = A_HEAD_DIM ** -0.5 * LOG2E
    qg = [jnp.concatenate([qa_ref[0, :, h * A_HEAD_DIM:(h + 1) * A_HEAD_DIM] * scale
                           for h in range(g * A_REP, (g + 1) * A_REP)], axis=0).astype(BF16)
          for g in range(A_KV_HEADS)]

    def att_body(c, carry):
        off = pl.multiple_of(c * tkb, tkb)
        maskb = selb_ref[c]
        for g in range(A_KV_HEADS):
            g0 = g * A_HEAD_DIM
            kc = ka_ref[0, pl.ds(off, tkb), g0:g0 + A_HEAD_DIM].astype(BF16)
            vt = vat_ref[0, c, g0:g0 + A_HEAD_DIM, :].astype(BF16)
            sg = _dot_nt(kc, qg[g])
            for r in range(A_REP):
                h = g * A_REP + r
                s = sg[:, r * tq:(r + 1) * tq] + (_chunk_bias(dtab_ref, h, i, c, ksub, tq) + maskb)
                _flash_update_t(h, s, vt, m_sc, l_sc, acc_sc)
        return carry

    lax.fori_loop(0, nbig, att_body, 0)
    for h in range(0, A_HEADS, 2):
        ot = jnp.concatenate([acc_sc[h] / l_sc[h], acc_sc[h + 1] / l_sc[h + 1]], axis=0)
        o_ref[0, :, h * A_HEAD_DIM:(h + 2) * A_HEAD_DIM] = ot.T


def _dsa_prompt(qi, kiw, qa, ka, vat, dtab, tkb):
    b, s, _ = qa.shape
    tq = TQ_DSA
    nq = s // tq
    nc = s // tkb
    topk = min(TOPK_MAX, s // 4)
    tril = (jnp.arange(tkb)[None, :] < jnp.arange(tkb)[:, None]).astype(BF16)
    wit = jnp.swapaxes(kiw[:, :, IDX_DIM:IDX_DIM + SUBLANES], 1, 2)
    blk = lambda w: pl.BlockSpec((1, tq, w), lambda bi, i: (bi, i, 0))
    full = lambda w: pl.BlockSpec((1, s, w), lambda bi, i: (bi, 0, 0))
    return pl.pallas_call(
        functools.partial(_dsa_prompt_kernel, topk=topk, tkb=tkb),
        out_shape=jax.ShapeDtypeStruct((b, s, A_Q), F32),
        grid=(b, nq),
        in_specs=[blk(IDX_Q), pl.BlockSpec((1, SUBLANES, tq), lambda bi, i: (bi, 0, i)), full(LANES), blk(A_Q),
                  full(A_KV), pl.BlockSpec((1, nc, A_KV, tkb), lambda bi, i: (bi, 0, 0, 0)),
                  _const_spec(dtab.shape), _const_spec(tril.shape)],
        out_specs=blk(A_Q),
        scratch_shapes=[pltpu.VMEM((nc, tkb, tq), F32), pltpu.VMEM((nc, tkb, tq), F32)]
        + _flash_scratch_t(A_HEADS, A_HEAD_DIM, tq),
        compiler_params=_cparams(("parallel", "arbitrary")),
        name="dsa_prompt",
    )(qi, wit, kiw, qa, ka, vat, dtab, tril)


def _diff_lambda(lp_ref, lam_init):
    lp = lp_ref[...]
    s1 = jnp.sum(lp[0:1] * lp[1:2], axis=-1, keepdims=True)
    s2 = jnp.sum(lp[2:3] * lp[3:4], axis=-1, keepdims=True)
    return jnp.exp(s1) - jnp.exp(s2) + lam_init


def _diff_finish(o0, o1, lam, sg, lam_init):
    o = o0 - lam * o1
    return _rms(o) * sg * (1.0 - lam_init)


def _diff_prompt_kernel(q_ref, k_ref, vt_ref, dtab_ref, lp_ref, sg_ref, o_ref, m_sc, l_sc, acc_sc, *, lam_init, tkb):
    tq = TQ_DIFF
    i = pl.program_id(1)
    ksub = tkb // TK
    nbig = (i * tq + tq - 1) // tkb + 1
    scale = B_HEAD_DIM ** -0.5 * LOG2E
    _init_flash(m_sc, l_sc, acc_sc)

    def body(c, carry):
        off = pl.multiple_of(c * tkb, tkb)
        for h in range(B_HEADS):
            bias = _chunk_bias(dtab_ref, h, i, c, ksub, tq)
            vt = vt_ref[0, c, h * B_VH:(h + 1) * B_VH, :].astype(BF16)
            for comp in range(2):
                st = h * 2 + comp
                c0 = st * B_HEAD_DIM
                qh = (q_ref[0, :, c0:c0 + B_HEAD_DIM] * scale).astype(BF16)
                kc = k_ref[0, pl.ds(off, tkb), c0:c0 + B_HEAD_DIM].astype(BF16)
                _flash_update_t(st, _dot_nt(kc, qh) + bias, vt, m_sc, l_sc, acc_sc)
        return carry

    lax.fori_loop(0, nbig, body, 0)
    lam = _diff_lambda(lp_ref, lam_init)
    for h in range(B_HEADS):
        o0 = (acc_sc[2 * h] / l_sc[2 * h]).T
        o1 = (acc_sc[2 * h + 1] / l_sc[2 * h + 1]).T
        o_ref[0, :, h * B_VH:(h + 1) * B_VH] = _diff_finish(o0, o1, lam, sg_ref[...], lam_init)


def _diff_prompt(qb, kb, vbt, dtab, lam_p, subln, lam_init, tkb):
    b, s, _ = qb.shape
    tq = TQ_DIFF
    nq = s // tq
    nc = s // tkb
    blk = lambda w: pl.BlockSpec((1, tq, w), lambda bi, i: (bi, i, 0))
    full = lambda w: pl.BlockSpec((1, s, w), lambda bi, i: (bi, 0, 0))
    return pl.pallas_call(
        functools.partial(_diff_prompt_kernel, lam_init=lam_init, tkb=tkb),
        out_shape=jax.ShapeDtypeStruct((b, s, B_V), F32),
        grid=(b, nq),
        in_specs=[blk(B_QK), full(B_QK), pl.BlockSpec((1, nc, B_V, tkb), lambda bi, i: (bi, 0, 0, 0)),
                  _const_spec(dtab.shape), _const_spec(lam_p.shape), _const_spec(subln.shape)],
        out_specs=blk(B_V),
        scratch_shapes=_flash_scratch_t(2 * B_HEADS, B_VH, tq),
        compiler_params=_cparams(("parallel", "arbitrary")),
        name="diff_prompt",
    )(qb, kb, vbt, dtab, lam_p, subln)


def _page_specs(pps, rows, cols):
    return [pl.BlockSpec((1, rows, cols), functools.partial(lambda b, s, pt, j: (pt[b, s * pps + j], 0, 0), j=j))
            for j in range(pps)]


def _all_page_specs(npages, rows, cols):
    return [pl.BlockSpec((1, rows, cols), functools.partial(lambda b, pt, j: (pt[b, j], 0, 0), j=j))
            for j in range(npages)]


def _dsa_sample_select_kernel(pt_ref, q_ref, wi_ref, kinew_ref, *rest, pps, npages, topk, group):
    del pt_ref
    page_refs = rest[:pps]
    tri_ref, selb_ref, sc_ref = rest[pps:]
    b = pl.program_id(0)
    s = pl.program_id(1)
    np1 = npages + 1
    q = q_ref[0].astype(BF16)
    wi = wi_ref[0] * (IDX_HEADS ** -0.5 * IDX_DIM ** -0.5)

    def scores(kt):
        return jnp.sum((jnp.maximum(_dot(q, kt), 0.0) * wi).reshape(IDX_HEADS, T_PAD, TK), axis=0)

    base = b * np1
    for j in range(pps):
        sc_ref[base + s * pps + j] = scores(page_refs[j][0].astype(BF16))

    last_page_step = s == pl.num_programs(1) - 1

    @pl.when(last_page_step)
    def _():
        row = lax.broadcasted_iota(I32, (T_PAD, TK), 0)
        col = lax.broadcasted_iota(I32, (T_PAD, TK), 1)
        sc_ref[base + npages] = jnp.where(col <= row, scores(kinew_ref[0].astype(BF16)), -jnp.inf)

    @pl.when(last_page_step & (b == pl.num_programs(0) - 1))
    def _():
        def group_body(gi, carry):
            sk = sc_ref[pl.ds(gi * (group * np1), group * np1)].reshape(group, np1, T_PAD, TK)

            def count(pred):
                cnt = jnp.sum(jnp.where(pred, 1.0, 0.0), axis=1)
                return jnp.sum(cnt, axis=-1, keepdims=True)

            thr = _kth_largest(lambda t: count(sk >= t[:, None]), (group, T_PAD, 1), topk)
            need = float(topk) - count(sk > thr[:, None])
            eqf = jnp.where(sk == thr[:, None], 1.0, 0.0)
            before = _dot(eqf.reshape(group * np1 * T_PAD, TK).astype(BF16), tri_ref[...]).reshape(sk.shape)
            ties = jnp.sum(eqf, axis=-1, keepdims=True)
            off = jnp.zeros((group, T_PAD, 1), F32)
            for c in range(np1):
                selb_ref[pl.ds(gi * group, group), c] = _select_mask(sk[:, c], thr, need, before[:, c] + off)
                off = off + ties[:, c]
            return carry

        lax.fori_loop(0, pl.num_programs(0) // group, group_body, 0)


def _dsa_sample_select(page_table, q_rows, wi_rows, ki_new_t, cache_ikt, n_new, pps):
    db, npages = page_table.shape
    np1 = npages + 1
    topk = min(TOPK_MAX, (npages * TK + n_new) // 4)
    group = math.gcd(db, SUBLANES)
    tri = (jnp.arange(TK)[:, None] < jnp.arange(TK)[None, :]).astype(BF16)
    per_b = lambda shape: pl.BlockSpec((1,) + shape, lambda b, s, pt: (b,) + (0,) * len(shape))
    grid_spec = pltpu.PrefetchScalarGridSpec(
        num_scalar_prefetch=1,
        grid=(db, npages // pps),
        in_specs=[per_b(q_rows.shape[1:]), per_b(wi_rows.shape[1:]), per_b((IDX_DIM, TK))]
        + _page_specs(pps, IDX_DIM, TK) + [pl.BlockSpec(tri.shape, lambda b, s, pt: (0, 0))],
        out_specs=pl.BlockSpec((db, np1, T_PAD, TK), lambda b, s, pt: (0, 0, 0, 0)),
        scratch_shapes=[pltpu.VMEM((db * np1, T_PAD, TK), F32)],
    )
    return pl.pallas_call(
        functools.partial(_dsa_sample_select_kernel, pps=pps, npages=npages, topk=topk, group=group),
        out_shape=jax.ShapeDtypeStruct((db, np1, T_PAD, TK), F32),
        grid_spec=grid_spec,
        compiler_params=_cparams(("arbitrary", "arbitrary")),
        name="dsa_sample_select",
    )(page_table, q_rows, wi_rows, ki_new_t, *([cache_ikt] * pps), tri)


def _page_bias(btab_ref, c0, n, npages):
    return jnp.concatenate([btab_ref[jnp.where(c0 + j == npages - 1, 1, 0)] for j in range(n)], axis=1)


def _dsa_sample_attn_kernel(pt_ref, q_ref, knew_ref, vnew_ref, selb_ref, btab_ref, *rest, npages):
    del pt_ref
    o_ref = rest[2 * npages]
    kts = [r[0] for r in rest[:npages]] + [knew_ref[0]]
    vts = [r[0] for r in rest[npages:2 * npages]] + [vnew_ref[0]]
    n = npages + 1
    rows = A_HEADS * T_PAD
    scale = A_HEAD_DIM ** -0.5
    q = (q_ref[0] * scale).astype(BF16)
    sc = jnp.concatenate([_dot(q, kt.astype(BF16)) for kt in kts], axis=1)
    bias = jnp.concatenate([btab_ref[0]] * (npages - 1) + [btab_ref[1], btab_ref[2]], axis=1)
    mask = jnp.concatenate([selb_ref[0, c] for c in range(n)], axis=1)
    sc = ((sc + bias).reshape(A_HEADS, T_PAD, n * TK) + mask[None]).reshape(rows, n * TK)
    p = jnp.exp(sc - jnp.max(sc, axis=-1, keepdims=True))
    l = jnp.sum(p, axis=-1, keepdims=True)
    p = p.astype(BF16)
    acc = _dot_nt(p[:, 0:TK], vts[0].astype(BF16))
    for j in range(1, n):
        acc = acc + _dot_nt(p[:, j * TK:(j + 1) * TK], vts[j].astype(BF16))
    o = acc / l
    for h in range(A_HEADS):
        g0 = (h // A_REP) * A_HEAD_DIM
        o_ref[0, :, h * A_HEAD_DIM:(h + 1) * A_HEAD_DIM] = o[h * T_PAD:(h + 1) * T_PAD, g0:g0 + A_HEAD_DIM]


def _dsa_sample_attn(page_table, q_bd, k_new_t, v_new_t, selb, btab, cache_kt, cache_vt):
    db, npages = page_table.shape
    rows = A_HEADS * T_PAD
    per_b = lambda shape: pl.BlockSpec((1,) + shape, lambda b, pt: (b,) + (0,) * len(shape))
    grid_spec = pltpu.PrefetchScalarGridSpec(
        num_scalar_prefetch=1,
        grid=(db,),
        in_specs=[per_b((rows, A_KV)), per_b((A_KV, TK)), per_b((A_KV, TK)), per_b((npages + 1, T_PAD, TK)),
                  pl.BlockSpec(btab.shape, lambda b, pt: (0, 0, 0))]
        + _all_page_specs(npages, A_KV, TK) + _all_page_specs(npages, A_KV, TK),
        out_specs=per_b((T_PAD, A_Q)),
    )
    return pl.pallas_call(
        functools.partial(_dsa_sample_attn_kernel, npages=npages),
        out_shape=jax.ShapeDtypeStruct((db, T_PAD, A_Q), F32),
        grid_spec=grid_spec,
        compiler_params=_cparams(("parallel",)),
        name="dsa_sample_attn",
    )(page_table, q_bd, k_new_t, v_new_t, selb, btab, *([cache_kt] * npages), *([cache_vt] * npages))


def _diff_sample_kernel(pt_ref, q_ref, knew_ref, vnew_ref, btab_ref, lp_ref, sg_ref, *rest, pps, npages, lam_init):
    del pt_ref
    k_refs = rest[:pps]
    v_refs = rest[pps:2 * pps]
    o_ref, m_sc, l_sc, acc_sc = rest[2 * pps:]
    s = pl.program_id(1)

    @pl.when(s == 0)
    def _():
        _init_flash(m_sc, l_sc, acc_sc)

    scale = B_HEAD_DIM ** -0.5
    q = (q_ref[0] * scale).astype(BF16)
    hrows = 2 * T_PAD

    def step(kt_list, v_fn, bias):
        n = len(kt_list)
        sc = jnp.concatenate([_dot(q, kt.astype(BF16)) for kt in kt_list], axis=1) + bias

        def pv(p):
            outs = []
            for h in range(B_HEADS):
                ph = p[h * hrows:(h + 1) * hrows]
                out = _dot(ph[:, 0:TK], v_fn(0, h))
                for j in range(1, n):
                    out = out + _dot(ph[:, j * TK:(j + 1) * TK], v_fn(j, h))
                outs.append(out)
            return jnp.concatenate(outs, axis=0)

        _flash_update(0, sc, pv, m_sc, l_sc, acc_sc)

    def page_v(j, h):
        return v_refs[j][0, pl.ds(h, TK, stride=B_HEADS), :].astype(BF16)

    step([r[0] for r in k_refs], page_v, _page_bias(btab_ref, s * pps, pps, npages))

    @pl.when(s == pl.num_programs(1) - 1)
    def _():
        step([knew_ref[0]], lambda j, h: vnew_ref[0, :, h * B_VH:(h + 1) * B_VH].astype(BF16), btab_ref[2])
        o = acc_sc[0] / l_sc[0]
        lam = _diff_lambda(lp_ref, lam_init)
        for h in range(B_HEADS):
            r0 = h * hrows
            o_ref[0, :, h * B_VH:(h + 1) * B_VH] = _diff_finish(
                o[r0:r0 + T_PAD], o[r0 + T_PAD:r0 + hrows], lam, sg_ref[...], lam_init)


def _diff_sample(page_table, q_bd, k_new_t, v_new, btab, lam_p, subln, cache_kt, cache_v2, lam_init, pps):
    db, npages = page_table.shape
    rows = 2 * B_HEADS * T_PAD
    per_b = lambda shape: pl.BlockSpec((1,) + shape, lambda b, s, pt: (b,) + (0,) * len(shape))
    const = lambda shape: pl.BlockSpec(shape, lambda b, s, pt: (0,) * len(shape))
    grid_spec = pltpu.PrefetchScalarGridSpec(
        num_scalar_prefetch=1,
        grid=(db, npages // pps),
        in_specs=[per_b((rows, B_QK)), per_b((B_QK, TK)), per_b((TK, B_V)), const(btab.shape),
                  const(lam_p.shape), const(subln.shape)]
        + _page_specs(pps, B_QK, TK) + _page_specs(pps, TK * B_HEADS, B_VH),
        out_specs=per_b((T_PAD, B_V)),
        scratch_shapes=_flash_scratch(1, rows, B_VH),
    )
    return pl.pallas_call(
        functools.partial(_diff_sample_kernel, pps=pps, npages=npages, lam_init=lam_init),
        out_shape=jax.ShapeDtypeStruct((db, T_PAD, B_V), F32),
        grid_spec=grid_spec,
        compiler_params=_cparams(("parallel", "arbitrary")),
        name="diff_sample",
    )(page_table, q_bd, k_new_t, v_new, btab, lam_p, subln, *([cache_kt] * pps), *([cache_v2] * pps))


def _mid_kernel(x_ref, ma_ref, mb_ref, mk_ref, mv_ref, wout_ref, gx_ref, wq_ref, gm_ref, qg_ref, wo_ref, h2_ref):
    group, tm, d = x_ref.shape
    rows = lambda ref: ref[...].reshape(group * tm, ref.shape[-1])
    h = (rows(x_ref) + _dot(rows(ma_ref).astype(BF16), wout_ref[:A_Q, :])
         + _dot(rows(mb_ref).astype(BF16), wout_ref[A_Q:, :]))
    hn = _rms(h) * gx_ref[...]
    q = _dot(hn.astype(BF16), wq_ref[...])
    q = (q * lax.rsqrt(_group_mean_sq(q, gm_ref) + EPS) * qg_ref[...]).astype(BF16)
    m_tok = mk_ref.shape[1] // MEM_HEADS
    o_rows = []
    for g in range(group):
        outs = []
        for hh in range(MEM_HEADS):
            sl = slice(hh * MEM_HEAD_DIM, (hh + 1) * MEM_HEAD_DIM)
            mk = mk_ref[g, pl.ds(hh, m_tok, stride=MEM_HEADS), :].astype(BF16)
            mv = mv_ref[g, pl.ds(hh, m_tok, stride=MEM_HEADS), :].astype(BF16)
            s = _dot_nt(q[g * tm:(g + 1) * tm, sl], mk) * (MEM_HEAD_DIM ** -0.5)
            p = jnp.exp(s - jnp.max(s, axis=-1, keepdims=True))
            l = jnp.sum(p, axis=-1, keepdims=True)
            outs.append(_dot(p.astype(BF16), mv) / l)
        o_rows.append(jnp.concatenate(outs, axis=-1))
    o = jnp.concatenate(o_rows, axis=0)
    h2_ref[...] = (h + _dot(o.astype(BF16), wo_ref[...])).reshape(group, tm, d)


def _mid(x, mix_a, mix_b, mk, mv, w_out, g_x, w_q, gmat, q_gain, w_o, tm, group):
    b, s, d = x.shape
    blk = lambda w: pl.BlockSpec((group, tm, w), lambda bi, i: (bi, i, 0))
    per_b = lambda w: pl.BlockSpec((group, mk.shape[1], w), lambda bi, i: (bi, 0, 0))
    return pl.pallas_call(
        _mid_kernel,
        out_shape=jax.ShapeDtypeStruct((b, s, d), F32),
        grid=(b // group, s // tm),
        in_specs=[blk(d), blk(A_Q), blk(B_V), per_b(MEM_HEAD_DIM), per_b(MEM_HEAD_DIM), _const_spec(w_out.shape),
                  _const_spec(g_x.shape), _const_spec(w_q.shape), _const_spec(gmat.shape),
                  _const_spec(q_gain.shape), _const_spec(w_o.shape)],
        out_specs=blk(d),
        compiler_params=_cparams(("parallel", "arbitrary")),
        name="mid",
    )(x, mix_a, mix_b, mk, mv, w_out, g_x, w_q, gmat, q_gain, w_o)


def _ffn_core(h, gn, wg_ref, wu_ref, cw_ref, cb_ref, wd_ref, shifted):
    xb = (_rms(h) * gn).astype(BF16)
    g = _dot(xb, wg_ref[...])
    u = _dot(xb, wu_ref[...])
    gm1, gm2 = shifted(g)
    gc = cb_ref[...] + cw_ref[0:1, :] * gm2 + cw_ref[1:2, :] * gm1 + cw_ref[2:3, :] * g
    a = gc / (1.0 + jnp.exp(-gc)) * u
    return h + _dot(a.astype(BF16), wd_ref[...]), g


def _ffn_prompt_kernel(h_ref, gn_ref, wg_ref, wu_ref, cw_ref, cb_ref, wd_ref, y_ref, tail_ref, carry_ref):
    @pl.when(pl.program_id(1) == 0)
    def _():
        carry_ref[...] = jnp.zeros(carry_ref.shape, F32)

    tm = h_ref.shape[1]
    row = lax.broadcasted_iota(I32, (tm, 1), 0)
    c0 = carry_ref[SUBLANES - 2:SUBLANES - 1, :]
    c1 = carry_ref[SUBLANES - 1:SUBLANES, :]

    def shifted(g):
        gm1 = jnp.where(row == 0, c1, pltpu.roll(g, 1, 0))
        gm2 = jnp.where(row == 0, c0, jnp.where(row == 1, c1, pltpu.roll(g, 2, 0)))
        return gm1, gm2

    y, g = _ffn_core(h_ref[0], gn_ref[...], wg_ref, wu_ref, cw_ref, cb_ref, wd_ref, shifted)
    y_ref[0] = y
    tail = g[tm - SUBLANES:, :]
    carry_ref[...] = tail
    tail_ref[0] = tail


def _ffn_prompt(h, gn, wg, wu, cw, cb, wd, tm):
    b, s, d = h.shape
    f = wg.shape[1]
    blk = pl.BlockSpec((1, tm, d), lambda bi, i: (bi, i, 0))
    wspec = lambda shape: _const_spec(shape, single_buffer=True)
    return pl.pallas_call(
        _ffn_prompt_kernel,
        out_shape=[jax.ShapeDtypeStruct((b, s, d), F32), jax.ShapeDtypeStruct((b, SUBLANES, f), F32)],
        grid=(b, s // tm),
        in_specs=[blk, _const_spec(gn.shape), wspec(wg.shape), wspec(wu.shape), _const_spec(cw.shape),
                  _const_spec(cb.shape), wspec(wd.shape)],
        out_specs=[blk, pl.BlockSpec((1, SUBLANES, f), lambda bi, i: (bi, 0, 0))],
        scratch_shapes=[pltpu.VMEM((SUBLANES, f), F32)],
        compiler_params=_cparams(("arbitrary", "arbitrary")),
        name="ffn_prompt",
    )(h, gn, wg, wu, cw, cb, wd)


def _ffn_sample_kernel(h_ref, gn_ref, wg_ref, wu_ref, cw_ref, cb_ref, wd_ref, st1_ref, st2_ref, y_ref, g_ref):
    m = h_ref.shape[0]
    t = lax.broadcasted_iota(I32, (m, 1), 0) & (T_PAD - 1)

    def shifted(g):
        gm1 = jnp.where(t == 0, st1_ref[...], pltpu.roll(g, 1, 0))
        gm2 = jnp.where(t < 2, st2_ref[...], pltpu.roll(g, 2, 0))
        return gm1, gm2

    y, g = _ffn_core(h_ref[...], gn_ref[...], wg_ref, wu_ref, cw_ref, cb_ref, wd_ref, shifted)
    y_ref[...] = y
    g_ref[...] = g


def _ffn_sample(h2d, gn, wg, wu, cw, cb, wd, st1, st2):
    m, d = h2d.shape
    f = wg.shape[1]
    wspec = lambda shape: _const_spec(shape, single_buffer=True)
    return pl.pallas_call(
        _ffn_sample_kernel,
        out_shape=[jax.ShapeDtypeStruct((m, d), F32), jax.ShapeDtypeStruct((m, f), F32)],
        grid=(1,),
        in_specs=[_const_spec((m, d)), _const_spec(gn.shape), wspec(wg.shape), wspec(wu.shape),
                  _const_spec(cw.shape), _const_spec(cb.shape), wspec(wd.shape), _const_spec((m, f)),
                  _const_spec((m, f))],
        out_specs=[_const_spec((m, d)), _const_spec((m, f))],
        compiler_params=_cparams(("arbitrary",)),
        name="ffn_sample",
    )(h2d, gn, wg, wu, cw, cb, wd, st1, st2)


def _rel_bucket(dist):
    n = np.maximum(dist, 0)
    max_exact = NUM_BUCKETS // 2
    nf = np.maximum(n, 1).astype(np.float32)
    log_b = (np.log(nf / np.float32(max_exact)) / np.float32(math.log(MAX_DISTANCE / max_exact))
             * np.float32(NUM_BUCKETS - max_exact))
    large = np.minimum(max_exact + log_b.astype(np.int32), NUM_BUCKETS - 1)
    return np.where(n < max_exact, n, large)


def _bias_by_dist(dist, causal, bias):
    onehot = (_rel_bucket(dist)[..., None] == np.arange(NUM_BUCKETS)).astype(np.float32)
    vals = jnp.einsum("...k,kh->h...", onehot, bias, precision=lax.Precision.HIGHEST)
    return jnp.where(causal[None], vals, NEG).astype(F32)


def _prompt_bias_tables(bias):
    r = np.arange(TK)[:, None]
    c = np.arange(TK)[None, :]
    always = np.ones((TK, TK), bool)
    masked = _bias_by_dist(r - c, ~always, bias)
    t0 = _bias_by_dist(r - c, r >= c, bias)
    t1 = _bias_by_dist(r - c + TK, always, bias)
    t2 = _bias_by_dist(r - c + 2 * TK, always, bias)
    return jnp.swapaxes(jnp.stack([masked, t0, t1, t2], axis=1), -1, -2) * LOG2E


def _sample_bias_tables(bias, streams_per_head):
    t = np.arange(T_PAD)[:, None]
    c = np.arange(TK)[None, :]
    always = np.ones((T_PAD, TK), bool)
    far = _bias_by_dist(t - c + 2 * TK, always, bias)
    last = _bias_by_dist(t - c + TK, always, bias)
    new = _bias_by_dist(t - c, c <= t, bias)
    tabs = jnp.stack([far, last, new], axis=0)
    tabs = jnp.repeat(tabs[:, :, None], streams_per_head, axis=2)
    return tabs.reshape(3, -1, TK)


def _group_mean_matrix(width, group):
    idx = jnp.arange(width) // group
    return jnp.where(idx[:, None] == idx[None, :], 1.0 / group, 0.0).astype(BF16)


def _pad_rows(x, rows):
    return jnp.pad(x, ((0, 0), (0, rows - x.shape[1]), (0, 0)))


def _new_keys_t(x, rows):
    return jnp.swapaxes(_pad_rows(x, rows), 1, 2)


def kernel(x_prompt, x_sample, mem_prompt, cache_a_k, cache_a_v, cache_idx_k, cache_b_k, cache_b_v, cache_mem_k, cache_mem_v, state_ffn_conv, page_table, rel_bias, norm_mix, w_in, a_q_norm, a_k_norm, b_q_norm, b_k_norm, diff_lambda, diff_subln, w_out, norm_mem_x, norm_mem_src, w_mem_q, w_mem_kv, mem_q_norm, mem_k_norm, w_mem_o, norm_ffn, w_up, w_gate, ffn_conv_w, ffn_conv_b, w_down):
    depth = w_in.shape[0]
    assert depth == 1, "single-layer trunk"
    layer = 0
    lam_init = 0.8 - 0.6 * math.exp(-0.3 * layer)
    b, s, d = x_prompt.shape
    db, t_new, _ = x_sample.shape
    assert CONV_W - 1 <= t_new <= T_PAD and cache_a_k.shape[2] == TK
    m_tok = mem_prompt.shape[1]
    f = w_up.shape[-1]
    n_pool = cache_a_k.shape[1]
    tkb = min(TKB_MAX, s)
    tkb_diff = min(TKB_DIFF, s)
    assert s % tkb == 0 and tkb % TK == 0 and s % tkb_diff == 0 and tkb_diff % TK == 0

    w_in_l = w_in[layer]
    n_front = A_Q + 2 * A_KV + IDX_Q + IDX_DIM + IDX_HEADS
    w_in_p = jnp.concatenate(
        [w_in_l[:, :n_front], jnp.zeros((d, C_QB - n_front), F32), w_in_l[:, n_front:]], axis=1).astype(BF16)
    ones = lambda n: jnp.ones((n,), F32)
    hgain = jnp.concatenate([
        jnp.tile(a_q_norm[layer], A_HEADS), jnp.tile(a_k_norm[layer], A_KV_HEADS), ones(C_QB - C_VA),
        jnp.tile(b_q_norm[layer], 2 * B_HEADS), jnp.tile(b_k_norm[layer], 2 * B_HEADS), ones(B_V)])[None, :]
    gmat64 = _group_mean_matrix(A_Q, A_HEAD_DIM)
    gmat128 = _group_mean_matrix(MEM_W, MEM_HEAD_DIM)
    proj_segs = ((C_QA, A_Q, True, A_Q, 1), (C_KA, A_KV, True, A_KV, 1), (C_VA, A_KV, False, A_KV, 1),
                 (C_QI, IDX_Q, False, IDX_Q, 1), (C_KIW, LANES, False, LANES, 1), (C_KIW, LANES, False, IDX_DIM, 1),
                 (C_QB, B_QK, True, B_QK, 1), (C_KB, B_QK, True, B_QK, 1), (C_VB, B_V, False, B_V, 1))
    prompt_segs = ((C_QA, A_Q, True, A_Q, 1), (C_KA, A_KV, True, A_KV, 1), (C_QI, IDX_Q, False, IDX_Q, 1),
                   (C_KIW, LANES, False, LANES, 1), (C_QB, B_QK, True, B_QK, 1), (C_KB, B_QK, True, B_QK, 1),
                   (C_VA, A_KV, False, A_KV, -tkb), (C_VB, B_V, False, B_V, -tkb_diff),
                   (C_KA, A_KV, True, A_KV, FEATURE_MAJOR), (C_VA, A_KV, False, A_KV, FEATURE_MAJOR),
                   (C_KIW, LANES, False, IDX_DIM, FEATURE_MAJOR), (C_KB, B_QK, True, B_QK, FEATURE_MAJOR),
                   (C_VB, B_V, False, B_V, B_HEADS))
    g_mix = norm_mix[layer][None, :]
    w_out_b = w_out[layer].astype(BF16)
    w_q_b = w_mem_q[layer].astype(BF16)
    w_o_b = w_mem_o[layer].astype(BF16)
    w_kv_b = w_mem_kv[layer].astype(BF16)
    w_gate_b = w_gate[layer].astype(BF16)
    w_up_b = w_up[layer].astype(BF16)
    w_down_b = w_down[layer].astype(BF16)
    g_memx = norm_mem_x[layer][None, :]
    q_gain = jnp.tile(mem_q_norm[layer], MEM_HEADS)[None, :]
    kv_gain = jnp.concatenate([jnp.tile(mem_k_norm[layer], MEM_HEADS), ones(MEM_W)])[None, :]
    g_ffn = norm_ffn[layer][None, :]
    conv_w = ffn_conv_w[layer]
    conv_b = ffn_conv_b[layer][None, :]
    lam_p = diff_lambda[layer]
    subln = diff_subln[layer][None, :]
    bias_a = rel_bias[:, :A_HEADS]
    bias_b = rel_bias[:, A_HEADS:]

    assert TM_TOKENS % tkb == 0 and TM_TOKENS % tkb_diff == 0, "value chunks must tile the projection's token tile"
    qa, ka, qi, kiw, qb, kb, va_ch, vb_ch, ka_t, va_t, ki_t, kb_t, vb_c = _norm_proj(
        x_prompt.reshape(b * s, d), g_mix, w_in_p, gmat64, hgain, prompt_segs, TM_TOKENS, "proj_prompt", seq=s)

    def from_feature_major(a_t, shape):
        nd = len(shape)
        return jnp.transpose(a_t.reshape((b,) + shape + (s,)), (0, nd + 1) + tuple(range(1, nd + 1)))[None]
    r3 = lambda a: a.reshape(b, s, a.shape[-1])
    mix_a = _dsa_prompt(r3(qi), r3(kiw), r3(qa), r3(ka), va_ch, _prompt_bias_tables(bias_a), tkb)
    mix_b = _diff_prompt(r3(qb), r3(kb), vb_ch, _prompt_bias_tables(bias_b), lam_p, subln, lam_init, tkb_diff)
    mk, mv = _norm_proj(mem_prompt.reshape(b * m_tok, d), norm_mem_src[layer][None, :], w_kv_b, gmat128, kv_gain,
                        ((0, MEM_W, True, MEM_W, MEM_HEADS), (MEM_W, MEM_W, False, MEM_W, MEM_HEADS)), 256, "mem_kv")
    mem_rows = lambda a: a.reshape(-1, m_tok * MEM_HEADS, MEM_HEAD_DIM)
    h2 = _mid(x_prompt, mix_a, mix_b, mem_rows(mk), mem_rows(mv), w_out_b, g_memx, w_q_b, gmat128, q_gain, w_o_b,
              TM_TOKENS, 1)
    yp, tail = _ffn_prompt(h2, g_ffn, w_gate_b, w_up_b, conv_w, conv_b, w_down_b, TM_TOKENS)
    conv_p = tail[:, SUBLANES - (CONV_W - 1):, :]

    xs = _pad_rows(x_sample, T_PAD)
    qa, ka_s, va_s, qi, kiw, ki_s, qb, kb_s, vb_s = _norm_proj(
        xs.reshape(db * T_PAD, d), g_mix, w_in_p, gmat64, hgain, proj_segs, db * T_PAD, "proj_sample")
    r3 = lambda a: a.reshape(db, T_PAD, a.shape[-1])
    idx_kt = jnp.transpose(cache_idx_k[layer], (0, 2, 1))
    a_kt = jnp.transpose(cache_a_k[layer], (0, 2, 3, 1)).reshape(n_pool, A_KV, TK)
    a_vt = jnp.transpose(cache_a_v[layer], (0, 2, 3, 1)).reshape(n_pool, A_KV, TK)
    b_kt = jnp.transpose(cache_b_k[layer], (0, 2, 3, 4, 1)).reshape(n_pool, B_QK, TK)
    b_v2 = cache_b_v[layer].reshape(n_pool, TK * B_HEADS, B_VH)
    n_pages = page_table.shape[1]
    qi_rows = r3(qi).reshape(db, T_PAD, IDX_HEADS, IDX_DIM).transpose(0, 2, 1, 3).reshape(db, IDX_HEADS * T_PAD, IDX_DIM)
    wi_rows = r3(kiw)[:, :, IDX_DIM:IDX_DIM + IDX_HEADS].transpose(0, 2, 1).reshape(db, IDX_HEADS * T_PAD, 1)
    selb = _dsa_sample_select(page_table, qi_rows, wi_rows, _new_keys_t(r3(ki_s), TK), idx_kt, t_new,
                              math.gcd(n_pages, SELECT_PAGES_PER_STEP))
    q_rows = r3(qa).reshape(db, T_PAD, A_HEADS, A_HEAD_DIM).transpose(0, 2, 1, 3)
    eye_g = jnp.repeat(jnp.eye(A_KV_HEADS, dtype=F32), A_REP, axis=0)
    qa_bd = (q_rows[:, :, :, None, :] * eye_g[None, :, None, :, None]).reshape(db, A_HEADS * T_PAD, A_KV)
    mix_a = _dsa_sample_attn(page_table, qa_bd, _new_keys_t(r3(ka_s), TK), _new_keys_t(r3(va_s), TK), selb,
                             _sample_bias_tables(bias_a, 1), a_kt, a_vt)
    n_str = 2 * B_HEADS
    q_rows = r3(qb).reshape(db, T_PAD, n_str, B_HEAD_DIM).transpose(0, 2, 1, 3)
    qb_bd = (q_rows[:, :, :, None, :] * jnp.eye(n_str, dtype=F32)[None, :, None, :, None]).reshape(
        db, n_str * T_PAD, B_QK)
    mix_b = _diff_sample(page_table, qb_bd, _new_keys_t(r3(kb_s), TK), _pad_rows(r3(vb_s), TK),
                         _sample_bias_tables(bias_b, 2), lam_p, subln, b_kt, b_v2, lam_init,
                         math.gcd(n_pages, DIFF_PAGES_PER_STEP))
    h2 = _mid(xs, mix_a, mix_b, mem_rows(cache_mem_k[layer]), mem_rows(cache_mem_v[layer]), w_out_b, g_memx, w_q_b,
              gmat128, q_gain, w_o_b, T_PAD, math.gcd(db, MID_SEQS_PER_STEP))
    state = state_ffn_conv[layer]
    st1 = _pad_rows(state[:, 1:2, :], T_PAD).reshape(db * T_PAD, f)
    st2 = _pad_rows(state, T_PAD).reshape(db * T_PAD, f)
    ys, g_s = _ffn_sample(h2.reshape(db * T_PAD, d), g_ffn, w_gate_b, w_up_b, conv_w, conv_b, w_down_b, st1, st2)
    ys = ys.reshape(db, T_PAD, d)[:, :t_new]
    conv_s = g_s.reshape(db, T_PAD, f)[:, t_new - (CONV_W - 1):t_new]

    def new_rows(a, shape):
        return a.reshape(db, T_PAD, -1)[:, :t_new].reshape((1, db, t_new) + shape)

    return (yp, ys,
            from_feature_major(ka_t, (A_KV_HEADS, A_HEAD_DIM)), from_feature_major(va_t, (A_KV_HEADS, A_HEAD_DIM)),
            from_feature_major(ki_t, (IDX_DIM,)), from_feature_major(kb_t, (B_HEADS, 2, B_HEAD_DIM)),
            vb_c.reshape(1, b, s, B_HEADS, 2 * B_HEAD_DIM),
            mk.reshape(1, b, m_tok, MEM_HEADS, MEM_HEAD_DIM), mv.reshape(1, b, m_tok, MEM_HEADS, MEM_HEAD_DIM),
            conv_p[None],
            new_rows(ka_s, (A_KV_HEADS, A_HEAD_DIM)), new_rows(va_s, (A_KV_HEADS, A_HEAD_DIM)),
            new_rows(ki_s, (IDX_DIM,)), new_rows(kb_s, (B_HEADS, 2, B_HEAD_DIM)),
            new_rows(vb_s, (B_HEADS, 2 * B_HEAD_DIM)), conv_s[None])
```

```python
import functools
import math

import jax
import jax.numpy as jnp
import numpy as np
from jax import lax
from jax.experimental import pallas as pl
from jax.experimental.pallas import tpu as pltpu

F32 = jnp.float32
BF16 = jnp.bfloat16
I32 = jnp.int32

EPS = 1e-6
NEG = -1e30
LOG2E = math.log2(math.e)
INT_MIN = -(2 ** 31)

A_HEADS = 8
A_KV_HEADS = 2
A_HEAD_DIM = 64
IDX_HEADS = 4
IDX_DIM = 64
TOPK_MAX = 256
B_HEADS = 4
B_HEAD_DIM = 64
MEM_HEADS = 4
MEM_HEAD_DIM = 128
CONV_W = 3
NUM_BUCKETS = 32
MAX_DISTANCE = 128

A_Q = A_HEADS * A_HEAD_DIM
A_KV = A_KV_HEADS * A_HEAD_DIM
IDX_Q = IDX_HEADS * IDX_DIM
B_QK = B_HEADS * 2 * B_HEAD_DIM
B_V = B_HEADS * 2 * B_HEAD_DIM
B_VH = 2 * B_HEAD_DIM
MEM_W = MEM_HEADS * MEM_HEAD_DIM
A_REP = A_HEADS // A_KV_HEADS

LANES = 128
SUBLANES = 8
TQ_DSA = 512
TQ_DIFF = 512
TK = 128
TKB_MAX = 512
T_PAD = SUBLANES
TM_TOKENS = 512
FEATURE_MAJOR = 0
TKB_DIFF = 512
MID_SEQS_PER_STEP = 8
SELECT_PAGES_PER_STEP = 64
DIFF_PAGES_PER_STEP = 32
VMEM_LIMIT = 56 * 1024 * 1024

C_QA = 0
C_KA = C_QA + A_Q
C_VA = C_KA + A_KV
C_QI = C_VA + A_KV
C_KIW = C_QI + IDX_Q
C_QB = C_KIW + LANES
C_KB = C_QB + B_QK
C_VB = C_KB + B_QK
D_IN_PAD = C_VB + B_V


def _dot(a, b):
    return jnp.dot(a, b, preferred_element_type=F32)


def _dot_nt(a, b):
    return lax.dot_general(a, b, (((1,), (1,)), ((), ())), preferred_element_type=F32)


def _rms(x):
    return x * lax.rsqrt(jnp.mean(x * x, axis=-1, keepdims=True) + EPS)


def _group_mean_sq(x, gm_ref):
    sq = (x * x).astype(BF16)
    gm = gm_ref[:LANES, :LANES]
    return jnp.concatenate([_dot(sq[:, j:j + LANES], gm) for j in range(0, x.shape[1], LANES)], axis=1)


KEY_NEG_INF = -(2 ** 31) + 0x7FFFFF


def _key_to_float(key):
    bits = key ^ ((key >> 31) & 0x7FFFFFFF)
    return jnp.where(key <= KEY_NEG_INF, -jnp.inf, lax.bitcast_convert_type(bits, F32))


def _kth_largest(count_ge, shape, topk):
    def bit_body(it, key):
        cand = key ^ lax.shift_left(jnp.int32(1), 31 - it)
        return jnp.where(count_ge(_key_to_float(cand)) >= topk, cand, key)

    return _key_to_float(lax.fori_loop(0, 32, bit_body, jnp.full(shape, INT_MIN, I32)))


def _cparams(sem, vmem=VMEM_LIMIT):
    return pltpu.CompilerParams(dimension_semantics=sem, vmem_limit_bytes=vmem)


def _const_spec(shape, single_buffer=False):
    nd = len(shape)
    if single_buffer:
        return pl.BlockSpec(shape, lambda *_: (0,) * nd, pipeline_mode=pl.Buffered(1))
    return pl.BlockSpec(shape, lambda *_: (0,) * nd)


def _norm_proj_kernel(x_ref, g_ref, w_ref, gm_ref, hg_ref, *out_refs, segs):
    xn = _rms(x_ref[...]) * g_ref[...]
    p = _dot(xn.astype(BF16), w_ref[...])
    tm = x_ref.shape[0]
    done = {}
    for (start, width, norm, out_width, split), o_ref in zip(segs, out_refs):
        if (start, width, norm) not in done:
            s = p[:, start:start + width]
            if norm:
                s = s * lax.rsqrt(_group_mean_sq(s, gm_ref) + EPS) * hg_ref[:, start:start + width]
            done[(start, width, norm)] = s
        s = done[(start, width, norm)]
        if split == 1:
            o_ref[...] = s[:, :out_width]
        elif split == FEATURE_MAJOR:
            o_ref[0] = s.T[:out_width, :]
        elif split < 0:
            st = s.T
            for u in range(tm // -split):
                o_ref[0, u] = st[:out_width, u * -split:(u + 1) * -split]
        else:
            pw = out_width // split
            for j in range(split):
                o_ref[pl.ds(j, tm, stride=split), :] = s[:, j * pw:(j + 1) * pw]


def _norm_proj(x2d, gain, w, gmat, hgain, segs, tm, name, seq=None):
    m, d = x2d.shape
    n = w.shape[1]
    tiles = None if seq is None else seq // tm

    def shape_spec(ow, sp):
        if sp == FEATURE_MAJOR:
            return (jax.ShapeDtypeStruct((m // seq, ow, seq), F32),
                    pl.BlockSpec((1, ow, tm), lambda i: (i // tiles, 0, i % tiles)))
        if sp < 0:
            return (jax.ShapeDtypeStruct((m // seq, seq // -sp, ow, -sp), F32),
                    pl.BlockSpec((1, tm // -sp, ow, -sp), lambda i: (i // tiles, i % tiles, 0, 0)))
        return (jax.ShapeDtypeStruct((m * sp, ow // sp), F32), pl.BlockSpec((tm * sp, ow // sp), lambda i: (i, 0)))

    shapes, specs = zip(*[shape_spec(ow, sp) for (_, _, _, ow, sp) in segs])
    return pl.pallas_call(
        functools.partial(_norm_proj_kernel, segs=segs),
        out_shape=list(shapes),
        grid=(m // tm,),
        in_specs=[pl.BlockSpec((tm, d), lambda i: (i, 0)), _const_spec((1, d)), _const_spec((d, n)),
                  _const_spec(gmat.shape), _const_spec((1, n))],
        out_specs=list(specs),
        compiler_params=_cparams(("parallel",)),
        name=name,
    )(x2d, gain, w, gmat, hgain)


def _init_flash(m_sc, l_sc, acc_sc):
    m_sc[...] = jnp.full(m_sc.shape, -jnp.inf, F32)
    l_sc[...] = jnp.zeros(l_sc.shape, F32)
    acc_sc[...] = jnp.zeros(acc_sc.shape, F32)


def _flash_update(st, s, pv_fn, m_sc, l_sc, acc_sc):
    m_old = m_sc[st]
    m_new = jnp.maximum(m_old, jnp.max(s, axis=-1, keepdims=True))
    alpha = jnp.exp(m_old - m_new)
    p = jnp.exp(s - m_new)
    l_sc[st] = alpha * l_sc[st] + jnp.sum(p, axis=-1, keepdims=True)
    acc_sc[st] = alpha * acc_sc[st] + pv_fn(p.astype(BF16))
    m_sc[st] = m_new


def _flash_scratch(streams, rows, dv):
    return [pltpu.VMEM((streams, rows, 1), F32), pltpu.VMEM((streams, rows, 1), F32),
            pltpu.VMEM((streams, rows, dv), F32)]


def _select_mask(sk, thr, need, before):
    need = jnp.where(thr > -jnp.inf, need, 0.0)
    return jnp.where(sk > thr, 0.0, jnp.where(sk == thr, jnp.where(before < need, 0.0, NEG), NEG))


def _chunk_bias(dtab_ref, h, i, c, ksub, tq):
    qsub = tq // TK
    return jnp.concatenate(
        [jnp.concatenate([dtab_ref[h, jnp.clip(i * qsub + a - (c * ksub + j) + 1, 0, 3)] for a in range(qsub)], axis=1)
         for j in range(ksub)], axis=0)


def _flash_update_t(st, s, vt, m_sc, l_sc, acc_sc, diagonal=False):
    m_old = m_sc[st]
    col_max = lambda x: jnp.max(_col_reduce(x, jnp.max), axis=0, keepdims=True)
    col_sum = lambda x: jnp.sum(_col_reduce(x, jnp.sum), axis=0, keepdims=True)
    if not diagonal:
        m_new = jnp.maximum(m_old, col_max(s))
        p = jnp.exp2(s - m_new)
        p_sum = col_sum(p)
    else:
        rows = s.shape[0]
        m_parts, p_parts, sum_parts = [], [], []
        for a in range(s.shape[1] // TK):
            lanes = slice(a * TK, (a + 1) * TK)
            live = (a + 1) * TK
            sa = s[0:live, lanes]
            ma = jnp.maximum(m_old[:, lanes], col_max(sa))
            pa = jnp.exp2(sa - ma)
            m_parts.append(ma)
            sum_parts.append(col_sum(pa))
            p_parts.append(pa if live == rows else jnp.concatenate([pa, jnp.zeros((rows - live, TK), F32)], axis=0))
        m_new = jnp.concatenate(m_parts, axis=1)
        p = jnp.concatenate(p_parts, axis=1)
        p_sum = jnp.concatenate(sum_parts, axis=1)
    alpha = jnp.exp2(m_old - m_new)
    l_sc[st] = alpha * l_sc[st] + p_sum
    acc_sc[st] = alpha * acc_sc[st] + _dot(vt, p.astype(BF16))
    m_sc[st] = m_new


def _flash_scratch_t(streams, dv, cols):
    return [pltpu.VMEM((streams, 1, cols), F32), pltpu.VMEM((streams, 1, cols), F32),
            pltpu.VMEM((streams, dv, cols), F32)]


REDUCE_WAYS = 8


def _col_reduce(x, op):
    rows, cols = x.shape
    slabs = rows // SUBLANES
    if slabs % REDUCE_WAYS == 0 and slabs > REDUCE_WAYS:
        x = op(x.reshape(REDUCE_WAYS, slabs // REDUCE_WAYS, SUBLANES, cols), axis=1)
    else:
        x = x.reshape(slabs, SUBLANES, cols)
    return op(x, axis=0)


def _col_count(w):
    return _col_reduce(w, jnp.sum)


def _dsa_prompt_kernel(qi_ref, wit_ref, kiw_ref, qa_ref, ka_ref, vat_ref, dtab_ref, tril_ref, o_ref,
                       skey_ref, selb_ref, m_sc, l_sc, acc_sc, *, topk, tkb):
    tq = TQ_DSA
    i = pl.program_id(1)
    ksub = tkb // TK
    nbig = (i * tq + tq - 1) // tkb + 1
    qi = qi_ref[0].astype(BF16)
    wit = wit_ref[0] * (IDX_HEADS ** -0.5 * IDX_DIM ** -0.5)
    krow = lax.broadcasted_iota(I32, (tkb, tq), 0)
    qcol = lax.broadcasted_iota(I32, (tkb, tq), 1)

    def score_body(c, carry):
        off = pl.multiple_of(c * tkb, tkb)
        kc = kiw_ref[0, pl.ds(off, tkb), 0:IDX_DIM].astype(BF16)
        sc = jnp.zeros((tkb, tq), F32)
        for h in range(IDX_HEADS):
            d = _dot_nt(kc, qi[:, h * IDX_DIM:(h + 1) * IDX_DIM])
            sc = sc + jnp.maximum(d, 0.0) * wit[h:h + 1, :]
        causal = (c * tkb + krow) <= (i * tq + qcol)
        skey_ref[c] = jnp.where(causal, sc, -jnp.inf)
        return carry

    lax.fori_loop(0, nbig, score_body, 0)

    diagonal_last = tq == tkb

    def count(cmp, t):
        def body(c, cnt):
            return cnt + _col_count(jnp.where(cmp(skey_ref[c], t), 1.0, 0.0))
        full = nbig - 1 if diagonal_last else nbig
        cnt = lax.fori_loop(0, full, body, jnp.zeros((SUBLANES, tq), F32))
        if diagonal_last:
            parts = []
            for a in range(tq // TK):
                blk = skey_ref[nbig - 1, 0:(a + 1) * TK, a * TK:(a + 1) * TK]
                parts.append(_col_count(jnp.where(cmp(blk, t[:, a * TK:(a + 1) * TK]), 1.0, 0.0)))
            cnt = cnt + jnp.concatenate(parts, axis=1)
        return jnp.sum(cnt, axis=0, keepdims=True)

    thr = _kth_largest(lambda t: count(lambda sk, tt: sk >= tt, t), (1, tq), topk)
    need = float(topk) - count(lambda sk, tt: sk > tt, thr)
    tril = tril_ref[...]

    def sel_body(c, off):
        sk = skey_ref[c]
        eqf = jnp.where(sk == thr, 1.0, 0.0)
        before = _dot(tril, eqf.astype(BF16)) + off
        selb_ref[c] = _select_mask(sk, thr, need, before)
        return off + jnp.sum(_col_count(eqf), axis=0, keepdims=True)

    lax.fori_loop(0, nbig, sel_body, jnp.zeros((1, tq), F32))

    _init_flash(m_sc, l_sc, acc_sc)
    scale = A_HEAD_DIM ** -0.5 * LOG2E
    qg = [jnp.concatenate([qa_ref[0, :, h * A_HEAD_DIM:(h + 1) * A_HEAD_DIM] * scale
                           for h in range(g * A_REP, (g + 1) * A_REP)], axis=0).astype(BF16)
          for g in range(A_KV_HEADS)]

    def att_body(c, carry, diagonal=False):
        off = pl.multiple_of(c * tkb, tkb)
        maskb = selb_ref[c]
        for g in range(A_KV_HEADS):
            g0 = g * A_HEAD_DIM
            kc = ka_ref[0, pl.ds(off, tkb), g0:g0 + A_HEAD_DIM].astype(BF16)
            vt = vat_ref[0, c, g0:g0 + A_HEAD_DIM, :].astype(BF16)
            sg = _dot_nt(kc, qg[g])
            for r in range(A_REP):
                h = g * A_REP + r
                s = sg[:, r * tq:(r + 1) * tq] + (_chunk_bias(dtab_ref, h, i, c, ksub, tq) + maskb)
                _flash_update_t(h, s, vt, m_sc, l_sc, acc_sc, diagonal)
        return carry

    if diagonal_last:
        lax.fori_loop(0, nbig - 1, att_body, 0)
        att_body(nbig - 1, 0, diagonal=True)
    else:
        lax.fori_loop(0, nbig, att_body, 0)
    for h in range(0, A_HEADS, 2):
        ot = jnp.concatenate([acc_sc[h] / l_sc[h], acc_sc[h + 1] / l_sc[h + 1]], axis=0)
        o_ref[0, :, h * A_HEAD_DIM:(h + 2) * A_HEAD_DIM] = ot.T


def _dsa_prompt(qi, kiw, qa, ka, vat, dtab, tkb):
    b, s, _ = qa.shape
    tq = TQ_DSA
    nq = s // tq
    nc = s // tkb
    topk = min(TOPK_MAX, s // 4)
    tril = (jnp.arange(tkb)[None, :] < jnp.arange(tkb)[:, None]).astype(BF16)
    wit = jnp.swapaxes(kiw[:, :, IDX_DIM:IDX_DIM + SUBLANES], 1, 2)
    blk = lambda w: pl.BlockSpec((1, tq, w), lambda bi, i: (bi, i, 0))
    full = lambda w: pl.BlockSpec((1, s, w), lambda bi, i: (bi, 0, 0))
    return pl.pallas_call(
        functools.partial(_dsa_prompt_kernel, topk=topk, tkb=tkb),
        out_shape=jax.ShapeDtypeStruct((b, s, A_Q), F32),
        grid=(b, nq),
        in_specs=[blk(IDX_Q), pl.BlockSpec((1, SUBLANES, tq), lambda bi, i: (bi, 0, i)), full(LANES), blk(A_Q),
                  full(A_KV), pl.BlockSpec((1, nc, A_KV, tkb), lambda bi, i: (bi, 0, 0, 0)),
                  _const_spec(dtab.shape), _const_spec(tril.shape)],
        out_specs=blk(A_Q),
        scratch_shapes=[pltpu.VMEM((nc, tkb, tq), F32), pltpu.VMEM((nc, tkb, tq), F32)]
        + _flash_scratch_t(A_HEADS, A_HEAD_DIM, tq),
        compiler_params=_cparams(("parallel", "arbitrary")),
        name="dsa_prompt",
    )(qi, wit, kiw, qa, ka, vat, dtab, tril)


def _diff_lambda(lp_ref, lam_init):
    lp = lp_ref[...]
    s1 = jnp.sum(lp[0:1] * lp[1:2], axis=-1, keepdims=True)
    s2 = jnp.sum(lp[2:3] * lp[3:4], axis=-1, keepdims=True)
    return jnp.exp(s1) - jnp.exp(s2) + lam_init


def _diff_finish(o0, o1, lam, sg, lam_init):
    o = o0 - lam * o1
    return _rms(o) * sg * (1.0 - lam_init)


def _diff_prompt_kernel(q_ref, k_ref, vt_ref, dtab_ref, lp_ref, sg_ref, o_ref, m_sc, l_sc, acc_sc, *, lam_init, tkb):
    tq = TQ_DIFF
    i = pl.program_id(1)
    ksub = tkb // TK
    nbig = (i * tq + tq - 1) // tkb + 1
    scale = B_HEAD_DIM ** -0.5 * LOG2E
    _init_flash(m_sc, l_sc, acc_sc)

    def body(c, carry, diagonal=False):
        off = pl.multiple_of(c * tkb, tkb)
        for h in range(B_HEADS):
            bias = _chunk_bias(dtab_ref, h, i, c, ksub, tq)
            vt = vt_ref[0, c, h * B_VH:(h + 1) * B_VH, :].astype(BF16)
            for comp in range(2):
                st = h * 2 + comp
                c0 = st * B_HEAD_DIM
                qh = (q_ref[0, :, c0:c0 + B_HEAD_DIM] * scale).astype(BF16)
                kc = k_ref[0, pl.ds(off, tkb), c0:c0 + B_HEAD_DIM].astype(BF16)
                _flash_update_t(st, _dot_nt(kc, qh) + bias, vt, m_sc, l_sc, acc_sc, diagonal)
        return carry

    if tq == tkb:
        lax.fori_loop(0, nbig - 1, body, 0)
        body(nbig - 1, 0, diagonal=True)
    else:
        lax.fori_loop(0, nbig, body, 0)
    lam = _diff_lambda(lp_ref, lam_init)
    for h in range(B_HEADS):
        o0 = (acc_sc[2 * h] / l_sc[2 * h]).T
        o1 = (acc_sc[2 * h + 1] / l_sc[2 * h + 1]).T
        o_ref[0, :, h * B_VH:(h + 1) * B_VH] = _diff_finish(o0, o1, lam, sg_ref[...], lam_init)


def _diff_prompt(qb, kb, vbt, dtab, lam_p, subln, lam_init, tkb):
    b, s, _ = qb.shape
    tq = TQ_DIFF
    nq = s // tq
    nc = s // tkb
    blk = lambda w: pl.BlockSpec((1, tq, w), lambda bi, i: (bi, i, 0))
    full = lambda w: pl.BlockSpec((1, s, w), lambda bi, i: (bi, 0, 0))
    return pl.pallas_call(
        functools.partial(_diff_prompt_kernel, lam_init=lam_init, tkb=tkb),
        out_shape=jax.ShapeDtypeStruct((b, s, B_V), F32),
        grid=(b, nq),
        in_specs=[blk(B_QK), full(B_QK), pl.BlockSpec((1, nc, B_V, tkb), lambda bi, i: (bi, 0, 0, 0)),
                  _const_spec(dtab.shape), _const_spec(lam_p.shape), _const_spec(subln.shape)],
        out_specs=blk(B_V),
        scratch_shapes=_flash_scratch_t(2 * B_HEADS, B_VH, tq),
        compiler_params=_cparams(("parallel", "arbitrary")),
        name="diff_prompt",
    )(qb, kb, vbt, dtab, lam_p, subln)


def _page_specs(pps, rows, cols):
    return [pl.BlockSpec((1, rows, cols), functools.partial(lambda b, s, pt, j: (pt[b, s * pps + j], 0, 0), j=j))
            for j in range(pps)]


def _all_page_specs(npages, rows, cols):
    return [pl.BlockSpec((1, rows, cols), functools.partial(lambda b, pt, j: (pt[b, j], 0, 0), j=j))
            for j in range(npages)]


def _dsa_sample_select_kernel(pt_ref, q_ref, wi_ref, kinew_ref, *rest, pps, npages, topk, group):
    del pt_ref
    page_refs = rest[:pps]
    tri_ref, selb_ref, sc_ref = rest[pps:]
    b = pl.program_id(0)
    s = pl.program_id(1)
    np1 = npages + 1
    q = q_ref[0].astype(BF16)
    wi = wi_ref[0] * (IDX_HEADS ** -0.5 * IDX_DIM ** -0.5)

    def scores(kt):
        return jnp.sum((jnp.maximum(_dot(q, kt), 0.0) * wi).reshape(IDX_HEADS, T_PAD, TK), axis=0)

    base = b * np1
    for j in range(pps):
        sc_ref[base + s * pps + j] = scores(page_refs[j][0].astype(BF16))

    last_page_step = s == pl.num_programs(1) - 1

    @pl.when(last_page_step)
    def _():
        row = lax.broadcasted_iota(I32, (T_PAD, TK), 0)
        col = lax.broadcasted_iota(I32, (T_PAD, TK), 1)
        sc_ref[base + npages] = jnp.where(col <= row, scores(kinew_ref[0].astype(BF16)), -jnp.inf)

    @pl.when(last_page_step & (b == pl.num_programs(0) - 1))
    def _():
        def group_body(gi, carry):
            sk = sc_ref[pl.ds(gi * (group * np1), group * np1)].reshape(group, np1, T_PAD, TK)

            def count(pred):
                cnt = jnp.sum(jnp.where(pred, 1.0, 0.0), axis=1)
                return jnp.sum(cnt, axis=-1, keepdims=True)

            thr = _kth_largest(lambda t: count(sk >= t[:, None]), (group, T_PAD, 1), topk)
            need = float(topk) - count(sk > thr[:, None])
            eqf = jnp.where(sk == thr[:, None], 1.0, 0.0)
            before = _dot(eqf.reshape(group * np1 * T_PAD, TK).astype(BF16), tri_ref[...]).reshape(sk.shape)
            ties = jnp.sum(eqf, axis=-1, keepdims=True)
            off = jnp.zeros((group, T_PAD, 1), F32)
            for c in range(np1):
                selb_ref[pl.ds(gi * group, group), c] = _select_mask(sk[:, c], thr, need, before[:, c] + off)
                off = off + ties[:, c]
            return carry

        lax.fori_loop(0, pl.num_programs(0) // group, group_body, 0)


def _dsa_sample_select(page_table, q_rows, wi_rows, ki_new_t, cache_ikt, n_new, pps):
    db, npages = page_table.shape
    np1 = npages + 1
    topk = min(TOPK_MAX, (npages * TK + n_new) // 4)
    group = math.gcd(db, SUBLANES)
    tri = (jnp.arange(TK)[:, None] < jnp.arange(TK)[None, :]).astype(BF16)
    per_b = lambda shape: pl.BlockSpec((1,) + shape, lambda b, s, pt: (b,) + (0,) * len(shape))
    grid_spec = pltpu.PrefetchScalarGridSpec(
        num_scalar_prefetch=1,
        grid=(db, npages // pps),
        in_specs=[per_b(q_rows.shape[1:]), per_b(wi_rows.shape[1:]), per_b((IDX_DIM, TK))]
        + _page_specs(pps, IDX_DIM, TK) + [pl.BlockSpec(tri.shape, lambda b, s, pt: (0, 0))],
        out_specs=pl.BlockSpec((db, np1, T_PAD, TK), lambda b, s, pt: (0, 0, 0, 0)),
        scratch_shapes=[pltpu.VMEM((db * np1, T_PAD, TK), F32)],
    )
    return pl.pallas_call(
        functools.partial(_dsa_sample_select_kernel, pps=pps, npages=npages, topk=topk, group=group),
        out_shape=jax.ShapeDtypeStruct((db, np1, T_PAD, TK), F32),
        grid_spec=grid_spec,
        compiler_params=_cparams(("arbitrary", "arbitrary")),
        name="dsa_sample_select",
    )(page_table, q_rows, wi_rows, ki_new_t, *([cache_ikt] * pps), tri)


def _page_bias(btab_ref, c0, n, npages):
    return jnp.concatenate([btab_ref[jnp.where(c0 + j == npages - 1, 1, 0)] for j in range(n)], axis=1)


def _dsa_sample_attn_kernel(pt_ref, q_ref, knew_ref, vnew_ref, selb_ref, btab_ref, *rest, npages):
    del pt_ref
    o_ref = rest[2 * npages]
    kts = [r[0] for r in rest[:npages]] + [knew_ref[0]]
    vts = [r[0] for r in rest[npages:2 * npages]] + [vnew_ref[0]]
    n = npages + 1
    rows = A_HEADS * T_PAD
    scale = A_HEAD_DIM ** -0.5
    q = (q_ref[0] * scale).astype(BF16)
    sc = jnp.concatenate([_dot(q, kt.astype(BF16)) for kt in kts], axis=1)
    bias = jnp.concatenate([btab_ref[0]] * (npages - 1) + [btab_ref[1], btab_ref[2]], axis=1)
    mask = jnp.concatenate([selb_ref[0, c] for c in range(n)], axis=1)
    sc = ((sc + bias).reshape(A_HEADS, T_PAD, n * TK) + mask[None]).reshape(rows, n * TK)
    p = jnp.exp(sc - jnp.max(sc, axis=-1, keepdims=True))
    l = jnp.sum(p, axis=-1, keepdims=True)
    p = p.astype(BF16)
    acc = _dot_nt(p[:, 0:TK], vts[0].astype(BF16))
    for j in range(1, n):
        acc = acc + _dot_nt(p[:, j * TK:(j + 1) * TK], vts[j].astype(BF16))
    o = acc / l
    for h in range(A_HEADS):
        g0 = (h // A_REP) * A_HEAD_DIM
        o_ref[0, :, h * A_HEAD_DIM:(h + 1) * A_HEAD_DIM] = o[h * T_PAD:(h + 1) * T_PAD, g0:g0 + A_HEAD_DIM]


def _dsa_sample_attn(page_table, q_bd, k_new_t, v_new_t, selb, btab, cache_kt, cache_vt):
    db, npages = page_table.shape
    rows = A_HEADS * T_PAD
    per_b = lambda shape: pl.BlockSpec((1,) + shape, lambda b, pt: (b,) + (0,) * len(shape))
    grid_spec = pltpu.PrefetchScalarGridSpec(
        num_scalar_prefetch=1,
        grid=(db,),
        in_specs=[per_b((rows, A_KV)), per_b((A_KV, TK)), per_b((A_KV, TK)), per_b((npages + 1, T_PAD, TK)),
                  pl.BlockSpec(btab.shape, lambda b, pt: (0, 0, 0))]
        + _all_page_specs(npages, A_KV, TK) + _all_page_specs(npages, A_KV, TK),
        out_specs=per_b((T_PAD, A_Q)),
    )
    return pl.pallas_call(
        functools.partial(_dsa_sample_attn_kernel, npages=npages),
        out_shape=jax.ShapeDtypeStruct((db, T_PAD, A_Q), F32),
        grid_spec=grid_spec,
        compiler_params=_cparams(("parallel",)),
        name="dsa_sample_attn",
    )(page_table, q_bd, k_new_t, v_new_t, selb, btab, *([cache_kt] * npages), *([cache_vt] * npages))


def _diff_sample_kernel(pt_ref, q_ref, knew_ref, vnew_ref, btab_ref, lp_ref, sg_ref, *rest, pps, npages, lam_init):
    del pt_ref
    k_refs = rest[:pps]
    v_refs = rest[pps:2 * pps]
    o_ref, m_sc, l_sc, acc_sc = rest[2 * pps:]
    s = pl.program_id(1)

    @pl.when(s == 0)
    def _():
        _init_flash(m_sc, l_sc, acc_sc)

    scale = B_HEAD_DIM ** -0.5
    q = (q_ref[0] * scale).astype(BF16)
    hrows = 2 * T_PAD

    def step(kt_list, v_fn, bias):
        n = len(kt_list)
        sc = jnp.concatenate([_dot(q, kt.astype(BF16)) for kt in kt_list], axis=1) + bias

        def pv(p):
            outs = []
            for h in range(B_HEADS):
                ph = p[h * hrows:(h + 1) * hrows]
                out = _dot(ph[:, 0:TK], v_fn(0, h))
                for j in range(1, n):
                    out = out + _dot(ph[:, j * TK:(j + 1) * TK], v_fn(j, h))
                outs.append(out)
            return jnp.concatenate(outs, axis=0)

        _flash_update(0, sc, pv, m_sc, l_sc, acc_sc)

    def page_v(j, h):
        return v_refs[j][0, pl.ds(h, TK, stride=B_HEADS), :].astype(BF16)

    step([r[0] for r in k_refs], page_v, _page_bias(btab_ref, s * pps, pps, npages))

    @pl.when(s == pl.num_programs(1) - 1)
    def _():
        step([knew_ref[0]], lambda j, h: vnew_ref[0, :, h * B_VH:(h + 1) * B_VH].astype(BF16), btab_ref[2])
        o = acc_sc[0] / l_sc[0]
        lam = _diff_lambda(lp_ref, lam_init)
        for h in range(B_HEADS):
            r0 = h * hrows
            o_ref[0, :, h * B_VH:(h + 1) * B_VH] = _diff_finish(
                o[r0:r0 + T_PAD], o[r0 + T_PAD:r0 + hrows], lam, sg_ref[...], lam_init)


def _diff_sample(page_table, q_bd, k_new_t, v_new, btab, lam_p, subln, cache_kt, cache_v2, lam_init, pps):
    db, npages = page_table.shape
    rows = 2 * B_HEADS * T_PAD
    per_b = lambda shape: pl.BlockSpec((1,) + shape, lambda b, s, pt: (b,) + (0,) * len(shape))
    const = lambda shape: pl.BlockSpec(shape, lambda b, s, pt: (0,) * len(shape))
    grid_spec = pltpu.PrefetchScalarGridSpec(
        num_scalar_prefetch=1,
        grid=(db, npages // pps),
        in_specs=[per_b((rows, B_QK)), per_b((B_QK, TK)), per_b((TK, B_V)), const(btab.shape),
                  const(lam_p.shape), const(subln.shape)]
        + _page_specs(pps, B_QK, TK) + _page_specs(pps, TK * B_HEADS, B_VH),
        out_specs=per_b((T_PAD, B_V)),
        scratch_shapes=_flash_scratch(1, rows, B_VH),
    )
    return pl.pallas_call(
        functools.partial(_diff_sample_kernel, pps=pps, npages=npages, lam_init=lam_init),
        out_shape=jax.ShapeDtypeStruct((db, T_PAD, B_V), F32),
        grid_spec=grid_spec,
        compiler_params=_cparams(("parallel", "arbitrary")),
        name="diff_sample",
    )(page_table, q_bd, k_new_t, v_new, btab, lam_p, subln, *([cache_kt] * pps), *([cache_v2] * pps))


def _mid_kernel(x_ref, ma_ref, mb_ref, mk_ref, mv_ref, wout_ref, gx_ref, wq_ref, gm_ref, qg_ref, wo_ref, h2_ref):
    group, tm, d = x_ref.shape
    rows = lambda ref: ref[...].reshape(group * tm, ref.shape[-1])
    h = (rows(x_ref) + _dot(rows(ma_ref).astype(BF16), wout_ref[:A_Q, :])
         + _dot(rows(mb_ref).astype(BF16), wout_ref[A_Q:, :]))
    hn = _rms(h) * gx_ref[...]
    q = _dot(hn.astype(BF16), wq_ref[...])
    q = (q * lax.rsqrt(_group_mean_sq(q, gm_ref) + EPS) * qg_ref[...]).astype(BF16)
    m_tok = mk_ref.shape[1] // MEM_HEADS
    o_rows = []
    for g in range(group):
        outs = []
        for hh in range(MEM_HEADS):
            sl = slice(hh * MEM_HEAD_DIM, (hh + 1) * MEM_HEAD_DIM)
            mk = mk_ref[g, pl.ds(hh, m_tok, stride=MEM_HEADS), :].astype(BF16)
            mv = mv_ref[g, pl.ds(hh, m_tok, stride=MEM_HEADS), :].astype(BF16)
            s = _dot_nt(q[g * tm:(g + 1) * tm, sl], mk) * (MEM_HEAD_DIM ** -0.5)
            p = jnp.exp(s - jnp.max(s, axis=-1, keepdims=True))
            l = jnp.sum(p, axis=-1, keepdims=True)
            outs.append(_dot(p.astype(BF16), mv) / l)
        o_rows.append(jnp.concatenate(outs, axis=-1))
    o = jnp.concatenate(o_rows, axis=0)
    h2_ref[...] = (h + _dot(o.astype(BF16), wo_ref[...])).reshape(group, tm, d)


def _mid(x, mix_a, mix_b, mk, mv, w_out, g_x, w_q, gmat, q_gain, w_o, tm, group):
    b, s, d = x.shape
    blk = lambda w: pl.BlockSpec((group, tm, w), lambda bi, i: (bi, i, 0))
    per_b = lambda w: pl.BlockSpec((group, mk.shape[1], w), lambda bi, i: (bi, 0, 0))
    return pl.pallas_call(
        _mid_kernel,
        out_shape=jax.ShapeDtypeStruct((b, s, d), F32),
        grid=(b // group, s // tm),
        in_specs=[blk(d), blk(A_Q), blk(B_V), per_b(MEM_HEAD_DIM), per_b(MEM_HEAD_DIM), _const_spec(w_out.shape),
                  _const_spec(g_x.shape), _const_spec(w_q.shape), _const_spec(gmat.shape),
                  _const_spec(q_gain.shape), _const_spec(w_o.shape)],
        out_specs=blk(d),
        compiler_params=_cparams(("parallel", "arbitrary")),
        name="mid",
    )(x, mix_a, mix_b, mk, mv, w_out, g_x, w_q, gmat, q_gain, w_o)


def _ffn_core(h, gn, wg_ref, wu_ref, cw_ref, cb_ref, wd_ref, shifted):
    xb = (_rms(h) * gn).astype(BF16)
    g = _dot(xb, wg_ref[...])
    u = _dot(xb, wu_ref[...])
    gm1, gm2 = shifted(g)
    gc = cb_ref[...] + cw_ref[0:1, :] * gm2 + cw_ref[1:2, :] * gm1 + cw_ref[2:3, :] * g
    a = gc / (1.0 + jnp.exp(-gc)) * u
    return h + _dot(a.astype(BF16), wd_ref[...]), g


def _ffn_prompt_kernel(h_ref, gn_ref, wg_ref, wu_ref, cw_ref, cb_ref, wd_ref, y_ref, tail_ref, carry_ref):
    @pl.when(pl.program_id(1) == 0)
    def _():
        carry_ref[...] = jnp.zeros(carry_ref.shape, F32)

    tm = h_ref.shape[1]
    row = lax.broadcasted_iota(I32, (tm, 1), 0)
    c0 = carry_ref[SUBLANES - 2:SUBLANES - 1, :]
    c1 = carry_ref[SUBLANES - 1:SUBLANES, :]

    def shifted(g):
        gm1 = jnp.where(row == 0, c1, pltpu.roll(g, 1, 0))
        gm2 = jnp.where(row == 0, c0, jnp.where(row == 1, c1, pltpu.roll(g, 2, 0)))
        return gm1, gm2

    y, g = _ffn_core(h_ref[0], gn_ref[...], wg_ref, wu_ref, cw_ref, cb_ref, wd_ref, shifted)
    y_ref[0] = y
    tail = g[tm - SUBLANES:, :]
    carry_ref[...] = tail
    tail_ref[0] = tail


def _ffn_prompt(h, gn, wg, wu, cw, cb, wd, tm):
    b, s, d = h.shape
    f = wg.shape[1]
    blk = pl.BlockSpec((1, tm, d), lambda bi, i: (bi, i, 0))
    wspec = lambda shape: _const_spec(shape, single_buffer=True)
    return pl.pallas_call(
        _ffn_prompt_kernel,
        out_shape=[jax.ShapeDtypeStruct((b, s, d), F32), jax.ShapeDtypeStruct((b, SUBLANES, f), F32)],
        grid=(b, s // tm),
        in_specs=[blk, _const_spec(gn.shape), wspec(wg.shape), wspec(wu.shape), _const_spec(cw.shape),
                  _const_spec(cb.shape), wspec(wd.shape)],
        out_specs=[blk, pl.BlockSpec((1, SUBLANES, f), lambda bi, i: (bi, 0, 0))],
        scratch_shapes=[pltpu.VMEM((SUBLANES, f), F32)],
        compiler_params=_cparams(("arbitrary", "arbitrary")),
        name="ffn_prompt",
    )(h, gn, wg, wu, cw, cb, wd)


def _ffn_sample_kernel(h_ref, gn_ref, wg_ref, wu_ref, cw_ref, cb_ref, wd_ref, st1_ref, st2_ref, y_ref, g_ref):
    m = h_ref.shape[0]
    t = lax.broadcasted_iota(I32, (m, 1), 0) & (T_PAD - 1)

    def shifted(g):
        gm1 = jnp.where(t == 0, st1_ref[...], pltpu.roll(g, 1, 0))
        gm2 = jnp.where(t < 2, st2_ref[...], pltpu.roll(g, 2, 0))
        return gm1, gm2

    y, g = _ffn_core(h_ref[...], gn_ref[...], wg_ref, wu_ref, cw_ref, cb_ref, wd_ref, shifted)
    y_ref[...] = y
    g_ref[...] = g


def _ffn_sample(h2d, gn, wg, wu, cw, cb, wd, st1, st2):
    m, d = h2d.shape
    f = wg.shape[1]
    wspec = lambda shape: _const_spec(shape, single_buffer=True)
    return pl.pallas_call(
        _ffn_sample_kernel,
        out_shape=[jax.ShapeDtypeStruct((m, d), F32), jax.ShapeDtypeStruct((m, f), F32)],
        grid=(1,),
        in_specs=[_const_spec((m, d)), _const_spec(gn.shape), wspec(wg.shape), wspec(wu.shape),
                  _const_spec(cw.shape), _const_spec(cb.shape), wspec(wd.shape), _const_spec((m, f)),
                  _const_spec((m, f))],
        out_specs=[_const_spec((m, d)), _const_spec((m, f))],
        compiler_params=_cparams(("arbitrary",)),
        name="ffn_sample",
    )(h2d, gn, wg, wu, cw, cb, wd, st1, st2)


def _rel_bucket(dist):
    n = np.maximum(dist, 0)
    max_exact = NUM_BUCKETS // 2
    nf = np.maximum(n, 1).astype(np.float32)
    log_b = (np.log(nf / np.float32(max_exact)) / np.float32(math.log(MAX_DISTANCE / max_exact))
             * np.float32(NUM_BUCKETS - max_exact))
    large = np.minimum(max_exact + log_b.astype(np.int32), NUM_BUCKETS - 1)
    return np.where(n < max_exact, n, large)


def _bias_by_dist(dist, causal, bias):
    onehot = (_rel_bucket(dist)[..., None] == np.arange(NUM_BUCKETS)).astype(np.float32)
    vals = jnp.einsum("...k,kh->h...", onehot, bias, precision=lax.Precision.HIGHEST)
    return jnp.where(causal[None], vals, NEG).astype(F32)


def _prompt_bias_tables(bias):
    r = np.arange(TK)[:, None]
    c = np.arange(TK)[None, :]
    always = np.ones((TK, TK), bool)
    masked = _bias_by_dist(r - c, ~always, bias)
    t0 = _bias_by_dist(r - c, r >= c, bias)
    t1 = _bias_by_dist(r - c + TK, always, bias)
    t2 = _bias_by_dist(r - c + 2 * TK, always, bias)
    return jnp.swapaxes(jnp.stack([masked, t0, t1, t2], axis=1), -1, -2) * LOG2E


def _sample_bias_tables(bias, streams_per_head):
    t = np.arange(T_PAD)[:, None]
    c = np.arange(TK)[None, :]
    always = np.ones((T_PAD, TK), bool)
    far = _bias_by_dist(t - c + 2 * TK, always, bias)
    last = _bias_by_dist(t - c + TK, always, bias)
    new = _bias_by_dist(t - c, c <= t, bias)
    tabs = jnp.stack([far, last, new], axis=0)
    tabs = jnp.repeat(tabs[:, :, None], streams_per_head, axis=2)
    return tabs.reshape(3, -1, TK)


def _group_mean_matrix(width, group):
    idx = jnp.arange(width) // group
    return jnp.where(idx[:, None] == idx[None, :], 1.0 / group, 0.0).astype(BF16)


def _pad_rows(x, rows):
    return jnp.pad(x, ((0, 0), (0, rows - x.shape[1]), (0, 0)))


def _new_keys_t(x, rows):
    return jnp.swapaxes(_pad_rows(x, rows), 1, 2)


def kernel(x_prompt, x_sample, mem_prompt, cache_a_k, cache_a_v, cache_idx_k, cache_b_k, cache_b_v, cache_mem_k, cache_mem_v, state_ffn_conv, page_table, rel_bias, norm_mix, w_in, a_q_norm, a_k_norm, b_q_norm, b_k_norm, diff_lambda, diff_subln, w_out, norm_mem_x, norm_mem_src, w_mem_q, w_mem_kv, mem_q_norm, mem_k_norm, w_mem_o, norm_ffn, w_up, w_gate, ffn_conv_w, ffn_conv_b, w_down):
    depth = w_in.shape[0]
    assert depth == 1, "single-layer trunk"
    layer = 0
    lam_init = 0.8 - 0.6 * math.exp(-0.3 * layer)
    b, s, d = x_prompt.shape
    db, t_new, _ = x_sample.shape
    assert CONV_W - 1 <= t_new <= T_PAD and cache_a_k.shape[2] == TK
    m_tok = mem_prompt.shape[1]
    f = w_up.shape[-1]
    n_pool = cache_a_k.shape[1]
    tkb = min(TKB_MAX, s)
    tkb_diff = min(TKB_DIFF, s)
    assert s % tkb == 0 and tkb % TK == 0 and s % tkb_diff == 0 and tkb_diff % TK == 0

    w_in_l = w_in[layer]
    n_front = A_Q + 2 * A_KV + IDX_Q + IDX_DIM + IDX_HEADS
    w_in_p = jnp.concatenate(
        [w_in_l[:, :n_front], jnp.zeros((d, C_QB - n_front), F32), w_in_l[:, n_front:]], axis=1).astype(BF16)
    ones = lambda n: jnp.ones((n,), F32)
    hgain = jnp.concatenate([
        jnp.tile(a_q_norm[layer], A_HEADS), jnp.tile(a_k_norm[layer], A_KV_HEADS), ones(C_QB - C_VA),
        jnp.tile(b_q_norm[layer], 2 * B_HEADS), jnp.tile(b_k_norm[layer], 2 * B_HEADS), ones(B_V)])[None, :]
    gmat64 = _group_mean_matrix(A_Q, A_HEAD_DIM)
    gmat128 = _group_mean_matrix(MEM_W, MEM_HEAD_DIM)
    proj_segs = ((C_QA, A_Q, True, A_Q, 1), (C_KA, A_KV, True, A_KV, 1), (C_VA, A_KV, False, A_KV, 1),
                 (C_QI, IDX_Q, False, IDX_Q, 1), (C_KIW, LANES, False, LANES, 1), (C_KIW, LANES, False, IDX_DIM, 1),
                 (C_QB, B_QK, True, B_QK, 1), (C_KB, B_QK, True, B_QK, 1), (C_VB, B_V, False, B_V, 1))
    prompt_segs = ((C_QA, A_Q, True, A_Q, 1), (C_KA, A_KV, True, A_KV, 1), (C_QI, IDX_Q, False, IDX_Q, 1),
                   (C_KIW, LANES, False, LANES, 1), (C_QB, B_QK, True, B_QK, 1), (C_KB, B_QK, True, B_QK, 1),
                   (C_VA, A_KV, False, A_KV, -tkb), (C_VB, B_V, False, B_V, -tkb_diff),
                   (C_KA, A_KV, True, A_KV, FEATURE_MAJOR), (C_VA, A_KV, False, A_KV, FEATURE_MAJOR),
                   (C_KIW, LANES, False, IDX_DIM, FEATURE_MAJOR), (C_KB, B_QK, True, B_QK, FEATURE_MAJOR),
                   (C_VB, B_V, False, B_V, B_HEADS))
    g_mix = norm_mix[layer][None, :]
    w_out_b = w_out[layer].astype(BF16)
    w_q_b = w_mem_q[layer].astype(BF16)
    w_o_b = w_mem_o[layer].astype(BF16)
    w_kv_b = w_mem_kv[layer].astype(BF16)
    w_gate_b = w_gate[layer].astype(BF16)
    w_up_b = w_up[layer].astype(BF16)
    w_down_b = w_down[layer].astype(BF16)
    g_memx = norm_mem_x[layer][None, :]
    q_gain = jnp.tile(mem_q_norm[layer], MEM_HEADS)[None, :]
    kv_gain = jnp.concatenate([jnp.tile(mem_k_norm[layer], MEM_HEADS), ones(MEM_W)])[None, :]
    g_ffn = norm_ffn[layer][None, :]
    conv_w = ffn_conv_w[layer]
    conv_b = ffn_conv_b[layer][None, :]
    lam_p = diff_lambda[layer]
    subln = diff_subln[layer][None, :]
    bias_a = rel_bias[:, :A_HEADS]
    bias_b = rel_bias[:, A_HEADS:]

    assert TM_TOKENS % tkb == 0 and TM_TOKENS % tkb_diff == 0, "value chunks must tile the projection's token tile"
    qa, ka, qi, kiw, qb, kb, va_ch, vb_ch, ka_t, va_t, ki_t, kb_t, vb_c = _norm_proj(
        x_prompt.reshape(b * s, d), g_mix, w_in_p, gmat64, hgain, prompt_segs, TM_TOKENS, "proj_prompt", seq=s)

    def from_feature_major(a_t, shape):
        nd = len(shape)
        return jnp.transpose(a_t.reshape((b,) + shape + (s,)), (0, nd + 1) + tuple(range(1, nd + 1)))[None]
    r3 = lambda a: a.reshape(b, s, a.shape[-1])
    mix_a = _dsa_prompt(r3(qi), r3(kiw), r3(qa), r3(ka), va_ch, _prompt_bias_tables(bias_a), tkb)
    mix_b = _diff_prompt(r3(qb), r3(kb), vb_ch, _prompt_bias_tables(bias_b), lam_p, subln, lam_init, tkb_diff)
    mk, mv = _norm_proj(mem_prompt.reshape(b * m_tok, d), norm_mem_src[layer][None, :], w_kv_b, gmat128, kv_gain,
                        ((0, MEM_W, True, MEM_W, MEM_HEADS), (MEM_W, MEM_W, False, MEM_W, MEM_HEADS)), 256, "mem_kv")
    mem_rows = lambda a: a.reshape(-1, m_tok * MEM_HEADS, MEM_HEAD_DIM)
    h2 = _mid(x_prompt, mix_a, mix_b, mem_rows(mk), mem_rows(mv), w_out_b, g_memx, w_q_b, gmat128, q_gain, w_o_b,
              TM_TOKENS, 1)
    yp, tail = _ffn_prompt(h2, g_ffn, w_gate_b, w_up_b, conv_w, conv_b, w_down_b, TM_TOKENS)
    conv_p = tail[:, SUBLANES - (CONV_W - 1):, :]

    xs = _pad_rows(x_sample, T_PAD)
    qa, ka_s, va_s, qi, kiw, ki_s, qb, kb_s, vb_s = _norm_proj(
        xs.reshape(db * T_PAD, d), g_mix, w_in_p, gmat64, hgain, proj_segs, db * T_PAD, "proj_sample")
    r3 = lambda a: a.reshape(db, T_PAD, a.shape[-1])
    idx_kt = jnp.transpose(cache_idx_k[layer], (0, 2, 1))
    a_kt = jnp.transpose(cache_a_k[layer], (0, 2, 3, 1)).reshape(n_pool, A_KV, TK)
    a_vt = jnp.transpose(cache_a_v[layer], (0, 2, 3, 1)).reshape(n_pool, A_KV, TK)
    b_kt = jnp.transpose(cache_b_k[layer], (0, 2, 3, 4, 1)).reshape(n_pool, B_QK, TK)
    b_v2 = cache_b_v[layer].reshape(n_pool, TK * B_HEADS, B_VH)
    n_pages = page_table.shape[1]
    qi_rows = r3(qi).reshape(db, T_PAD, IDX_HEADS, IDX_DIM).transpose(0, 2, 1, 3).reshape(db, IDX_HEADS * T_PAD, IDX_DIM)
    wi_rows = r3(kiw)[:, :, IDX_DIM:IDX_DIM + IDX_HEADS].transpose(0, 2, 1).reshape(db, IDX_HEADS * T_PAD, 1)
    selb = _dsa_sample_select(page_table, qi_rows, wi_rows, _new_keys_t(r3(ki_s), TK), idx_kt, t_new,
                              math.gcd(n_pages, SELECT_PAGES_PER_STEP))
    q_rows = r3(qa).reshape(db, T_PAD, A_HEADS, A_HEAD_DIM).transpose(0, 2, 1, 3)
    eye_g = jnp.repeat(jnp.eye(A_KV_HEADS, dtype=F32), A_REP, axis=0)
    qa_bd = (q_rows[:, :, :, None, :] * eye_g[None, :, None, :, None]).reshape(db, A_HEADS * T_PAD, A_KV)
    mix_a = _dsa_sample_attn(page_table, qa_bd, _new_keys_t(r3(ka_s), TK), _new_keys_t(r3(va_s), TK), selb,
                             _sample_bias_tables(bias_a, 1), a_kt, a_vt)
    n_str = 2 * B_HEADS
    q_rows = r3(qb).reshape(db, T_PAD, n_str, B_HEAD_DIM).transpose(0, 2, 1, 3)
    qb_bd = (q_rows[:, :, :, None, :] * jnp.eye(n_str, dtype=F32)[None, :, None, :, None]).reshape(
        db, n_str * T_PAD, B_QK)
    mix_b = _diff_sample(page_table, qb_bd, _new_keys_t(r3(kb_s), TK), _pad_rows(r3(vb_s), TK),
                         _sample_bias_tables(bias_b, 2), lam_p, subln, b_kt, b_v2, lam_init,
                         math.gcd(n_pages, DIFF_PAGES_PER_STEP))
    h2 = _mid(xs, mix_a, mix_b, mem_rows(cache_mem_k[layer]), mem_rows(cache_mem_v[layer]), w_out_b, g_memx, w_q_b,
              gmat128, q_gain, w_o_b, T_PAD, math.gcd(db, MID_SEQS_PER_STEP))
    state = state_ffn_conv[layer]
    st1 = _pad_rows(state[:, 1:2, :], T_PAD).reshape(db * T_PAD, f)
    st2 = _pad_rows(state, T_PAD).reshape(db * T_PAD, f)
    ys, g_s = _ffn_sample(h2.reshape(db * T_PAD, d), g_ffn, w_gate_b, w_up_b, conv_w, conv_b, w_down_b, st1, st2)
    ys = ys.reshape(db, T_PAD, d)[:, :t_new]
    conv_s = g_s.reshape(db, T_PAD, f)[:, t_new - (CONV_W - 1):t_new]

    def new_rows(a, shape):
        return a.reshape(db, T_PAD, -1)[:, :t_new].reshape((1, db, t_new) + shape)

    return (yp, ys,
            from_feature_major(ka_t, (A_KV_HEADS, A_HEAD_DIM)), from_feature_major(va_t, (A_KV_HEADS, A_HEAD_DIM)),
            from_feature_major(ki_t, (IDX_DIM,)), from_feature_major(kb_t, (B_HEADS, 2, B_HEAD_DIM)),
            vb_c.reshape(1, b, s, B_HEADS, 2 * B_HEAD_DIM),
            mk.reshape(1, b, m_tok, MEM_HEADS, MEM_HEAD_DIM), mv.reshape(1, b, m_tok, MEM_HEADS, MEM_HEAD_DIM),
            conv_p[None],
            new_rows(ka_s, (A_KV_HEADS, A_HEAD_DIM)), new_rows(va_s, (A_KV_HEADS, A_HEAD_DIM)),
            new_rows(ki_s, (IDX_DIM,)), new_rows(kb_s, (B_HEADS, 2, B_HEAD_DIM)),
            new_rows(vb_s, (B_HEADS, 2 * B_HEAD_DIM)), conv_s[None])
```
